```python
import math
import jax
import jax.numpy as jnp
from jax import lax
import numpy as np

D_MODEL = 1024
BATCH = 8
SEQ = 2048
DEPTH = 2

N_EVEN = (DEPTH + 1) // 2
N_ODD = DEPTH // 2
NORM_EPS = 1e-6
HEAD_DIM = 64
ROPE_DIM = HEAD_DIM // 4
ROPE_THETA = 500000.0
MASK_NEG = -1e30

RWKV_HEADS = 8
RWKV_WIDTH = RWKV_HEADS * HEAD_DIM
RWKV_DECAY_RANK = 32
RWKV_A_RANK = 32
RWKV_GATE_RANK = 96
RWKV_GN_EPS = 64e-5
RWKV_COLS = 3 * RWKV_WIDTH + RWKV_DECAY_RANK + RWKV_A_RANK + RWKV_GATE_RANK
RWKV_SPLITS = [RWKV_WIDTH, 2 * RWKV_WIDTH, 3 * RWKV_WIDTH, 3 * RWKV_WIDTH + RWKV_DECAY_RANK, 3 * RWKV_WIDTH + RWKV_DECAY_RANK + RWKV_A_RANK]

NSA_Q_HEADS = 8
NSA_KV_HEADS = 2
NSA_GROUP = NSA_Q_HEADS // NSA_KV_HEADS
NSA_WIDTH = NSA_Q_HEADS * HEAD_DIM
NSA_KV_WIDTH = NSA_KV_HEADS * HEAD_DIM
NSA_COLS = NSA_WIDTH + 6 * NSA_KV_WIDTH + 3 * NSA_Q_HEADS
NSA_SPLITS = [NSA_WIDTH + i * NSA_KV_WIDTH for i in range(7)]
CMP_BLOCK = 32
CMP_STRIDE = 16
CMP_HIDDEN = 2 * HEAD_DIM
SLC_BLOCK = 64
SLC_TOPN = 16
SLC_Q_CHUNK = 64
WINDOW = 512
WIN_BLOCK = 128
FORCED_SCORE = 1e6

HYB_COLS = RWKV_COLS + NSA_COLS
MIX_WIDTH = RWKV_WIDTH + NSA_WIDTH

SSM_D_INNER = 2 * D_MODEL
SSM_HEAD_DIM = 64
SSM_HEADS = SSM_D_INNER // SSM_HEAD_DIM
SSM_GROUPS = 4
SSM_STATE = 128
SSM_CONV = 4
SSM_CHUNK = 256
SSM_NORM_EPS = 1e-5
SSM_CONV_CH = SSM_D_INNER + 2 * SSM_GROUPS * SSM_STATE
SSM_IN_COLS = SSM_D_INNER + SSM_CONV_CH + SSM_HEADS

MOE_GROUPS = 4
MOE_EPG = 8
MOE_EXPERTS = MOE_GROUPS * MOE_EPG
MOE_TOPK = 2
MOE_HIDDEN = 512

kernel_name = 'hybrid_rwkv7_nsa_mamba2_hmoe'


def rms_norm(x, g, eps=NORM_EPS):
    xf = x.astype(jnp.float32)
    y = xf * lax.rsqrt(jnp.mean(xf * xf, -1, keepdims=True) + eps)
    return (y * g.astype(jnp.float32)).astype(x.dtype)


def partial_rope(x, pos):
    half = ROPE_DIM // 2
    inv = jnp.power(ROPE_THETA, -jnp.arange(half, dtype=jnp.float32) * (2.0 / ROPE_DIM))
    ang = pos.astype(jnp.float32)[:, None] * inv[None, :]
    cos = jnp.cos(ang)[:, None, :]
    sin = jnp.sin(ang)[:, None, :]
    xr = x[..., :ROPE_DIM].astype(jnp.float32)
    x1, x2 = xr[..., :half], xr[..., half:]
    rot = jnp.concatenate([x1 * cos - x2 * sin, x2 * cos + x1 * sin], -1).astype(x.dtype)
    return jnp.concatenate([rot, x[..., ROPE_DIM:]], -1)


def token_shift(z):
    return jnp.pad(z, ((0, 0), (1, 0), (0, 0)))[:, :-1]


def rwkv7_scan(r, w, k, v, a, b):
    f32 = jnp.float32
    xs = tuple(jnp.moveaxis(t.astype(f32), 1, 0) for t in (r, w, k, v, a, b))

    def step(state, inp):
        r_t, w_t, k_t, v_t, a_t, b_t = inp
        sa = jnp.einsum('bhij,bhj->bhi', state, a_t)
        state = state * w_t[:, :, None, :] + sa[..., None] * b_t[:, :, None, :] + v_t[..., None] * k_t[:, :, None, :]
        return state, jnp.einsum('bhij,bhj->bhi', state, r_t)

    b_, _, h_, n_ = r.shape
    s0 = jnp.zeros((b_, h_, n_, n_), f32)
    _, ys = lax.scan(step, s0, xs)
    return jnp.moveaxis(ys, 0, 1)


def rwkv7_mix(cols, mu, w0, w_up, a0, a_up, g_up, k_k, k_a, r_k, gn_w, gn_b):
    b_, s_, _ = cols.shape
    f32 = jnp.float32
    z = cols + (token_shift(cols) - cols) * mu
    r, k, v, w_lo, a_lo, g_lo = jnp.split(z, RWKV_SPLITS, axis=-1)
    w_log = -jax.nn.softplus(-(w0 + jnp.tanh(w_lo) @ w_up).astype(f32)) - 0.5
    a = jax.nn.sigmoid(a0 + a_lo @ a_up)
    g = jax.nn.sigmoid(g_lo) @ g_up
    heads = lambda t: t.reshape(b_, s_, RWKV_HEADS, HEAD_DIM)
    kk = heads(k * k_k).astype(f32)
    kk = kk * lax.rsqrt(jnp.maximum(jnp.sum(kk * kk, -1, keepdims=True), 1e-24))
    k = k * (1.0 + (a - 1.0) * k_a)
    r_h, k_h, v_h, a_h = heads(r), heads(k), heads(v), heads(a)
    decay = jnp.exp(-jnp.exp(heads(w_log)))
    y = rwkv7_scan(r_h, decay, k_h, v_h, -kk, kk * a_h)
    mean = jnp.mean(y, -1, keepdims=True)
    var = jnp.mean(jnp.square(y - mean), -1, keepdims=True)
    y = ((y - mean) * lax.rsqrt(var + RWKV_GN_EPS)).reshape(b_, s_, RWKV_WIDTH)
    y = y * gn_w + gn_b
    bonus = jnp.sum(r_h * k_h * r_k, -1, keepdims=True) * v_h
    y = y.astype(cols.dtype) + bonus.reshape(b_, s_, RWKV_WIDTH)
    return y * g


def nsa_mix(cols, pe_k, w1_k, b1_k, w2_k, pe_v, w1_v, b1_v, w2_v):
    b_, s_, _ = cols.shape
    f32 = jnp.float32
    hk, gq, d = NSA_KV_HEADS, NSA_GROUP, HEAD_DIM
    scale = HEAD_DIM ** -0.5
    q, kc, vc, ks, vs, kw, vw, gates = jnp.split(cols, NSA_SPLITS, axis=-1)
    pos = jnp.arange(s_)
    q = partial_rope(q.reshape(b_, s_, NSA_Q_HEADS, d), pos)
    q = q.reshape(b_, s_, hk, gq, d).transpose(0, 2, 3, 1, 4)
    kv_heads = lambda t: t.reshape(b_, s_, hk, d)

    n_cmp = (s_ - CMP_BLOCK) // CMP_STRIDE + 1
    blk_idx = np.arange(n_cmp)[:, None] * CMP_STRIDE + np.arange(CMP_BLOCK)[None, :]
    cmp_end = jnp.asarray(blk_idx[:, -1])

    def compress(t, pe, w1, b1, w2):
        blocks = kv_heads(t)[:, blk_idx] + pe[:, None, :]
        flat = blocks.transpose(0, 1, 3, 2, 4).reshape(b_, n_cmp, hk, CMP_BLOCK * d)
        return jax.nn.silu(flat @ w1 + b1) @ w2

    k_cmp = partial_rope(compress(kc, pe_k, w1_k, b1_k, w2_k), cmp_end).transpose(0, 2, 1, 3)
    v_cmp = compress(vc, pe_v, w1_v, b1_v, w2_v).transpose(0, 2, 1, 3)
    s_cmp = jnp.einsum('bhgsd,bhcd->bhgsc', q, k_cmp).astype(f32) * scale
    cmask = cmp_end[None, :] <= pos[:, None]
    p_cmp = jax.nn.softmax(jnp.where(cmask, s_cmp, MASK_NEG), -1) * cmask
    o_cmp = jnp.einsum('bhgsc,bhcd->bhgsd', p_cmp.astype(q.dtype), v_cmp)

    n_slc = s_ // SLC_BLOCK
    top_n = min(SLC_TOPN, n_slc)
    overlap = np.any((blk_idx[:, :, None] // SLC_BLOCK) == np.arange(n_slc)[None, None, :], axis=1).astype(np.float32)
    imp = jnp.einsum('bhgsc,cj->bhsj', p_cmp, jnp.asarray(overlap))
    blk = jnp.arange(n_slc)[None, :]
    cur = (pos // SLC_BLOCK)[:, None]
    valid = blk <= cur
    forced = (blk == 0) | (blk == cur) | (blk == cur - 1)
    imp = jnp.where(forced, FORCED_SCORE, jnp.where(valid, imp, -FORCED_SCORE))
    _, sel = lax.top_k(imp, top_n)

    k_slc = partial_rope(kv_heads(ks), pos).transpose(0, 2, 1, 3).reshape(b_, hk, n_slc, SLC_BLOCK, d)
    v_slc = kv_heads(vs).transpose(0, 2, 1, 3).reshape(b_, hk, n_slc, SLC_BLOCK, d)
    n_qc = s_ // SLC_Q_CHUNK
    q_ch = q.reshape(b_, hk, gq, n_qc, SLC_Q_CHUNK, d).transpose(3, 0, 1, 2, 4, 5)
    sel_ch = sel.reshape(b_, hk, n_qc, SLC_Q_CHUNK, top_n).transpose(2, 0, 1, 3, 4)
    pos_ch = pos.reshape(n_qc, SLC_Q_CHUNK)
    gather = jax.vmap(jax.vmap(lambda blocks, ix: blocks[ix]))

    def sel_block(args):
        q_c, ix, t = args
        kg = gather(k_slc, ix)
        vg = gather(v_slc, ix)
        s = jnp.einsum('bhgqd,bhqnld->bhgqnl', q_c, kg).astype(f32) * scale
        tok = ix[..., None] * SLC_BLOCK + jnp.arange(SLC_BLOCK)
        m = (tok <= t[None, None, :, None, None])[:, :, None]
        s = jnp.where(m, s, MASK_NEG)
        p = jax.nn.softmax(s.reshape(s.shape[:4] + (-1,)), -1).reshape(s.shape)
        return jnp.einsum('bhgqnl,bhqnld->bhgqd', p.astype(q_c.dtype), vg)

    o_slc = lax.map(sel_block, (q_ch, sel_ch, pos_ch))
    o_slc = o_slc.transpose(1, 2, 3, 0, 4, 5).reshape(b_, hk, gq, s_, d)

    n_wb = s_ // WIN_BLOCK
    n_back = WINDOW // WIN_BLOCK
    k_win = partial_rope(kv_heads(kw), pos).transpose(0, 2, 1, 3)
    v_win = kv_heads(vw).transpose(0, 2, 1, 3)

    def band(t):
        tp = jnp.pad(t, ((0, 0), (0, 0), (WINDOW, 0), (0, 0))).reshape(b_, hk, n_wb + n_back, WIN_BLOCK, d)
        return jnp.concatenate([tp[:, :, j:j + n_wb] for j in range(n_back + 1)], axis=3)

    kb, vb = band(k_win), band(v_win)
    qb = q.reshape(b_, hk, gq, n_wb, WIN_BLOCK, d)
    s_w = jnp.einsum('bhgiqd,bhikd->bhgiqk', qb, kb).astype(f32) * scale
    tq = jnp.arange(n_wb)[:, None] * WIN_BLOCK + jnp.arange(WIN_BLOCK)[None, :]
    tk = (jnp.arange(n_wb)[:, None] - n_back) * WIN_BLOCK + jnp.arange((n_back + 1) * WIN_BLOCK)[None, :]
    wm = (tk[:, None, :] >= 0) & (tk[:, None, :] <= tq[:, :, None]) & (tq[:, :, None] - tk[:, None, :] < WINDOW)
    p_w = jax.nn.softmax(jnp.where(wm, s_w, MASK_NEG), -1)
    o_win = jnp.einsum('bhgiqk,bhikd->bhgiqd', p_w.astype(q.dtype), vb).reshape(b_, hk, gq, s_, d)

    gate = jax.nn.sigmoid(gates.reshape(b_, s_, NSA_Q_HEADS, 3))
    to_bshd = lambda o: o.transpose(0, 3, 1, 2, 4).reshape(b_, s_, NSA_Q_HEADS, d)
    o = gate[..., 0:1] * to_bshd(o_cmp) + gate[..., 1:2] * to_bshd(o_slc) + gate[..., 2:3] * to_bshd(o_win)
    return o.reshape(b_, s_, NSA_WIDTH)


def segsum(a):
    t_ = a.shape[-1]
    cs = jnp.cumsum(a, -1)
    diff = cs[..., :, None] - cs[..., None, :]
    return jnp.where(jnp.tril(jnp.ones((t_, t_), bool)), diff, -jnp.inf)


def ssd_chunked(x, dt, a_head, bm, cm):
    f32 = jnp.float32
    b_, s_, h_, p_ = x.shape
    pad = (-s_) % SSM_CHUNK
    lc = SSM_CHUNK
    nc = (s_ + pad) // lc
    g_, n_ = SSM_GROUPS, SSM_STATE
    e_ = h_ // g_
    padt = lambda t: jnp.pad(t.astype(f32), ((0, 0), (0, pad)) + ((0, 0),) * (t.ndim - 2))
    xc = padt(x.astype(f32) * dt[..., None]).reshape(b_, nc, lc, g_, e_, p_)
    ad = padt(dt * a_head).reshape(b_, nc, lc, g_, e_).transpose(0, 3, 4, 1, 2)
    bc = padt(bm).reshape(b_, nc, lc, g_, n_)
    cc = padt(cm).reshape(b_, nc, lc, g_, n_)
    a_cs = jnp.cumsum(ad, -1)
    lmat = jnp.exp(segsum(ad))
    cb = jnp.einsum('bclgn,bcsgn->bgcls', cc, bc)
    y_diag = jnp.einsum('bgecls,bcsgep->bclgep', cb[:, :, None] * lmat, xc)
    decay_states = jnp.exp(a_cs[..., -1:] - a_cs).transpose(0, 3, 4, 1, 2)
    states = jnp.einsum('bclgn,bclgep->bcgepn', bc, xc * decay_states[..., None])
    states = jnp.concatenate([jnp.zeros_like(states[:, :1]), states], 1)
    chunk_tot = jnp.pad(a_cs[..., -1], ((0, 0), (0, 0), (0, 0), (1, 0)))
    decay_chunk = jnp.exp(segsum(chunk_tot))
    states = jnp.einsum('bgezc,bcgepn->bzgepn', decay_chunk, states)[:, :-1]
    y_off = jnp.einsum('bclgn,bcgepn->bclgep', cc, states) * jnp.exp(a_cs).transpose(0, 3, 4, 1, 2)[..., None]
    return (y_diag + y_off).reshape(b_, nc * lc, h_, p_)[:, :s_]


def mamba2_mix(u, w_in, conv_w, conv_b, dt_bias, a_log, d_skip, norm_w, w_out):
    b_, s_, _ = u.shape
    f32 = jnp.float32
    zxbcdt = u @ w_in
    z, xbc, dt = jnp.split(zxbcdt, [SSM_D_INNER, SSM_D_INNER + SSM_CONV_CH], axis=-1)
    xbc = lax.conv_general_dilated(xbc, conv_w[:, None, :], window_strides=(1,), padding=[(SSM_CONV - 1, 0)],
                                   dimension_numbers=('NWC', 'WIO', 'NWC'), feature_group_count=SSM_CONV_CH)
    xbc = jax.nn.silu(xbc + conv_b)
    xs, bm, cm = jnp.split(xbc, [SSM_D_INNER, SSM_D_INNER + SSM_GROUPS * SSM_STATE], axis=-1)
    dt = jax.nn.softplus((dt + dt_bias).astype(f32))
    a_head = -jnp.exp(a_log.astype(f32))
    xh = xs.reshape(b_, s_, SSM_HEADS, SSM_HEAD_DIM)
    y = ssd_chunked(xh, dt, a_head,
                    bm.reshape(b_, s_, SSM_GROUPS, SSM_STATE), cm.reshape(b_, s_, SSM_GROUPS, SSM_STATE))
    y = y + xh.astype(f32) * d_skip.astype(f32)[:, None]
    y = y.reshape(b_, s_, SSM_D_INNER) * jax.nn.silu(z.astype(f32))
    yg = y.reshape(b_, s_, SSM_GROUPS, SSM_D_INNER // SSM_GROUPS)
    yg = yg * lax.rsqrt(jnp.mean(yg * yg, -1, keepdims=True) + SSM_NORM_EPS)
    y = yg.reshape(b_, s_, SSM_D_INNER) * norm_w.astype(f32)
    return y.astype(u.dtype) @ w_out


def hier_moe(u, wg, bg, we, be, w_gate, w_up, w_down):
    b_, s_, d_ = u.shape
    f32 = jnp.float32
    t = u.reshape(-1, d_)
    pg = jax.nn.softmax((t @ wg + bg).astype(f32), -1)
    g_val, g_idx = lax.top_k(pg, 1)
    le = (t @ we + be).astype(f32).reshape(-1, MOE_GROUPS, MOE_EPG)
    le_sel = jnp.take_along_axis(le, g_idx[:, :, None], 1)[:, 0]
    pe = jax.nn.softmax(le_sel, -1)
    e_val, e_idx = lax.top_k(pe, MOE_TOPK)
    e_w = e_val / jnp.sum(e_val, -1, keepdims=True) * g_val
    expert_id = g_idx * MOE_EPG + e_idx
    comb = jnp.sum(jax.nn.one_hot(expert_id, MOE_EXPERTS, dtype=f32) * e_w[..., None], 1)
    comb = comb.astype(u.dtype).reshape(-1, MOE_GROUPS, MOE_EPG)
    out = jnp.zeros_like(t)
    for grp in range(MOE_GROUPS):
        sl = slice(grp * MOE_EPG, (grp + 1) * MOE_EPG)
        act = jax.nn.silu(jnp.einsum('td,edf->tef', t, w_gate[sl])) * jnp.einsum('td,edf->tef', t, w_up[sl])
        out = out + jnp.einsum('tef,efd->td', act * comb[:, grp, :, None], w_down[sl])
    return out.reshape(b_, s_, d_)


def setup_inputs(seed: int = 0) -> dict:
    key = jax.random.key(seed)
    keys = iter(jax.random.split(key, 64))
    f32 = jnp.float32

    def nrm(shape, scale):
        return jax.random.normal(next(keys), shape, f32) * scale

    def unif(shape, lo, hi):
        return jax.random.uniform(next(keys), shape, f32, lo, hi)

    ne, no, nl = N_EVEN, N_ODD, DEPTH
    dt0 = jnp.exp(unif((no, SSM_HEADS), math.log(1e-3), math.log(1e-1)))
    return {
        'x': nrm((BATCH, SEQ, D_MODEL), 1.0),
        'norm_mix': 1.0 + nrm((nl, D_MODEL), 0.02),
        'norm_ffn': 1.0 + nrm((nl, D_MODEL), 0.02),
        'norm_final': 1.0 + nrm((D_MODEL,), 0.02),
        'hyb_w_in': nrm((ne, D_MODEL, HYB_COLS), D_MODEL ** -0.5),
        'hyb_w_out': nrm((ne, MIX_WIDTH, D_MODEL), MIX_WIDTH ** -0.5),
        'rwkv_mu': unif((ne, RWKV_COLS), 0.0, 1.0),
        'rwkv_w0': unif((ne, RWKV_WIDTH), -5.0, 0.0),
        'rwkv_w_up': nrm((ne, RWKV_DECAY_RANK, RWKV_WIDTH), 0.1),
        'rwkv_a0': nrm((ne, RWKV_WIDTH), 0.1),
        'rwkv_a_up': nrm((ne, RWKV_A_RANK, RWKV_WIDTH), 0.5 * RWKV_A_RANK ** -0.5),
        'rwkv_g_up': nrm((ne, RWKV_GATE_RANK, RWKV_WIDTH), RWKV_GATE_RANK ** -0.5),
        'rwkv_k_k': 0.85 + nrm((ne, RWKV_WIDTH), 0.05),
        'rwkv_k_a': 1.0 + nrm((ne, RWKV_WIDTH), 0.05),
        'rwkv_r_k': nrm((ne, RWKV_HEADS, HEAD_DIM), 0.1),
        'rwkv_gn_w': 1.0 + nrm((ne, RWKV_WIDTH), 0.02),
        'rwkv_gn_b': nrm((ne, RWKV_WIDTH), 0.02),
        'nsa_pe_k': nrm((ne, CMP_BLOCK, HEAD_DIM), 0.1),
        'nsa_w1_k': nrm((ne, CMP_BLOCK * HEAD_DIM, CMP_HIDDEN), (CMP_BLOCK * HEAD_DIM) ** -0.5),
        'nsa_b1_k': nrm((ne, CMP_HIDDEN), 0.02),
        'nsa_w2_k': nrm((ne, CMP_HIDDEN, HEAD_DIM), CMP_HIDDEN ** -0.5),
        'nsa_pe_v': nrm((ne, CMP_BLOCK, HEAD_DIM), 0.1),
        'nsa_w1_v': nrm((ne, CMP_BLOCK * HEAD_DIM, CMP_HIDDEN), (CMP_BLOCK * HEAD_DIM) ** -0.5),
        'nsa_b1_v': nrm((ne, CMP_HIDDEN), 0.02),
        'nsa_w2_v': nrm((ne, CMP_HIDDEN, HEAD_DIM), CMP_HIDDEN ** -0.5),
        'ssm_w_in': nrm((no, D_MODEL, SSM_IN_COLS), D_MODEL ** -0.5),
        'ssm_conv_w': nrm((no, SSM_CONV, SSM_CONV_CH), 0.5),
        'ssm_conv_b': nrm((no, SSM_CONV_CH), 0.02),
        'ssm_dt_bias': dt0 + jnp.log(-jnp.expm1(-dt0)),
        'ssm_a_log': jnp.log(unif((no, SSM_HEADS), 1.0, 16.0)),
        'ssm_d': 1.0 + nrm((no, SSM_HEADS), 0.1),
        'ssm_norm_w': 1.0 + nrm((no, SSM_D_INNER), 0.02),
        'ssm_w_out': nrm((no, SSM_D_INNER, D_MODEL), SSM_D_INNER ** -0.5),
        'moe_wg': nrm((nl, D_MODEL, MOE_GROUPS), D_MODEL ** -0.5),
        'moe_bg': nrm((nl, MOE_GROUPS), 0.01),
        'moe_we': nrm((nl, D_MODEL, MOE_EXPERTS), D_MODEL ** -0.5),
        'moe_be': nrm((nl, MOE_EXPERTS), 0.01),
        'moe_w_gate': nrm((nl, MOE_EXPERTS, D_MODEL, MOE_HIDDEN), D_MODEL ** -0.5),
        'moe_w_up': nrm((nl, MOE_EXPERTS, D_MODEL, MOE_HIDDEN), D_MODEL ** -0.5),
        'moe_w_down': nrm((nl, MOE_EXPERTS, MOE_HIDDEN, D_MODEL), MOE_HIDDEN ** -0.5),
    }


def reference(x, norm_mix, norm_ffn, norm_final, hyb_w_in, hyb_w_out,
              rwkv_mu, rwkv_w0, rwkv_w_up, rwkv_a0, rwkv_a_up, rwkv_g_up, rwkv_k_k, rwkv_k_a, rwkv_r_k,
              rwkv_gn_w, rwkv_gn_b,
              nsa_pe_k, nsa_w1_k, nsa_b1_k, nsa_w2_k, nsa_pe_v, nsa_w1_v, nsa_b1_v, nsa_w2_v,
              ssm_w_in, ssm_conv_w, ssm_conv_b, ssm_dt_bias, ssm_a_log, ssm_d, ssm_norm_w, ssm_w_out,
              moe_wg, moe_bg, moe_we, moe_be, moe_w_gate, moe_w_up, moe_w_down):
    h = x
    for layer in range(DEPTH):
        u = rms_norm(h, norm_mix[layer])
        i = layer // 2
        if layer % 2 == 0:
            cols = u @ hyb_w_in[i]
            y_rwkv = rwkv7_mix(cols[..., :RWKV_COLS], rwkv_mu[i], rwkv_w0[i], rwkv_w_up[i], rwkv_a0[i],
                               rwkv_a_up[i], rwkv_g_up[i], rwkv_k_k[i], rwkv_k_a[i], rwkv_r_k[i],
                               rwkv_gn_w[i], rwkv_gn_b[i])
            y_nsa = nsa_mix(cols[..., RWKV_COLS:], nsa_pe_k[i], nsa_w1_k[i], nsa_b1_k[i], nsa_w2_k[i],
                            nsa_pe_v[i], nsa_w1_v[i], nsa_b1_v[i], nsa_w2_v[i])
            mix = jnp.concatenate([y_rwkv, y_nsa], -1) @ hyb_w_out[i]
        else:
            mix = mamba2_mix(u, ssm_w_in[i], ssm_conv_w[i], ssm_conv_b[i], ssm_dt_bias[i], ssm_a_log[i],
                             ssm_d[i], ssm_norm_w[i], ssm_w_out[i])
        h = h + mix
        h = h + hier_moe(rms_norm(h, norm_ffn[layer]), moe_wg[layer], moe_bg[layer], moe_we[layer],
                         moe_be[layer], moe_w_gate[layer], moe_w_up[layer], moe_w_down[layer])
    return rms_norm(h, norm_final)
```

```python
import functools
import math

import numpy as np
import jax
import jax.numpy as jnp
from jax import lax
from jax.experimental import pallas as pl
from jax.experimental.pallas import tpu as pltpu

F32 = jnp.float32
BF16 = jnp.bfloat16

D_MODEL = 1024
NORM_EPS = 1e-6
HEAD_DIM = 64
ROPE_DIM = 16
ROPE_THETA = 500000.0
MASK_NEG = -1e30

VMEM_LIMIT = 48 * 1024 * 1024


def _cp(sem, vmem=VMEM_LIMIT):
    return pltpu.CompilerParams(dimension_semantics=sem, vmem_limit_bytes=vmem)


def _dg(a, b, ca=1, cb=0):
    return lax.dot_general(a, b, (((ca,), (cb,)), ((), ())), preferred_element_type=F32)


def _split2(x):
    hi = x.astype(BF16)
    lo = (x - hi.astype(F32)).astype(BF16)
    return hi, lo


def _split3(x):
    hi = x.astype(BF16)
    r = x - hi.astype(F32)
    mid = r.astype(BF16)
    lo = (r - mid.astype(F32)).astype(BF16)
    return hi, mid, lo


def _dot1(a, b, ca=1, cb=0):
    return _dg(a.astype(BF16), b.astype(BF16), ca, cb)


def _dot3(a, b, ca=1, cb=0):
    ah, al = _split2(a)
    bh, bl = _split2(b)
    return _dg(ah, bh, ca, cb) + (_dg(ah, bl, ca, cb) + _dg(al, bh, ca, cb))


def _dot_xl(a, b01, ca=1, cb=0):
    h, m, l = _split3(a)
    return _dg(h, b01, ca, cb) + (_dg(m, b01, ca, cb) + _dg(l, b01, ca, cb))


def _dot_xr(a01, b, ca=1, cb=0):
    h, m, l = _split3(b)
    return _dg(a01, h, ca, cb) + (_dg(a01, m, ca, cb) + _dg(a01, l, ca, cb))


def _sigmoid(x):
    return 1.0 / (1.0 + jnp.exp(-x))


def _silu(x):
    return x * _sigmoid(x)


def _softplus(x):
    return jnp.maximum(x, 0.0) + jnp.log(1.0 + jnp.exp(-jnp.abs(x)))


def _norm_mm_kernel(x_ref, g_ref, w_ref, o_ref, xn_ref):
    @pl.when(pl.program_id(1) == 0)
    def _():
        x = x_ref[...]
        ms = jnp.mean(x * x, axis=-1, keepdims=True)
        xn_ref[...] = (x * lax.rsqrt(ms + NORM_EPS) * g_ref[...]).astype(BF16)

    o_ref[...] = jnp.dot(xn_ref[...], w_ref[...], preferred_element_type=F32)


def norm_matmul(x, g, w_bf16, tm, tn):
    t, d = x.shape
    n = w_bf16.shape[1]
    return pl.pallas_call(
        _norm_mm_kernel,
        grid=(t // tm, n // tn),
        in_specs=[
            pl.BlockSpec((tm, d), lambda i, j: (i, 0)),
            pl.BlockSpec((1, d), lambda i, j: (0, 0)),
            pl.BlockSpec((d, tn), lambda i, j: (0, j)),
        ],
        out_specs=pl.BlockSpec((tm, tn), lambda i, j: (i, j)),
        out_shape=jax.ShapeDtypeStruct((t, n), F32),
        scratch_shapes=[pltpu.VMEM((tm, d), BF16)],
        compiler_params=_cp(("parallel", "arbitrary")),
        name="norm_matmul",
    )(x, g.reshape(1, d), w_bf16)


def _mm_res_kernel(a_ref, w_ref, r_ref, o_ref):
    o_ref[...] = r_ref[...] + jnp.dot(a_ref[...].astype(BF16), w_ref[...], preferred_element_type=F32)


def matmul_residual(a, w_bf16, res, tm, tn):
    t, k = a.shape
    n = w_bf16.shape[1]
    return pl.pallas_call(
        _mm_res_kernel,
        grid=(t // tm, n // tn),
        in_specs=[
            pl.BlockSpec((tm, k), lambda i, j: (i, 0)),
            pl.BlockSpec((k, tn), lambda i, j: (0, j)),
            pl.BlockSpec((tm, tn), lambda i, j: (i, j)),
        ],
        out_specs=pl.BlockSpec((tm, tn), lambda i, j: (i, j)),
        out_shape=jax.ShapeDtypeStruct((t, n), F32),
        compiler_params=_cp(("parallel", "arbitrary")),
        name="matmul_residual",
    )(a, w_bf16, res)


def _final_norm_kernel(x_ref, g_ref, o_ref):
    x = x_ref[...]
    ms = jnp.mean(x * x, axis=-1, keepdims=True)
    o_ref[...] = x * lax.rsqrt(ms + NORM_EPS) * g_ref[...]


def final_norm(x, g, tm=1024):
    t, d = x.shape
    return pl.pallas_call(
        _final_norm_kernel,
        grid=(t // tm,),
        in_specs=[pl.BlockSpec((tm, d), lambda i: (i, 0)), pl.BlockSpec((1, d), lambda i: (0, 0))],
        out_specs=pl.BlockSpec((tm, d), lambda i: (i, 0)),
        out_shape=jax.ShapeDtypeStruct((t, d), F32),
        compiler_params=_cp(("parallel",)),
        name="final_norm",
    )(x, g.reshape(1, d))


MOE_GROUPS = 4
MOE_EPG = 8
MOE_EXPERTS = 32
MOE_HIDDEN = 512
MOE_TM = 256


def _router_kernel(x_ref, g_ref, w_ref, b_ref, xn_ref, ids_ref, wts_ref):
    x = x_ref[...]
    ms = jnp.mean(x * x, axis=-1, keepdims=True)
    u = x * lax.rsqrt(ms + NORM_EPS) * g_ref[...]
    xn_ref[...] = u.astype(BF16)
    lg = _dot3(u, w_ref[...]) + b_ref[...]
    tm = x.shape[0]
    lane = lax.broadcasted_iota(jnp.int32, (tm, 128), 1)
    gl = lg[:, :128]
    el = lg[:, 128:]
    gmask = lane < MOE_GROUPS
    gmax = jnp.max(jnp.where(gmask, gl, -jnp.inf), axis=-1, keepdims=True)
    ge = jnp.where(gmask, jnp.exp(gl - gmax), 0.0)
    pg = ge / jnp.sum(ge, axis=-1, keepdims=True)
    g_val = jnp.max(pg, axis=-1, keepdims=True)
    g_idx = jnp.min(jnp.where(gmask & (pg == g_val), lane, 128), axis=-1, keepdims=True)
    emask = (lane >= g_idx * MOE_EPG) & (lane < (g_idx + 1) * MOE_EPG)
    emax = jnp.max(jnp.where(emask, el, -jnp.inf), axis=-1, keepdims=True)
    ee = jnp.where(emask, jnp.exp(el - emax), 0.0)
    pe = ee / jnp.sum(ee, axis=-1, keepdims=True)
    v1 = jnp.max(jnp.where(emask, pe, -1.0), axis=-1, keepdims=True)
    i1 = jnp.min(jnp.where(emask & (pe == v1), lane, 128), axis=-1, keepdims=True)
    m2 = emask & (lane != i1)
    v2 = jnp.max(jnp.where(m2, pe, -1.0), axis=-1, keepdims=True)
    i2 = jnp.min(jnp.where(m2 & (pe == v2), lane, 128), axis=-1, keepdims=True)
    den = v1 + v2
    w1 = v1 / den * g_val
    w2 = v2 / den * g_val
    ids_ref[...] = jnp.where(lane == 0, i1, jnp.where(lane == 1, i2, 0))
    wts_ref[...] = jnp.where(lane == 0, w1, jnp.where(lane == 1, w2, 0.0))


def moe_router(h, g, wg, bg, we, be, tm=512):
    t, d = h.shape
    w = jnp.zeros((d, 256), F32).at[:, :MOE_GROUPS].set(wg).at[:, 128:128 + MOE_EXPERTS].set(we)
    b = jnp.zeros((1, 256), F32).at[0, :MOE_GROUPS].set(bg).at[0, 128:128 + MOE_EXPERTS].set(be)
    return pl.pallas_call(
        _router_kernel,
        grid=(t // tm,),
        in_specs=[
            pl.BlockSpec((tm, d), lambda i: (i, 0)),
            pl.BlockSpec((1, d), lambda i: (0, 0)),
            pl.BlockSpec((d, 256), lambda i: (0, 0)),
            pl.BlockSpec((1, 256), lambda i: (0, 0)),
        ],
        out_specs=[
            pl.BlockSpec((tm, d), lambda i: (i, 0)),
            pl.BlockSpec((tm, 128), lambda i: (i, 0)),
            pl.BlockSpec((tm, 128), lambda i: (i, 0)),
        ],
        out_shape=[
            jax.ShapeDtypeStruct((t, d), BF16),
            jax.ShapeDtypeStruct((t, 128), jnp.int32),
            jax.ShapeDtypeStruct((t, 128), F32),
        ],
        compiler_params=_cp(("parallel",)),
        name="moe_router",
    )(h, g.reshape(1, d), w, b)


def _moe_ffn_kernel(te_ref, tv_ref, x_ref, rw_ref, wg_ref, wu_ref, wd_ref, o_ref, wg_s, wu_s, wd_s):
    i = pl.program_id(0)
    prev = te_ref[jnp.maximum(i - 1, 0)]
    changed = (i == 0) | (te_ref[i] != prev)

    @pl.when(changed)
    def _():
        wg_s[...] = wg_ref[0].astype(BF16)
        wu_s[...] = wu_ref[0].astype(BF16)
        wd_s[...] = wd_ref[0].astype(BF16)

    @pl.when(tv_ref[i] > 0)
    def _():
        x = x_ref[...]
        gate = jnp.dot(x, wg_s[...], preferred_element_type=F32)
        up = jnp.dot(x, wu_s[...], preferred_element_type=F32)
        act = _silu(gate) * up * rw_ref[...]
        o_ref[...] = jnp.dot(act.astype(BF16), wd_s[...], preferred_element_type=F32)

    @pl.when(tv_ref[i] == 0)
    def _():
        o_ref[...] = jnp.zeros_like(o_ref)


def moe_ffn(xs, roww, tile_expert, tile_valid, w_gate, w_up, w_down):
    np_rows, d = xs.shape
    f = w_gate.shape[-1]
    tm = MOE_TM
    n_tiles = np_rows // tm
    grid_spec = pltpu.PrefetchScalarGridSpec(
        num_scalar_prefetch=2,
        grid=(n_tiles,),
        in_specs=[
            pl.BlockSpec((tm, d), lambda i, te, tv: (i, 0)),
            pl.BlockSpec((tm, 1), lambda i, te, tv: (i, 0)),
            pl.BlockSpec((1, d, f), lambda i, te, tv: (te[i], 0, 0)),
            pl.BlockSpec((1, d, f), lambda i, te, tv: (te[i], 0, 0)),
            pl.BlockSpec((1, f, d), lambda i, te, tv: (te[i], 0, 0)),
        ],
        out_specs=pl.BlockSpec((tm, d), lambda i, te, tv: (i, 0)),
        scratch_shapes=[pltpu.VMEM((d, f), BF16), pltpu.VMEM((d, f), BF16), pltpu.VMEM((f, d), BF16)],
    )
    return pl.pallas_call(
        _moe_ffn_kernel,
        grid_spec=grid_spec,
        out_shape=jax.ShapeDtypeStruct((np_rows, d), F32),
        compiler_params=_cp(("arbitrary",)),
        name="moe_ffn",
    )(tile_expert, tile_valid, xs, roww, w_gate, w_up, w_down)


def hier_moe_block(h, g, wg, bg, we, be, w_gate, w_up, w_down):
    t, d = h.shape
    tm = MOE_TM
    xn, ids, wts = moe_router(h, g, wg, bg, we, be)
    ids = ids[:, :2]
    wts = wts[:, :2]
    e_flat = ids.reshape(-1)
    n_asg = e_flat.shape[0]
    onehot = (e_flat[:, None] == jnp.arange(MOE_EXPERTS, dtype=jnp.int32)[None, :]).astype(jnp.int32)
    seen = jnp.cumsum(onehot, axis=0)
    counts = seen[-1]
    rank = jnp.sum((seen - onehot) * onehot, axis=1)
    padded = ((counts + tm - 1) // tm) * tm
    pad_end = jnp.cumsum(padded)
    pad_off = pad_end - padded
    dest = jnp.sum(onehot * pad_off[None, :], axis=1) + rank
    np_rows = n_asg + MOE_EXPERTS * tm
    asg = jnp.arange(n_asg, dtype=jnp.int32)
    row_tok = jnp.zeros((np_rows,), jnp.int32).at[dest].set(asg // 2)
    row_w = jnp.zeros((np_rows,), F32).at[dest].set(wts.reshape(-1))
    n_tiles = np_rows // tm
    tile_start = jnp.arange(n_tiles, dtype=jnp.int32) * tm
    tile_expert = jnp.minimum(jnp.sum((tile_start[:, None] >= pad_end[None, :]).astype(jnp.int32), axis=1),
                              MOE_EXPERTS - 1)
    tile_valid = (tile_start < pad_end[-1]).astype(jnp.int32)
    pos = dest.reshape(t, 2)
    xs = jnp.take(xn, row_tok, axis=0)
    y = moe_ffn(xs, row_w.reshape(np_rows, 1), tile_expert, tile_valid, w_gate, w_up, w_down)
    return h + jnp.take(y, pos[:, 0], axis=0) + jnp.take(y, pos[:, 1], axis=0)


SSM_D_INNER = 2048
SSM_HEADS = 32
SSM_GROUPS = 4
SSM_HPG = 8
SSM_STATE = 128
SSM_CONV = 4
SSM_CHUNK = 256
SSM_NORM_EPS = 1e-5
SSM_CONV_CH = SSM_D_INNER + 2 * SSM_GROUPS * SSM_STATE
SSM_IN_COLS = SSM_D_INNER + SSM_CONV_CH + SSM_HEADS
SSM_IN_PAD = 5376
SSM_GW = SSM_HPG * HEAD_DIM


def _conv_silu_kernel(x_ref, p_ref, w_ref, b_ref, o_ref, *, seq, tm):
    i = pl.program_id(0)
    x = x_ref[...]
    first = (i * tm) % seq == 0
    prev = jnp.where(first, 0.0, p_ref[...])
    xx = jnp.concatenate([prev, x], axis=0)
    w = w_ref[...]
    acc = b_ref[...] + w[3:4, :] * x
    for k in range(1, SSM_CONV):
        acc = acc + w[3 - k:4 - k, :] * xx[8 - k:8 - k + tm, :]
    o_ref[...] = _silu(acc)


def conv_silu(zx, conv_w, conv_b, seq, tm=512, tc=1024):
    t = zx.shape[0]
    c0 = SSM_D_INNER // tc
    kern = functools.partial(_conv_silu_kernel, seq=seq, tm=tm)
    return pl.pallas_call(
        kern,
        grid=(t // tm, SSM_CONV_CH // tc),
        in_specs=[
            pl.BlockSpec((tm, tc), lambda i, j: (i, c0 + j)),
            pl.BlockSpec((8, tc), lambda i, j: (jnp.maximum(i * (tm // 8) - 1, 0), c0 + j)),
            pl.BlockSpec((SSM_CONV, tc), lambda i, j: (0, j)),
            pl.BlockSpec((1, tc), lambda i, j: (0, j)),
        ],
        out_specs=pl.BlockSpec((tm, tc), lambda i, j: (i, j)),
        out_shape=jax.ShapeDtypeStruct((t, SSM_CONV_CH), F32),
        compiler_params=_cp(("parallel", "parallel")),
        name="conv_silu",
    )(zx, zx, conv_w, conv_b.reshape(1, -1))


def _ssd_kernel(x_ref, b_ref, c_ref, z_ref, dt_ref, dtt_ref, br_ref, bc_ref, ar_ref, ac_ref, d_ref, nw_ref,
                e8_ref, tri_ref, o_ref, h_ref):
    c = pl.program_id(2)
    lc = SSM_CHUNK

    @pl.when(c == 0)
    def _():
        h_ref[...] = jnp.zeros_like(h_ref)

    e8 = e8_ref[...]
    tri = tri_ref[...]
    x = x_ref[...]
    bm = b_ref[...]
    cm = c_ref[...]
    dt = _softplus(dt_ref[0, 0] + br_ref[0])
    dtt = _softplus(dtt_ref[0, 0] + bc_ref[0])
    a = dt * (-jnp.exp(ar_ref[0]))
    at = dtt * (-jnp.exp(ac_ref[0]))
    cs = _dot_xr(tri, a)
    cst = _dot_xl(at, tri, 1, 1)
    cs_last = cs[lc - 1:lc, :]

    xdt = x * _dot_xl(dt, e8)
    cb = _dot1(cm, bm, 1, 1)
    row = lax.broadcasted_iota(jnp.int32, (lc, lc), 0)
    col = lax.broadcasted_iota(jnp.int32, (lc, lc), 1)
    causal = row >= col
    ys = []
    for hh in range(SSM_HPG):
        lmat = jnp.where(causal, jnp.exp(cs[:, hh:hh + 1] - cst[hh:hh + 1, :]), 0.0)
        ys.append(_dot1(cb * lmat, xdt[:, hh * HEAD_DIM:(hh + 1) * HEAD_DIM]))
    y = jnp.concatenate(ys, axis=1)

    h_in = h_ref[...]
    y = y + _dot1(cm, h_in) * _dot_xl(jnp.exp(cs), e8)
    xdec = xdt * _dot_xl(jnp.exp(cs_last - cs), e8)
    h_ref[...] = h_in * _dot_xl(jnp.exp(cs_last), e8) + _dot1(bm, xdec, 0, 0)

    y = y + x * _dot_xl(d_ref[0], e8)
    zz = z_ref[...]
    y = y * _silu(zz)
    y = y * lax.rsqrt(jnp.mean(y * y, axis=-1, keepdims=True) + SSM_NORM_EPS) * nw_ref[...]
    o_ref[...] = y.astype(BF16)


def ssd_mix(zx, xbc, dt_bias, a_log, d_skip, norm_w, batch, seq):
    t = zx.shape[0]
    lc = SSM_CHUNK
    nc = seq // lc
    g, hp = SSM_GROUPS, SSM_HPG
    dt_raw = zx[:, SSM_D_INNER + SSM_CONV_CH:SSM_IN_COLS].reshape(batch, seq, g, hp)
    dt_g = dt_raw.transpose(0, 2, 1, 3)
    dt_t = dt_raw.transpose(0, 2, 3, 1)
    b_row = dt_bias.reshape(g, 1, hp)
    b_col = dt_bias.reshape(g, hp, 1)
    a_row = a_log.reshape(g, 1, hp)
    a_col = a_log.reshape(g, hp, 1)
    d_row = d_skip.reshape(g, 1, hp)
    e8 = jnp.asarray(np.kron(np.eye(hp), np.ones((1, HEAD_DIM))), BF16)
    tri = jnp.asarray(np.tril(np.ones((lc, lc))), BF16)
    xcol = SSM_D_INNER // SSM_STATE
    return pl.pallas_call(
        _ssd_kernel,
        grid=(batch, g, nc),
        in_specs=[
            pl.BlockSpec((lc, SSM_GW), lambda b, gg, c: (b * nc + c, gg)),
            pl.BlockSpec((lc, SSM_STATE), lambda b, gg, c: (b * nc + c, xcol + gg)),
            pl.BlockSpec((lc, SSM_STATE), lambda b, gg, c: (b * nc + c, xcol + g + gg)),
            pl.BlockSpec((lc, SSM_GW), lambda b, gg, c: (b * nc + c, gg)),
            pl.BlockSpec((1, 1, lc, hp), lambda b, gg, c: (b, gg, c, 0)),
            pl.BlockSpec((1, 1, hp, lc), lambda b, gg, c: (b, gg, 0, c)),
            pl.BlockSpec((1, 1, hp), lambda b, gg, c: (gg, 0, 0)),
            pl.BlockSpec((1, hp, 1), lambda b, gg, c: (gg, 0, 0)),
            pl.BlockSpec((1, 1, hp), lambda b, gg, c: (gg, 0, 0)),
            pl.BlockSpec((1, hp, 1), lambda b, gg, c: (gg, 0, 0)),
            pl.BlockSpec((1, 1, hp), lambda b, gg, c: (gg, 0, 0)),
            pl.BlockSpec((1, SSM_GW), lambda b, gg, c: (0, gg)),
            pl.BlockSpec((hp, SSM_GW), lambda b, gg, c: (0, 0)),
            pl.BlockSpec((lc, lc), lambda b, gg, c: (0, 0)),
        ],
        out_specs=pl.BlockSpec((lc, SSM_GW), lambda b, gg, c: (b * nc + c, gg)),
        out_shape=jax.ShapeDtypeStruct((t, SSM_D_INNER), BF16),
        scratch_shapes=[pltpu.VMEM((SSM_STATE, SSM_GW), F32)],
        compiler_params=_cp(("parallel", "parallel", "arbitrary")),
        name="ssd",
    )(xbc, xbc, xbc, zx, dt_g, dt_t, b_row, b_col, a_row, a_col, d_row, norm_w.reshape(1, -1), e8, tri)


def mamba2_block(h, g_norm, w_in, conv_w, conv_b, dt_bias, a_log, d_skip, norm_w, w_out, batch, seq):
    w_in_p = jnp.pad(w_in, ((0, 0), (0, SSM_IN_PAD - SSM_IN_COLS))).astype(BF16)
    zx = norm_matmul(h, g_norm, w_in_p, tm=1024, tn=768)
    xbc = conv_silu(zx, conv_w, conv_b, seq)
    y = ssd_mix(zx, xbc, dt_bias, a_log, d_skip, norm_w, batch, seq)
    return matmul_residual(y, w_out.astype(BF16), h, tm=1024, tn=512)


RWKV_HEADS = 8
RWKV_WIDTH = 512
RWKV_DECAY_RANK = 32
RWKV_A_RANK = 32
RWKV_GATE_RANK = 96
RWKV_GN_EPS = 64e-5
RWKV_COLS = 3 * RWKV_WIDTH + RWKV_DECAY_RANK + RWKV_A_RANK + RWKV_GATE_RANK
RWKV_COLS_PAD = 1792
RWKV_LO = 3 * RWKV_WIDTH
RWKV_LO_W = RWKV_COLS_PAD - RWKV_LO
RWKV_CHUNK = 64


def _rwkv_prep_kernel(c_ref, p_ref, mu_ref, w0_ref, wup_ref, a0_ref, aup_ref, gup_ref, kk_ref, ka_ref, rk_ref,
                      bd_ref, tri_ref,
                      rt_ref, at_ref, bt_ref, kt_ref, v_ref, bh_ref, kh_ref, pc_ref, g_ref, bonus_ref, *, seq, tm):
    i = pl.program_id(0)
    w = RWKV_WIDTH
    c = c_ref[...]
    first = (i * tm) % seq == 0
    prev_row = jnp.where(first, 0.0, p_ref[7:8, :])
    row = lax.broadcasted_iota(jnp.int32, c.shape, 0)
    shifted = jnp.where(row == 0, prev_row, pltpu.roll(c, 1, axis=0))
    z = c + (shifted - c) * mu_ref[...]
    r = z[:, 0:w]
    k = z[:, w:2 * w]
    v = z[:, 2 * w:3 * w]
    lo = z[:, RWKV_LO:RWKV_COLS_PAD]
    w_log = -_softplus(-(w0_ref[...] + _dot3(jnp.tanh(lo), wup_ref[...]))) - 0.5
    lw = -jnp.exp(w_log)
    a = _sigmoid(a0_ref[...] + _dot3(lo, aup_ref[...]))
    g_ref[...] = _dot3(_sigmoid(lo), gup_ref[...])
    bd = bd_ref[...]
    kk = k * kk_ref[...]
    kk = kk * lax.rsqrt(jnp.maximum(_dot_xl(kk * kk, bd), 1e-24))
    k = k * (1.0 + (a - 1.0) * ka_ref[...])
    bonus_ref[...] = _dot_xl(r * k * rk_ref[...], bd) * v
    v_ref[...] = v
    av = -kk
    bv = kk * a
    tri = tri_ref[...]
    lc = RWKV_CHUNK
    pcs = []
    for cc in range(tm // lc):
        sl = slice(cc * lc, (cc + 1) * lc)
        lwc = lw[sl]
        cs = _dot_xr(tri, lwc)
        cs_end = cs[lc - 1:lc, :]
        e_in = jnp.exp(cs)
        e_out = jnp.exp(-cs)
        e_end = jnp.exp(cs_end - cs)
        rt_ref[sl, :] = r[sl] * e_in
        at_ref[sl, :] = av[sl] * jnp.exp(cs - lwc)
        bt_ref[sl, :] = bv[sl] * e_out
        kt_ref[sl, :] = k[sl] * e_out
        bh_ref[sl, :] = bv[sl] * e_end
        kh_ref[sl, :] = k[sl] * e_end
        pcs.append(jnp.exp(cs_end))
    pc_ref[...] = jnp.concatenate(pcs, axis=0)


def _rwkv_chunk_kernel(rt_ref, at_ref, bt_ref, kt_ref, v_ref, bh_ref, kh_ref, pc_ref,
                       rh_ref, yh_ref, phi_ref, psi_ref):
    lc = RWKV_CHUNK
    d = HEAD_DIM
    row = lax.broadcasted_iota(jnp.int32, (lc, lc), 0)
    col = lax.broadcasted_iota(jnp.int32, (lc, lc), 1)
    strict = row > col
    incl = row >= col
    eye = (row == col).astype(F32)
    pc_row = pc_ref[0]
    rhs, yhs, phis, psis = [], [], [], []
    for hh in range(RWKV_HEADS):
        hs = slice(hh * d, (hh + 1) * d)
        rt = rt_ref[:, hs]
        at = at_ref[:, hs]
        bt = bt_ref[:, hs]
        kt = kt_ref[:, hs]
        vv = v_ref[:, hs]
        bh = bh_ref[:, hs]
        kh = kh_ref[:, hs]
        lab = jnp.where(strict, _dot3(at, bt, 1, 1), 0.0)
        lak = jnp.where(strict, _dot3(at, kt, 1, 1), 0.0)
        mrb = jnp.where(incl, _dot3(rt, bt, 1, 1), 0.0)
        mrk = jnp.where(incl, _dot3(rt, kt, 1, 1), 0.0)
        tinv = eye + lab
        lp = lab
        for _ in range(5):
            lp = _dot3(lp, lp)
            tinv = tinv + _dot3(tinv, lp)
        uh = _dot3(tinv, _dot3(lak, vv))
        ah = _dot3(tinv, at)
        rhs.append(rt + _dot3(mrb, ah))
        yhs.append(_dot3(mrb, uh) + _dot3(mrk, vv))
        phis.append(eye * pc_row[:, hs] + _dot3(bh, ah, 0, 0))
        psis.append(_dot3(bh, uh, 0, 0) + _dot3(kh, vv, 0, 0))
    rh_ref[...] = jnp.concatenate(rhs, axis=1)
    yh_ref[...] = jnp.concatenate(yhs, axis=1)
    phi_ref[...] = jnp.concatenate(phis, axis=1)
    psi_ref[...] = jnp.concatenate(psis, axis=1)


def _rwkv_scan_kernel(rh_ref, yh_ref, phi_ref, psi_ref, bonus_ref, g_ref, gw_ref, gb_ref, bd_ref, o_ref, st_ref,
                      *, n_chunks):
    lc = RWKV_CHUNK
    d = HEAD_DIM

    @pl.when(pl.program_id(1) == 0)
    def _():
        st_ref[...] = jnp.zeros_like(st_ref)

    for cc in range(n_chunks):
        sl = slice(cc * lc, (cc + 1) * lc)
        ys = []
        for hh in range(RWKV_HEADS):
            hs = slice(hh * d, (hh + 1) * d)
            st = st_ref[hh]
            st_new = _dot3(phi_ref[sl, hs], st) + psi_ref[sl, hs]
            ys.append(_dot3(rh_ref[sl, hs], st) + yh_ref[sl, hs])
            st_ref[hh] = st_new
        y = jnp.concatenate(ys, axis=1)
        bd = bd_ref[...]
        mean = _dot_xl(y, bd) * (1.0 / d)
        yc = y - mean
        var = _dot_xl(yc * yc, bd) * (1.0 / d)
        yn = yc * lax.rsqrt(var + RWKV_GN_EPS) * gw_ref[...] + gb_ref[...]
        o_ref[sl, :] = (yn + bonus_ref[sl, :]) * g_ref[sl, :]


def rwkv7_mix(cols, mu, w0, w_up, a0, a_up, g_up, k_k, k_a, r_k, gn_w, gn_b, batch, seq, col_block=0):
    t = cols.shape[0]
    w = RWKV_WIDTH
    lc = RWKV_CHUNK
    tm = 512
    row2 = lambda x: x.reshape(1, -1)
    mu_p = jnp.pad(mu, (0, RWKV_COLS_PAD - RWKV_COLS)).reshape(1, -1)
    r0 = RWKV_DECAY_RANK
    r1 = r0 + RWKV_A_RANK
    r2 = r1 + RWKV_GATE_RANK
    wup_p = jnp.zeros((RWKV_LO_W, w), F32).at[0:r0].set(w_up)
    aup_p = jnp.zeros((RWKV_LO_W, w), F32).at[r0:r1].set(a_up)
    gup_p = jnp.zeros((RWKV_LO_W, w), F32).at[r1:r2].set(g_up)
    bd = jnp.asarray(np.kron(np.eye(RWKV_HEADS), np.ones((HEAD_DIM, HEAD_DIM))), BF16)
    tri = jnp.asarray(np.tril(np.ones((lc, lc))), BF16)
    full = lambda shape: pl.BlockSpec(shape, lambda i: (0,) * len(shape))
    tok = pl.BlockSpec((tm, w), lambda i: (i, 0))
    big = jax.ShapeDtypeStruct((t, w), F32)
    prep = functools.partial(_rwkv_prep_kernel, seq=seq, tm=tm)
    rt, at, bt, kt, v, bh, kh, pc, g, bonus = pl.pallas_call(
        prep,
        grid=(t // tm,),
        in_specs=[
            pl.BlockSpec((tm, RWKV_COLS_PAD), lambda i: (i, col_block)),
            pl.BlockSpec((8, RWKV_COLS_PAD), lambda i: (jnp.maximum(i * (tm // 8) - 1, 0), col_block)),
            full((1, RWKV_COLS_PAD)),
            full((1, w)), full((RWKV_LO_W, w)), full((1, w)), full((RWKV_LO_W, w)), full((RWKV_LO_W, w)),
            full((1, w)), full((1, w)), full((1, w)), full((w, w)), full((lc, lc)),
        ],
        out_specs=[tok, tok, tok, tok, tok, tok, tok, pl.BlockSpec((tm // lc, w), lambda i: (i, 0)), tok, tok],
        out_shape=[big, big, big, big, big, big, big, jax.ShapeDtypeStruct((t // lc, w), F32), big, big],
        compiler_params=_cp(("parallel",)),
        name="rwkv_prep",
    )(cols, cols, mu_p, row2(w0), wup_p, row2(a0), aup_p, gup_p, row2(k_k), row2(k_a), row2(r_k), bd, tri)

    ch = pl.BlockSpec((lc, w), lambda i: (i, 0))
    rh, yh, phi, psi = pl.pallas_call(
        _rwkv_chunk_kernel,
        grid=(t // lc,),
        in_specs=[ch, ch, ch, ch, ch, ch, ch, pl.BlockSpec((1, 1, w), lambda i: (i, 0, 0))],
        out_specs=[ch, ch, ch, ch],
        out_shape=[big, big, big, big],
        compiler_params=_cp(("parallel",)),
        name="rwkv_chunk",
    )(rt, at, bt, kt, v, bh, kh, pc.reshape(t // lc, 1, w))

    n_chunks = 4
    ts = n_chunks * lc
    nsb = seq // ts
    blk = pl.BlockSpec((ts, w), lambda b, j: (b * nsb + j, 0))
    scan = functools.partial(_rwkv_scan_kernel, n_chunks=n_chunks)
    return pl.pallas_call(
        scan,
        grid=(batch, nsb),
        in_specs=[blk, blk, blk, blk, blk, blk,
                  pl.BlockSpec((1, w), lambda b, j: (0, 0)), pl.BlockSpec((1, w), lambda b, j: (0, 0)),
                  pl.BlockSpec((w, w), lambda b, j: (0, 0))],
        out_specs=blk,
        out_shape=big,
        scratch_shapes=[pltpu.VMEM((RWKV_HEADS, HEAD_DIM, HEAD_DIM), F32)],
        compiler_params=_cp(("parallel", "arbitrary")),
        name="rwkv_scan",
    )(rh, yh, phi, psi, bonus, g, row2(gn_w), row2(gn_b), bd)


NSA_Q_HEADS = 8
NSA_KV_HEADS = 2
NSA_GROUP = 4
NSA_WIDTH = 512
NSA_KV_WIDTH = 128
NSA_COLS = NSA_WIDTH + 6 * NSA_KV_WIDTH + 3 * NSA_Q_HEADS
CMP_BLOCK = 32
CMP_STRIDE = 16
CMP_HIDDEN = 128
SLC_BLOCK = 64
SLC_TOPN = 16
WINDOW = 512
FORCED_SCORE = 1e6
NSA_TQ = 128
NSA_CB_KC, NSA_CB_VC, NSA_CB_KS, NSA_CB_VS, NSA_CB_KW, NSA_CB_VW, NSA_CB_GATE = 4, 5, 6, 7, 8, 9, 10


def _rope_tables(pos, reps):
    half = ROPE_DIM // 2
    inv = jnp.power(ROPE_THETA, -jnp.arange(half, dtype=F32) * (2.0 / ROPE_DIM))
    ang = pos.astype(F32)[:, None] * inv[None, :]
    cos = jnp.cos(ang)
    sin = jnp.sin(ang)
    p = pos.shape[0]
    ones = jnp.ones((p, HEAD_DIM - ROPE_DIM), F32)
    zeros = jnp.zeros((p, HEAD_DIM - half), F32)
    cos_t = jnp.concatenate([cos, cos, ones], axis=1)
    sin_a = jnp.concatenate([-sin, zeros], axis=1)
    sin_b = jnp.concatenate([jnp.zeros((p, half), F32), sin, jnp.zeros((p, HEAD_DIM - ROPE_DIM), F32)], axis=1)
    tile = lambda x: jnp.tile(x, (1, reps))
    return tile(cos_t), tile(sin_a), tile(sin_b)


def _rope(x, cos_t, sin_a, sin_b):
    n = x.shape[1]
    half = ROPE_DIM // 2
    return x * cos_t + pltpu.roll(x, n - half, axis=1) * sin_a + pltpu.roll(x, half, axis=1) * sin_b


def _compress_kernel(xk_ref, xv_ref, pek_ref, w1k_ref, b1k_ref, w2k_ref, pev_ref, w1v_ref, b1v_ref, w2v_ref,
                     cos_ref, sa_ref, sb_ref, ko_ref, vo_ref):
    half_w = (CMP_BLOCK // 2) * HEAD_DIM
    n_half = xk_ref.shape[2]

    def mlp(x, pe_ref, w1_ref, b1_ref, w2_ref):
        w1 = w1_ref[...]
        top = _dot3(x, w1[:half_w])
        bot = _dot3(x, w1[half_w:])
        pe = jnp.broadcast_to(pe_ref[...], (8, 2 * half_w))
        const = _dot3(pe, w1)[0:1, :] + b1_ref[...]
        hid = top + pltpu.roll(bot, n_half - 1, axis=0) + const
        return _dot3(_silu(hid), w2_ref[...])

    k = mlp(xk_ref[0, 0], pek_ref, w1k_ref, b1k_ref, w2k_ref)
    k = _rope(k, cos_ref[...], sa_ref[...], sb_ref[...])
    ko_ref[0, 0] = k[:, :HEAD_DIM]
    v = mlp(xv_ref[0, 0], pev_ref, w1v_ref, b1v_ref, w2v_ref)
    vo_ref[0, 0] = v[:, :HEAD_DIM]


def nsa_compress(cols, pe_k, w1_k, b1_k, w2_k, pe_v, w1_v, b1_v, w2_v, batch, seq):
    hk, d = NSA_KV_HEADS, HEAD_DIM
    n_half = seq // CMP_STRIDE
    half_w = CMP_STRIDE * d

    def halves(cb):
        x = cols[:, cb * 128:(cb + 1) * 128].reshape(batch, seq, hk, d).transpose(0, 2, 1, 3)
        return x.reshape(batch, hk, n_half, half_w)

    xk = halves(NSA_CB_KC)
    xv = halves(NSA_CB_VC)
    cmp_end = jnp.arange(n_half) * CMP_STRIDE + (CMP_BLOCK - 1)
    cos_t, sin_a, sin_b = _rope_tables(cmp_end, 2)
    pad_w2 = lambda w: jnp.pad(w, ((0, 0), (0, 128 - d)))
    full = lambda shape: pl.BlockSpec(shape, lambda b, h: (0,) * len(shape))
    xspec = pl.BlockSpec((1, 1, n_half, half_w), lambda b, h: (b, h, 0, 0))
    ospec = pl.BlockSpec((1, 1, n_half, d), lambda b, h: (b, h, 0, 0))
    wspecs = [full((1, 2 * half_w)), full((2 * half_w, CMP_HIDDEN)), full((1, CMP_HIDDEN)), full((CMP_HIDDEN, 128))]
    oshape = jax.ShapeDtypeStruct((batch, hk, n_half, d), F32)
    return pl.pallas_call(
        _compress_kernel,
        grid=(batch, hk),
        in_specs=[xspec, xspec] + wspecs + wspecs + [full((n_half, 128))] * 3,
        out_specs=[ospec, ospec],
        out_shape=[oshape, oshape],
        compiler_params=_cp(("parallel", "parallel")),
        name="nsa_compress",
    )(xk, xv, pe_k.reshape(1, -1), w1_k, b1_k.reshape(1, -1), pad_w2(w2_k),
      pe_v.reshape(1, -1), w1_v, b1_v.reshape(1, -1), pad_w2(w2_v), cos_t, sin_a, sin_b)


def _nsa_attn_kernel(q_ref, ks_ref, vs_ref, kw_ref, vw_ref, gate_ref, kc_ref, vc_ref,
                     cosk_ref, sak_ref, sbk_ref, cosq_ref, saq_ref, sbq_ref, ovl_ref, exp_ref, ge_ref,
                     o_ref, ks_s, vs_s, kw_s, vw_s, *, seq):
    i = pl.program_id(1)
    tq = NSA_TQ
    d = HEAD_DIM
    g = NSA_GROUP
    rows = g * tq

    @pl.when(i == 0)
    def _():
        ck, sa, sb = cosk_ref[...], sak_ref[...], sbk_ref[...]
        ks = _rope(ks_ref[...], ck, sa, sb)
        kw = _rope(kw_ref[...], ck, sa, sb)
        vs = vs_ref[...]
        vw = vw_ref[...]
        for hk in range(NSA_KV_HEADS):
            hs = slice(hk * d, (hk + 1) * d)
            ks_s[hk] = ks[:, hs].astype(BF16)
            kw_s[hk] = kw[:, hs].astype(BF16)
            vs_s[hk] = vs[:, hs].astype(BF16)
            vw_s[hk] = vw[:, hs].astype(BF16)

    qf = _rope(q_ref[...], cosq_ref[...], saq_ref[...], sbq_ref[...]) * (HEAD_DIM ** -0.5)
    gexp = _dot_xl(_sigmoid(gate_ref[...]), ge_ref[...])

    pos_q = i * tq + (lax.broadcasted_iota(jnp.int32, (rows, 1), 0) & (tq - 1))
    lane = lax.broadcasted_iota(jnp.int32, (rows, tq), 1)
    blk = lax.broadcasted_iota(jnp.int32, (32, tq), 0)
    cur = (i * tq + lax.broadcasted_iota(jnp.int32, (32, tq), 1)) // SLC_BLOCK
    sel_valid = blk <= cur
    sel_forced = (blk == 0) | (blk == cur) | (blk == cur - 1)

    def flash(q4b, k_s, v_s, hk, j_lo, mask_fn):
        def body(j, carry):
            m_run, l_run, acc = carry
            k = k_s[hk, pl.ds(j * tq, tq), :]
            v = v_s[hk, pl.ds(j * tq, tq), :]
            s = _dg(q4b, k, 1, 1)
            key_pos = j * tq + lane
            ok = mask_fn(j, key_pos)
            s = jnp.where(ok, s, MASK_NEG)
            m_new = jnp.maximum(m_run, jnp.max(s, axis=-1, keepdims=True))
            alpha = jnp.exp(m_run - m_new)
            p = jnp.where(ok, jnp.exp(s - m_new), 0.0)
            l_new = alpha * l_run + jnp.sum(p, axis=-1, keepdims=True)
            acc = alpha * acc + _dg(p.astype(BF16), v)
            return m_new, l_new, acc

        init = (jnp.full((rows, 1), MASK_NEG, F32), jnp.zeros((rows, 1), F32), jnp.zeros((rows, d), F32))
        _, l_fin, acc = lax.fori_loop(j_lo, i + 1, body, init)
        return acc / l_fin

    outs = []
    for hk in range(NSA_KV_HEADS):
        q4 = jnp.concatenate([qf[:, (hk * g + gg) * d:(hk * g + gg + 1) * d] for gg in range(g)], axis=0)
        q4b = q4.astype(BF16)

        kc = kc_ref[0, hk]
        vc = vc_ref[0, hk]
        s = _dot3(q4, kc, 1, 1)
        cend = lane * CMP_STRIDE + (CMP_BLOCK - 1)
        cmask = (cend <= pos_q) & (lane < seq // CMP_STRIDE - 1)
        sm = jnp.where(cmask, s, MASK_NEG)
        e = jnp.exp(sm - jnp.max(sm, axis=-1, keepdims=True))
        p = e / jnp.sum(e, axis=-1, keepdims=True) * cmask.astype(F32)
        o_cmp = _dot1(p, vc)

        psum = p[0:tq] + p[tq:2 * tq] + p[2 * tq:3 * tq] + p[3 * tq:4 * tq]
        imp = _dot_xr(ovl_ref[...], psum, 1, 1)
        imp = jnp.where(sel_forced, FORCED_SCORE, jnp.where(sel_valid, imp, -FORCED_SCORE))
        members = []
        for jj in range(32):
            rj = imp[jj:jj + 1, :]
            beats = (imp > rj) | ((imp == rj) & (blk < jj))
            cnt = jnp.sum(beats.astype(F32), axis=0, keepdims=True)
            members.append((cnt < SLC_TOPN).astype(F32))
        member_t = jnp.concatenate(members + [jnp.zeros((128 - 32, tq), F32)], axis=0)
        member = member_t.T.astype(BF16)
        member4 = jnp.concatenate([member] * g, axis=0)

        def slc_mask(j, key_pos):
            sel = _dg(member4, exp_ref[pl.ds(j * tq, tq), :], 1, 1)
            return (sel > 0.5) & (key_pos <= pos_q)

        o_slc = flash(q4b, ks_s, vs_s, hk, 0, slc_mask)

        def win_mask(j, key_pos):
            return (key_pos <= pos_q) & (pos_q - key_pos < WINDOW)

        o_win = flash(q4b, kw_s, vw_s, hk, jnp.maximum(i - WINDOW // tq, 0), win_mask)

        for gg in range(g):
            h = hk * g + gg
            rs = slice(gg * tq, (gg + 1) * tq)
            ge = lambda br: gexp[:, br * NSA_WIDTH + h * d: br * NSA_WIDTH + (h + 1) * d]
            outs.append(ge(0) * o_cmp[rs] + ge(1) * o_slc[rs] + ge(2) * o_win[rs])
    o_ref[...] = jnp.concatenate(outs, axis=1)


def nsa_mix(cols, pe_k, w1_k, b1_k, w2_k, pe_v, w1_v, b1_v, w2_v, batch, seq):
    t = cols.shape[0]
    tq = NSA_TQ
    nq = seq // tq
    d = HEAD_DIM
    k_cmp, v_cmp = nsa_compress(cols, pe_k, w1_k, b1_k, w2_k, pe_v, w1_v, b1_v, w2_v, batch, seq)
    pos = jnp.arange(seq)
    cos_k, sa_k, sb_k = _rope_tables(pos, 2)
    cos_q, sa_q, sb_q = _rope_tables(pos, NSA_Q_HEADS)
    n_slc = seq // SLC_BLOCK
    n_half = seq // CMP_STRIDE
    blk_tok = np.arange(n_half - 1)[:, None] * CMP_STRIDE + np.arange(CMP_BLOCK)[None, :]
    overlap = np.any((blk_tok[:, :, None] // SLC_BLOCK) == np.arange(n_slc)[None, None, :], axis=1)
    ovl_t = np.zeros((32, 128), np.float32)
    ovl_t[:n_slc, :n_half - 1] = overlap.T
    expand_t = (np.arange(seq)[:, None] // SLC_BLOCK == np.arange(128)[None, :]).astype(np.float32)
    gate_e = np.zeros((128, 3 * NSA_WIDTH), np.float32)
    for h in range(NSA_Q_HEADS):
        for br in range(3):
            gate_e[h * 3 + br, br * NSA_WIDTH + h * d: br * NSA_WIDTH + (h + 1) * d] = 1.0
    kvspec = lambda cb: pl.BlockSpec((seq, 128), lambda b, i: (b, cb))
    full = lambda shape: pl.BlockSpec(shape, lambda b, i: (0,) * len(shape))
    cspec = pl.BlockSpec((1, NSA_KV_HEADS, n_half, d), lambda b, i: (b, 0, 0, 0))
    kern = functools.partial(_nsa_attn_kernel, seq=seq)
    return pl.pallas_call(
        kern,
        grid=(batch, nq),
        in_specs=[
            pl.BlockSpec((tq, NSA_WIDTH), lambda b, i: (b * nq + i, 0)),
            kvspec(NSA_CB_KS), kvspec(NSA_CB_VS), kvspec(NSA_CB_KW), kvspec(NSA_CB_VW),
            pl.BlockSpec((tq, 128), lambda b, i: (b * nq + i, NSA_CB_GATE)),
            cspec, cspec,
            full((seq, 128)), full((seq, 128)), full((seq, 128)),
            pl.BlockSpec((tq, NSA_WIDTH), lambda b, i: (i, 0)),
            pl.BlockSpec((tq, NSA_WIDTH), lambda b, i: (i, 0)),
            pl.BlockSpec((tq, NSA_WIDTH), lambda b, i: (i, 0)),
            full((32, 128)), full((seq, 128)), full((128, 3 * NSA_WIDTH)),
        ],
        out_specs=pl.BlockSpec((tq, NSA_WIDTH), lambda b, i: (b * nq + i, 0)),
        out_shape=jax.ShapeDtypeStruct((t, NSA_WIDTH), F32),
        scratch_shapes=[pltpu.VMEM((NSA_KV_HEADS, seq, d), BF16)] * 4,
        compiler_params=_cp(("parallel", "arbitrary")),
        name="nsa_attn",
    )(cols, cols, cols, cols, cols, cols, k_cmp, v_cmp, cos_k, sa_k, sb_k, cos_q, sa_q, sb_q,
      jnp.asarray(ovl_t, BF16), jnp.asarray(expand_t, BF16), jnp.asarray(gate_e, BF16))


HYB_NSA_PAD = 1792
HYB_COLS_PAD = HYB_NSA_PAD + RWKV_COLS_PAD


def _mm2_res_kernel(a1_ref, a2_ref, w1_ref, w2_ref, r_ref, o_ref):
    acc = jnp.dot(a1_ref[...].astype(BF16), w1_ref[...], preferred_element_type=F32)
    acc = acc + jnp.dot(a2_ref[...].astype(BF16), w2_ref[...], preferred_element_type=F32)
    o_ref[...] = r_ref[...] + acc


def matmul2_residual(a1, a2, w_bf16, res, tm, tn):
    t, k1 = a1.shape
    k2 = a2.shape[1]
    n = w_bf16.shape[1]
    return pl.pallas_call(
        _mm2_res_kernel,
        grid=(t // tm, n // tn),
        in_specs=[
            pl.BlockSpec((tm, k1), lambda i, j: (i, 0)),
            pl.BlockSpec((tm, k2), lambda i, j: (i, 0)),
            pl.BlockSpec((k1, tn), lambda i, j: (0, j)),
            pl.BlockSpec((k2, tn), lambda i, j: (k1 // k2, j)),
            pl.BlockSpec((tm, tn), lambda i, j: (i, j)),
        ],
        out_specs=pl.BlockSpec((tm, tn), lambda i, j: (i, j)),
        out_shape=jax.ShapeDtypeStruct((t, n), F32),
        compiler_params=_cp(("parallel", "arbitrary")),
        name="matmul2_residual",
    )(a1, a2, w_bf16, w_bf16, res)


def hybrid_block(h, g_norm, w_in, w_out, rwkv_params, nsa_params, batch, seq):
    w_rwkv = w_in[:, :RWKV_COLS]
    w_nsa = w_in[:, RWKV_COLS:]
    w_p = jnp.concatenate([
        jnp.pad(w_nsa, ((0, 0), (0, HYB_NSA_PAD - NSA_COLS))),
        jnp.pad(w_rwkv, ((0, 0), (0, RWKV_COLS_PAD - RWKV_COLS))),
    ], axis=1).astype(BF16)
    cols = norm_matmul(h, g_norm, w_p, tm=1024, tn=896)
    y_rwkv = rwkv7_mix(cols, *rwkv_params, batch=batch, seq=seq, col_block=1)
    y_nsa = nsa_mix(cols, *nsa_params, batch=batch, seq=seq)
    return matmul2_residual(y_rwkv, y_nsa, w_out.astype(BF16), h, tm=1024, tn=512)


def kernel(x, norm_mix, norm_ffn, norm_final, hyb_w_in, hyb_w_out, rwkv_mu, rwkv_w0, rwkv_w_up, rwkv_a0, rwkv_a_up, rwkv_g_up, rwkv_k_k, rwkv_k_a, rwkv_r_k, rwkv_gn_w, rwkv_gn_b, nsa_pe_k, nsa_w1_k, nsa_b1_k, nsa_w2_k, nsa_pe_v, nsa_w1_v, nsa_b1_v, nsa_w2_v, ssm_w_in, ssm_conv_w, ssm_conv_b, ssm_dt_bias, ssm_a_log, ssm_d, ssm_norm_w, ssm_w_out, moe_wg, moe_bg, moe_we, moe_be, moe_w_gate, moe_w_up, moe_w_down):
    batch, seq, d = x.shape
    depth = norm_mix.shape[0]
    h = x.reshape(batch * seq, d)
    for layer in range(depth):
        i = layer // 2
        if layer % 2 == 0:
            rwkv_params = (rwkv_mu[i], rwkv_w0[i], rwkv_w_up[i], rwkv_a0[i], rwkv_a_up[i], rwkv_g_up[i],
                           rwkv_k_k[i], rwkv_k_a[i], rwkv_r_k[i].reshape(-1), rwkv_gn_w[i], rwkv_gn_b[i])
            nsa_params = (nsa_pe_k[i], nsa_w1_k[i], nsa_b1_k[i], nsa_w2_k[i],
                          nsa_pe_v[i], nsa_w1_v[i], nsa_b1_v[i], nsa_w2_v[i])
            h = hybrid_block(h, norm_mix[layer], hyb_w_in[i], hyb_w_out[i], rwkv_params, nsa_params, batch, seq)
        else:
            h = mamba2_block(h, norm_mix[layer], ssm_w_in[i], ssm_conv_w[i], ssm_conv_b[i], ssm_dt_bias[i],
                             ssm_a_log[i], ssm_d[i], ssm_norm_w[i], ssm_w_out[i], batch, seq)
        h = hier_moe_block(h, norm_ffn[layer], moe_wg[layer], moe_bg[layer], moe_we[layer], moe_be[layer],
                           moe_w_gate[layer], moe_w_up[layer], moe_w_down[layer])
    return final_norm(h, norm_final).reshape(batch, seq, d)
```

```python
import functools
import math

import numpy as np
import jax
import jax.numpy as jnp
from jax import lax
from jax.experimental import pallas as pl
from jax.experimental.pallas import tpu as pltpu

F32 = jnp.float32
BF16 = jnp.bfloat16

D_MODEL = 1024
NORM_EPS = 1e-6
HEAD_DIM = 64
ROPE_DIM = 16
ROPE_THETA = 500000.0
MASK_NEG = -1e30

VMEM_LIMIT = 48 * 1024 * 1024


def _cp(sem, vmem=VMEM_LIMIT):
    return pltpu.CompilerParams(dimension_semantics=sem, vmem_limit_bytes=vmem)


def _dg(a, b, ca=1, cb=0):
    return lax.dot_general(a, b, (((ca,), (cb,)), ((), ())), preferred_element_type=F32)


def _split2(x):
    hi = x.astype(BF16)
    lo = (x - hi.astype(F32)).astype(BF16)
    return hi, lo


def _split3(x):
    hi = x.astype(BF16)
    r = x - hi.astype(F32)
    mid = r.astype(BF16)
    lo = (r - mid.astype(F32)).astype(BF16)
    return hi, mid, lo


def _dot1(a, b, ca=1, cb=0):
    return _dg(a.astype(BF16), b.astype(BF16), ca, cb)


def _dot3(a, b, ca=1, cb=0):
    ah, al = _split2(a)
    bh, bl = _split2(b)
    return _dg(ah, bh, ca, cb) + (_dg(ah, bl, ca, cb) + _dg(al, bh, ca, cb))


def _dot_xl(a, b01, ca=1, cb=0):
    h, m, l = _split3(a)
    return _dg(h, b01, ca, cb) + (_dg(m, b01, ca, cb) + _dg(l, b01, ca, cb))


def _dot_xr(a01, b, ca=1, cb=0):
    h, m, l = _split3(b)
    return _dg(a01, h, ca, cb) + (_dg(a01, m, ca, cb) + _dg(a01, l, ca, cb))


def _sigmoid(x):
    return 1.0 / (1.0 + jnp.exp(-x))


def _silu(x):
    return x * _sigmoid(x)


def _softplus(x):
    return jnp.maximum(x, 0.0) + jnp.log(1.0 + jnp.exp(-jnp.abs(x)))


def _norm_mm_kernel(x_ref, g_ref, w_ref, o_ref, xn_ref):
    @pl.when(pl.program_id(1) == 0)
    def _():
        x = x_ref[...]
        ms = jnp.mean(x * x, axis=-1, keepdims=True)
        xn_ref[...] = (x * lax.rsqrt(ms + NORM_EPS) * g_ref[...]).astype(BF16)

    o_ref[...] = jnp.dot(xn_ref[...], w_ref[...], preferred_element_type=F32)


def norm_matmul(x, g, w_bf16, tm, tn):
    t, d = x.shape
    n = w_bf16.shape[1]
    return pl.pallas_call(
        _norm_mm_kernel,
        grid=(t // tm, n // tn),
        in_specs=[
            pl.BlockSpec((tm, d), lambda i, j: (i, 0)),
            pl.BlockSpec((1, d), lambda i, j: (0, 0)),
            pl.BlockSpec((d, tn), lambda i, j: (0, j)),
        ],
        out_specs=pl.BlockSpec((tm, tn), lambda i, j: (i, j)),
        out_shape=jax.ShapeDtypeStruct((t, n), F32),
        scratch_shapes=[pltpu.VMEM((tm, d), BF16)],
        compiler_params=_cp(("parallel", "arbitrary")),
        name="norm_matmul",
    )(x, g.reshape(1, d), w_bf16)


def _mm_res_kernel(a_ref, w_ref, r_ref, o_ref):
    o_ref[...] = r_ref[...] + jnp.dot(a_ref[...].astype(BF16), w_ref[...], preferred_element_type=F32)


def matmul_residual(a, w_bf16, res, tm, tn):
    t, k = a.shape
    n = w_bf16.shape[1]
    return pl.pallas_call(
        _mm_res_kernel,
        grid=(t // tm, n // tn),
        in_specs=[
            pl.BlockSpec((tm, k), lambda i, j: (i, 0)),
            pl.BlockSpec((k, tn), lambda i, j: (0, j)),
            pl.BlockSpec((tm, tn), lambda i, j: (i, j)),
        ],
        out_specs=pl.BlockSpec((tm, tn), lambda i, j: (i, j)),
        out_shape=jax.ShapeDtypeStruct((t, n), F32),
        compiler_params=_cp(("parallel", "arbitrary")),
        name="matmul_residual",
    )(a, w_bf16, res)


def _final_norm_kernel(x_ref, g_ref, o_ref):
    x = x_ref[...]
    ms = jnp.mean(x * x, axis=-1, keepdims=True)
    o_ref[...] = x * lax.rsqrt(ms + NORM_EPS) * g_ref[...]


def final_norm(x, g, tm=1024):
    t, d = x.shape
    return pl.pallas_call(
        _final_norm_kernel,
        grid=(t // tm,),
        in_specs=[pl.BlockSpec((tm, d), lambda i: (i, 0)), pl.BlockSpec((1, d), lambda i: (0, 0))],
        out_specs=pl.BlockSpec((tm, d), lambda i: (i, 0)),
        out_shape=jax.ShapeDtypeStruct((t, d), F32),
        compiler_params=_cp(("parallel",)),
        name="final_norm",
    )(x, g.reshape(1, d))


MOE_GROUPS = 4
MOE_EPG = 8
MOE_EXPERTS = 32
MOE_HIDDEN = 512
MOE_TM = 256


def _router_kernel(x_ref, g_ref, w_ref, b_ref, xn_ref, ids_ref, wts_ref):
    x = x_ref[...]
    ms = jnp.mean(x * x, axis=-1, keepdims=True)
    u = x * lax.rsqrt(ms + NORM_EPS) * g_ref[...]
    xn_ref[...] = u.astype(BF16)
    lg = _dot3(u, w_ref[...]) + b_ref[...]
    tm = x.shape[0]
    lane = lax.broadcasted_iota(jnp.int32, (tm, 128), 1)
    gl = lg[:, :128]
    el = lg[:, 128:]
    gmask = lane < MOE_GROUPS
    gmax = jnp.max(jnp.where(gmask, gl, -jnp.inf), axis=-1, keepdims=True)
    ge = jnp.where(gmask, jnp.exp(gl - gmax), 0.0)
    pg = ge / jnp.sum(ge, axis=-1, keepdims=True)
    g_val = jnp.max(pg, axis=-1, keepdims=True)
    g_idx = jnp.min(jnp.where(gmask & (pg == g_val), lane, 128), axis=-1, keepdims=True)
    emask = (lane >= g_idx * MOE_EPG) & (lane < (g_idx + 1) * MOE_EPG)
    emax = jnp.max(jnp.where(emask, el, -jnp.inf), axis=-1, keepdims=True)
    ee = jnp.where(emask, jnp.exp(el - emax), 0.0)
    pe = ee / jnp.sum(ee, axis=-1, keepdims=True)
    v1 = jnp.max(jnp.where(emask, pe, -1.0), axis=-1, keepdims=True)
    i1 = jnp.min(jnp.where(emask & (pe == v1), lane, 128), axis=-1, keepdims=True)
    m2 = emask & (lane != i1)
    v2 = jnp.max(jnp.where(m2, pe, -1.0), axis=-1, keepdims=True)
    i2 = jnp.min(jnp.where(m2 & (pe == v2), lane, 128), axis=-1, keepdims=True)
    den = v1 + v2
    w1 = v1 / den * g_val
    w2 = v2 / den * g_val
    ids_ref[...] = jnp.where(lane == 0, i1, jnp.where(lane == 1, i2, 0))
    wts_ref[...] = jnp.where(lane == 0, w1, jnp.where(lane == 1, w2, 0.0))


def moe_router(h, g, wg, bg, we, be, tm=512):
    t, d = h.shape
    w = jnp.zeros((d, 256), F32).at[:, :MOE_GROUPS].set(wg).at[:, 128:128 + MOE_EXPERTS].set(we)
    b = jnp.zeros((1, 256), F32).at[0, :MOE_GROUPS].set(bg).at[0, 128:128 + MOE_EXPERTS].set(be)
    return pl.pallas_call(
        _router_kernel,
        grid=(t // tm,),
        in_specs=[
            pl.BlockSpec((tm, d), lambda i: (i, 0)),
            pl.BlockSpec((1, d), lambda i: (0, 0)),
            pl.BlockSpec((d, 256), lambda i: (0, 0)),
            pl.BlockSpec((1, 256), lambda i: (0, 0)),
        ],
        out_specs=[
            pl.BlockSpec((tm, d), lambda i: (i, 0)),
            pl.BlockSpec((tm, 128), lambda i: (i, 0)),
            pl.BlockSpec((tm, 128), lambda i: (i, 0)),
        ],
        out_shape=[
            jax.ShapeDtypeStruct((t, d), BF16),
            jax.ShapeDtypeStruct((t, 128), jnp.int32),
            jax.ShapeDtypeStruct((t, 128), F32),
        ],
        compiler_params=_cp(("parallel",)),
        name="moe_router",
    )(h, g.reshape(1, d), w, b)


def _moe_ffn_kernel(te_ref, tv_ref, x_ref, rw_ref, wg_ref, wu_ref, wd_ref, o_ref, wg_s, wu_s, wd_s):
    i = pl.program_id(0)
    prev = te_ref[jnp.maximum(i - 1, 0)]
    changed = (i == 0) | (te_ref[i] != prev)

    @pl.when(changed)
    def _():
        wg_s[...] = wg_ref[0].astype(BF16)
        wu_s[...] = wu_ref[0].astype(BF16)
        wd_s[...] = wd_ref[0].astype(BF16)

    @pl.when(tv_ref[i] > 0)
    def _():
        x = x_ref[...]
        gate = jnp.dot(x, wg_s[...], preferred_element_type=F32)
        up = jnp.dot(x, wu_s[...], preferred_element_type=F32)
        act = _silu(gate) * up * rw_ref[...]
        o_ref[...] = jnp.dot(act.astype(BF16), wd_s[...], preferred_element_type=F32)

    @pl.when(tv_ref[i] == 0)
    def _():
        o_ref[...] = jnp.zeros_like(o_ref)


def moe_ffn(xs, roww, tile_expert, tile_valid, w_gate, w_up, w_down):
    np_rows, d = xs.shape
    f = w_gate.shape[-1]
    tm = MOE_TM
    n_tiles = np_rows // tm
    grid_spec = pltpu.PrefetchScalarGridSpec(
        num_scalar_prefetch=2,
        grid=(n_tiles,),
        in_specs=[
            pl.BlockSpec((tm, d), lambda i, te, tv: (i, 0)),
            pl.BlockSpec((tm, 1), lambda i, te, tv: (i, 0)),
            pl.BlockSpec((1, d, f), lambda i, te, tv: (te[i], 0, 0)),
            pl.BlockSpec((1, d, f), lambda i, te, tv: (te[i], 0, 0)),
            pl.BlockSpec((1, f, d), lambda i, te, tv: (te[i], 0, 0)),
        ],
        out_specs=pl.BlockSpec((tm, d), lambda i, te, tv: (i, 0)),
        scratch_shapes=[pltpu.VMEM((d, f), BF16), pltpu.VMEM((d, f), BF16), pltpu.VMEM((f, d), BF16)],
    )
    return pl.pallas_call(
        _moe_ffn_kernel,
        grid_spec=grid_spec,
        out_shape=jax.ShapeDtypeStruct((np_rows, d), F32),
        compiler_params=_cp(("arbitrary",)),
        name="moe_ffn",
    )(tile_expert, tile_valid, xs, roww, w_gate, w_up, w_down)


def hier_moe_block(h, g, wg, bg, we, be, w_gate, w_up, w_down, g_final=None):
    t, d = h.shape
    tm = MOE_TM
    xn, ids, wts = moe_router(h, g, wg, bg, we, be)
    ids = ids[:, :2]
    wts = wts[:, :2]
    e_flat = ids.reshape(-1)
    n_asg = e_flat.shape[0]
    onehot = (e_flat[:, None] == jnp.arange(MOE_EXPERTS, dtype=jnp.int32)[None, :]).astype(jnp.int32)
    seen = jnp.cumsum(onehot, axis=0)
    counts = seen[-1]
    rank = jnp.sum((seen - onehot) * onehot, axis=1)
    padded = ((counts + tm - 1) // tm) * tm
    pad_end = jnp.cumsum(padded)
    pad_off = pad_end - padded
    dest = jnp.sum(onehot * pad_off[None, :], axis=1) + rank
    np_rows = n_asg + MOE_EXPERTS * tm
    asg = jnp.arange(n_asg, dtype=jnp.int32)
    row_src = jnp.full((np_rows,), -1, jnp.int32).at[dest].set(asg)
    row_asg = jnp.maximum(row_src, 0)
    row_tok = row_asg // 2
    row_w = jnp.where(row_src >= 0, jnp.take(wts.reshape(-1), row_asg), 0.0)
    n_tiles = np_rows // tm
    tile_start = jnp.arange(n_tiles, dtype=jnp.int32) * tm
    tile_expert = jnp.minimum(jnp.sum((tile_start[:, None] >= pad_end[None, :]).astype(jnp.int32), axis=1),
                              MOE_EXPERTS - 1)
    tile_valid = (tile_start < pad_end[-1]).astype(jnp.int32)
    xs = jnp.take(xn, row_tok, axis=0)
    y = moe_ffn(xs, row_w.reshape(np_rows, 1), tile_expert, tile_valid, w_gate, w_up, w_down)
    y2 = jnp.take(y, dest, axis=0).reshape(t, 2 * d)
    return moe_combine(h, y2, g_final)


def _combine_kernel(h_ref, y_ref, o_ref):
    d = h_ref.shape[1]
    o_ref[...] = h_ref[...] + y_ref[:, :d] + y_ref[:, d:]


def _combine_norm_kernel(h_ref, y_ref, g_ref, o_ref):
    d = h_ref.shape[1]
    x = h_ref[...] + y_ref[:, :d] + y_ref[:, d:]
    ms = jnp.mean(x * x, axis=-1, keepdims=True)
    o_ref[...] = x * lax.rsqrt(ms + NORM_EPS) * g_ref[...]


def moe_combine(h, y2, g_final=None, tm=1024):
    t, d = h.shape
    specs = [pl.BlockSpec((tm, d), lambda i: (i, 0)), pl.BlockSpec((tm, 2 * d), lambda i: (i, 0))]
    args = [h, y2]
    kern = _combine_kernel
    if g_final is not None:
        specs.append(pl.BlockSpec((1, d), lambda i: (0, 0)))
        args.append(g_final.reshape(1, d))
        kern = _combine_norm_kernel
    return pl.pallas_call(
        kern,
        grid=(t // tm,),
        in_specs=specs,
        out_specs=pl.BlockSpec((tm, d), lambda i: (i, 0)),
        out_shape=jax.ShapeDtypeStruct((t, d), F32),
        compiler_params=_cp(("parallel",)),
        name="moe_combine",
    )(*args)


SSM_D_INNER = 2048
SSM_HEADS = 32
SSM_GROUPS = 4
SSM_HPG = 8
SSM_STATE = 128
SSM_CONV = 4
SSM_CHUNK = 256
SSM_NORM_EPS = 1e-5
SSM_CONV_CH = SSM_D_INNER + 2 * SSM_GROUPS * SSM_STATE
SSM_IN_COLS = SSM_D_INNER + SSM_CONV_CH + SSM_HEADS
SSM_IN_PAD = 5376
SSM_GW = SSM_HPG * HEAD_DIM


def _conv_silu_kernel(x_ref, p_ref, w_ref, b_ref, o_ref, *, seq, tm):
    i = pl.program_id(0)
    x = x_ref[...]
    first = (i * tm) % seq == 0
    prev = jnp.where(first, 0.0, p_ref[...])
    xx = jnp.concatenate([prev, x], axis=0)
    w = w_ref[...]
    acc = b_ref[...] + w[3:4, :] * x
    for k in range(1, SSM_CONV):
        acc = acc + w[3 - k:4 - k, :] * xx[8 - k:8 - k + tm, :]
    o_ref[...] = _silu(acc)


def conv_silu(zx, conv_w, conv_b, seq, tm=512, tc=1024):
    t = zx.shape[0]
    c0 = SSM_D_INNER // tc
    kern = functools.partial(_conv_silu_kernel, seq=seq, tm=tm)
    return pl.pallas_call(
        kern,
        grid=(t // tm, SSM_CONV_CH // tc),
        in_specs=[
            pl.BlockSpec((tm, tc), lambda i, j: (i, c0 + j)),
            pl.BlockSpec((8, tc), lambda i, j: (jnp.maximum(i * (tm // 8) - 1, 0), c0 + j)),
            pl.BlockSpec((SSM_CONV, tc), lambda i, j: (0, j)),
            pl.BlockSpec((1, tc), lambda i, j: (0, j)),
        ],
        out_specs=pl.BlockSpec((tm, tc), lambda i, j: (i, j)),
        out_shape=jax.ShapeDtypeStruct((t, SSM_CONV_CH), F32),
        compiler_params=_cp(("parallel", "parallel")),
        name="conv_silu",
    )(zx, zx, conv_w, conv_b.reshape(1, -1))


def _ssd_kernel(x_ref, b_ref, c_ref, z_ref, dt_ref, dtt_ref, br_ref, bc_ref, ar_ref, ac_ref, d_ref, nw_ref,
                e8_ref, tri_ref, o_ref, h_ref):
    c = pl.program_id(2)
    lc = SSM_CHUNK

    @pl.when(c == 0)
    def _():
        h_ref[...] = jnp.zeros_like(h_ref)

    e8 = e8_ref[...]
    tri = tri_ref[...]
    x = x_ref[...]
    bm = b_ref[...]
    cm = c_ref[...]
    dt = _softplus(dt_ref[0, 0] + br_ref[0])
    dtt = _softplus(dtt_ref[0, 0] + bc_ref[0])
    a = dt * (-jnp.exp(ar_ref[0]))
    at = dtt * (-jnp.exp(ac_ref[0]))
    cs = _dot_xr(tri, a)
    cst = _dot_xl(at, tri, 1, 1)
    cs_last = cs[lc - 1:lc, :]

    xdt = x * _dot_xl(dt, e8)
    cb = _dot1(cm, bm, 1, 1)
    row = lax.broadcasted_iota(jnp.int32, (lc, lc), 0)
    col = lax.broadcasted_iota(jnp.int32, (lc, lc), 1)
    causal = row >= col
    ys = []
    for hh in range(SSM_HPG):
        lmat = jnp.where(causal, jnp.exp(cs[:, hh:hh + 1] - cst[hh:hh + 1, :]), 0.0)
        ys.append(_dot1(cb * lmat, xdt[:, hh * HEAD_DIM:(hh + 1) * HEAD_DIM]))
    y = jnp.concatenate(ys, axis=1)

    h_in = h_ref[...]
    y = y + _dot1(cm, h_in) * _dot_xl(jnp.exp(cs), e8)
    xdec = xdt * _dot_xl(jnp.exp(cs_last - cs), e8)
    h_ref[...] = h_in * _dot_xl(jnp.exp(cs_last), e8) + _dot1(bm, xdec, 0, 0)

    y = y + x * _dot_xl(d_ref[0], e8)
    zz = z_ref[...]
    y = y * _silu(zz)
    y = y * lax.rsqrt(jnp.mean(y * y, axis=-1, keepdims=True) + SSM_NORM_EPS) * nw_ref[...]
    o_ref[...] = y.astype(BF16)


def ssd_mix(zx, xbc, dt_bias, a_log, d_skip, norm_w, batch, seq):
    t = zx.shape[0]
    lc = SSM_CHUNK
    nc = seq // lc
    g, hp = SSM_GROUPS, SSM_HPG
    dt_raw = zx[:, SSM_D_INNER + SSM_CONV_CH:SSM_IN_COLS].reshape(batch, seq, g, hp)
    dt_g = dt_raw.transpose(0, 2, 1, 3)
    dt_t = dt_raw.transpose(0, 2, 3, 1)
    b_row = dt_bias.reshape(g, 1, hp)
    b_col = dt_bias.reshape(g, hp, 1)
    a_row = a_log.reshape(g, 1, hp)
    a_col = a_log.reshape(g, hp, 1)
    d_row = d_skip.reshape(g, 1, hp)
    e8 = jnp.asarray(np.kron(np.eye(hp), np.ones((1, HEAD_DIM))), BF16)
    tri = jnp.asarray(np.tril(np.ones((lc, lc))), BF16)
    xcol = SSM_D_INNER // SSM_STATE
    return pl.pallas_call(
        _ssd_kernel,
        grid=(batch, g, nc),
        in_specs=[
            pl.BlockSpec((lc, SSM_GW), lambda b, gg, c: (b * nc + c, gg)),
            pl.BlockSpec((lc, SSM_STATE), lambda b, gg, c: (b * nc + c, xcol + gg)),
            pl.BlockSpec((lc, SSM_STATE), lambda b, gg, c: (b * nc + c, xcol + g + gg)),
            pl.BlockSpec((lc, SSM_GW), lambda b, gg, c: (b * nc + c, gg)),
            pl.BlockSpec((1, 1, lc, hp), lambda b, gg, c: (b, gg, c, 0)),
            pl.BlockSpec((1, 1, hp, lc), lambda b, gg, c: (b, gg, 0, c)),
            pl.BlockSpec((1, 1, hp), lambda b, gg, c: (gg, 0, 0)),
            pl.BlockSpec((1, hp, 1), lambda b, gg, c: (gg, 0, 0)),
            pl.BlockSpec((1, 1, hp), lambda b, gg, c: (gg, 0, 0)),
            pl.BlockSpec((1, hp, 1), lambda b, gg, c: (gg, 0, 0)),
            pl.BlockSpec((1, 1, hp), lambda b, gg, c: (gg, 0, 0)),
            pl.BlockSpec((1, SSM_GW), lambda b, gg, c: (0, gg)),
            pl.BlockSpec((hp, SSM_GW), lambda b, gg, c: (0, 0)),
            pl.BlockSpec((lc, lc), lambda b, gg, c: (0, 0)),
        ],
        out_specs=pl.BlockSpec((lc, SSM_GW), lambda b, gg, c: (b * nc + c, gg)),
        out_shape=jax.ShapeDtypeStruct((t, SSM_D_INNER), BF16),
        scratch_shapes=[pltpu.VMEM((SSM_STATE, SSM_GW), F32)],
        compiler_params=_cp(("parallel", "parallel", "arbitrary")),
        name="ssd",
    )(xbc, xbc, xbc, zx, dt_g, dt_t, b_row, b_col, a_row, a_col, d_row, norm_w.reshape(1, -1), e8, tri)


def mamba2_block(h, g_norm, w_in, conv_w, conv_b, dt_bias, a_log, d_skip, norm_w, w_out, batch, seq):
    w_in_p = jnp.pad(w_in, ((0, 0), (0, SSM_IN_PAD - SSM_IN_COLS))).astype(BF16)
    zx = norm_matmul(h, g_norm, w_in_p, tm=1024, tn=768)
    xbc = conv_silu(zx, conv_w, conv_b, seq)
    y = ssd_mix(zx, xbc, dt_bias, a_log, d_skip, norm_w, batch, seq)
    return matmul_residual(y, w_out.astype(BF16), h, tm=1024, tn=512)


RWKV_HEADS = 8
RWKV_WIDTH = 512
RWKV_DECAY_RANK = 32
RWKV_A_RANK = 32
RWKV_GATE_RANK = 96
RWKV_GN_EPS = 64e-5
RWKV_COLS = 3 * RWKV_WIDTH + RWKV_DECAY_RANK + RWKV_A_RANK + RWKV_GATE_RANK
RWKV_COLS_PAD = 1792
RWKV_LO = 3 * RWKV_WIDTH
RWKV_LO_W = RWKV_COLS_PAD - RWKV_LO
RWKV_CHUNK = 64


def _rwkv_prep_kernel(c_ref, p_ref, mu_ref, w0_ref, wup_ref, a0_ref, aup_ref, gup_ref, kk_ref, ka_ref, rk_ref,
                      bd_ref, tri_ref,
                      rt_ref, at_ref, bt_ref, kt_ref, v_ref, bh_ref, kh_ref, pc_ref, g_ref, bonus_ref, *, seq, tm):
    i = pl.program_id(0)
    w = RWKV_WIDTH
    c = c_ref[...]
    first = (i * tm) % seq == 0
    prev_row = jnp.where(first, 0.0, p_ref[7:8, :])
    row = lax.broadcasted_iota(jnp.int32, c.shape, 0)
    shifted = jnp.where(row == 0, prev_row, pltpu.roll(c, 1, axis=0))
    z = c + (shifted - c) * mu_ref[...]
    r = z[:, 0:w]
    k = z[:, w:2 * w]
    v = z[:, 2 * w:3 * w]
    lo = z[:, RWKV_LO:RWKV_COLS_PAD]
    w_log = -_softplus(-(w0_ref[...] + _dot3(jnp.tanh(lo), wup_ref[...]))) - 0.5
    lw = -jnp.exp(w_log)
    a = _sigmoid(a0_ref[...] + _dot3(lo, aup_ref[...]))
    g_ref[...] = _dot3(_sigmoid(lo), gup_ref[...])
    bd = bd_ref[...]
    kk = k * kk_ref[...]
    kk = kk * lax.rsqrt(jnp.maximum(_dot_xl(kk * kk, bd), 1e-24))
    k = k * (1.0 + (a - 1.0) * ka_ref[...])
    bonus_ref[...] = _dot_xl(r * k * rk_ref[...], bd) * v
    v_ref[...] = v
    av = -kk
    bv = kk * a
    tri = tri_ref[...]
    lc = RWKV_CHUNK
    pcs = []
    for cc in range(tm // lc):
        sl = slice(cc * lc, (cc + 1) * lc)
        lwc = lw[sl]
        cs = _dot_xr(tri, lwc)
        cs_end = cs[lc - 1:lc, :]
        e_in = jnp.exp(cs)
        e_out = jnp.exp(-cs)
        e_end = jnp.exp(cs_end - cs)
        rt_ref[sl, :] = r[sl] * e_in
        at_ref[sl, :] = av[sl] * jnp.exp(cs - lwc)
        bt_ref[sl, :] = bv[sl] * e_out
        kt_ref[sl, :] = k[sl] * e_out
        bh_ref[sl, :] = bv[sl] * e_end
        kh_ref[sl, :] = k[sl] * e_end
        pcs.append(jnp.exp(cs_end))
    pc_ref[...] = jnp.concatenate(pcs, axis=0)


def _rwkv_chunk_kernel(rt_ref, at_ref, bt_ref, kt_ref, v_ref, bh_ref, kh_ref, pc_ref,
                       rh_ref, yh_ref, phi_ref, psi_ref):
    lc = RWKV_CHUNK
    d = HEAD_DIM
    row = lax.broadcasted_iota(jnp.int32, (lc, lc), 0)
    col = lax.broadcasted_iota(jnp.int32, (lc, lc), 1)
    strict = row > col
    incl = row >= col
    eye = (row == col).astype(F32)
    n_ch = rt_ref.shape[0] // lc
    probs = [(cc, hh) for cc in range(n_ch) for hh in range(RWKV_HEADS)]

    def load(ref):
        return [ref[cc * lc:(cc + 1) * lc, hh * d:(hh + 1) * d] for cc, hh in probs]

    def each(fn, *lists):
        return [fn(*xs) for xs in zip(*lists)]

    rt, at, bt, kt, vv, bh, kh = (load(r) for r in (rt_ref, at_ref, bt_ref, kt_ref, v_ref, bh_ref, kh_ref))
    lowp = lambda xs: [x.astype(BF16) for x in xs]
    rtb, atb, btb, ktb, vvb, bhb, khb = (lowp(x) for x in (rt, at, bt, kt, vv, bh, kh))
    nt = lambda a, b: _dg(a, b, 1, 1)
    tn = lambda a, b: _dg(a, b, 0, 0)
    lab = each(lambda a, b: jnp.where(strict, nt(a, b), 0.0), atb, btb)
    lak = each(lambda a, b: jnp.where(strict, nt(a, b), 0.0), atb, ktb)
    mrb = lowp(each(lambda a, b: jnp.where(incl, nt(a, b), 0.0), rtb, btb))
    mrk = lowp(each(lambda a, b: jnp.where(incl, nt(a, b), 0.0), rtb, ktb))
    tinv = each(lambda l: eye + l, lab)
    lp = lowp(lab)
    for _ in range(5):
        lp = lowp(each(_dg, lp, lp))
        tinv = each(lambda t, l: t + _dg(t.astype(BF16), l), tinv, lp)
    tinv = lowp(tinv)
    lv = lowp(each(_dg, lowp(lak), vvb))
    uh = lowp(each(_dg, tinv, lv))
    ah = lowp(each(_dg, tinv, atb))
    rh = each(lambda r, m, a: r + _dg(m, a), rt, mrb, ah)
    yh = each(lambda m, u, mk, v: _dg(m, u) + _dg(mk, v), mrb, uh, mrk, vvb)
    phi = each(lambda b, a: tn(b, a), bhb, ah)
    psi = each(lambda b, u, k, v: tn(b, u) + tn(k, v), bhb, uh, khb, vvb)
    for cc in range(n_ch):
        sl = slice(cc * lc, (cc + 1) * lc)
        ps = slice(cc * RWKV_HEADS, (cc + 1) * RWKV_HEADS)
        pc_row = pc_ref[cc]
        rh_ref[sl, :] = jnp.concatenate(rh[ps], axis=1)
        yh_ref[sl, :] = jnp.concatenate(yh[ps], axis=1)
        phi_ref[sl, :] = jnp.concatenate(phi[ps], axis=1) + jnp.concatenate([eye] * RWKV_HEADS, axis=1) * pc_row
        psi_ref[sl, :] = jnp.concatenate(psi[ps], axis=1)


def _rwkv_scan_kernel(rh_ref, yh_ref, phi_ref, psi_ref, bonus_ref, g_ref, gw_ref, gb_ref, bd_ref, o_ref, st_ref,
                      *, n_chunks):
    lc = RWKV_CHUNK
    d = HEAD_DIM

    @pl.when(pl.program_id(1) == 0)
    def _():
        st_ref[...] = jnp.zeros_like(st_ref)

    for cc in range(n_chunks):
        sl = slice(cc * lc, (cc + 1) * lc)
        hsl = [slice(hh * d, (hh + 1) * d) for hh in range(RWKV_HEADS)]
        stb = [st_ref[hh].astype(BF16) for hh in range(RWKV_HEADS)]
        ys = [_dg(rh_ref[sl, hs].astype(BF16), st) + yh_ref[sl, hs] for hs, st in zip(hsl, stb)]
        for hh, (hs, st) in enumerate(zip(hsl, stb)):
            st_ref[hh] = _dg(phi_ref[sl, hs].astype(BF16), st) + psi_ref[sl, hs]
        y = jnp.concatenate(ys, axis=1)
        bd = bd_ref[...]
        mean = _dot_xl(y, bd) * (1.0 / d)
        yc = y - mean
        var = _dot_xl(yc * yc, bd) * (1.0 / d)
        yn = yc * lax.rsqrt(var + RWKV_GN_EPS) * gw_ref[...] + gb_ref[...]
        o_ref[sl, :] = (yn + bonus_ref[sl, :]) * g_ref[sl, :]


def rwkv7_mix(cols, mu, w0, w_up, a0, a_up, g_up, k_k, k_a, r_k, gn_w, gn_b, batch, seq, col_block=0):
    t = cols.shape[0]
    w = RWKV_WIDTH
    lc = RWKV_CHUNK
    tm = 512
    row2 = lambda x: x.reshape(1, -1)
    mu_p = jnp.pad(mu, (0, RWKV_COLS_PAD - RWKV_COLS)).reshape(1, -1)
    r0 = RWKV_DECAY_RANK
    r1 = r0 + RWKV_A_RANK
    r2 = r1 + RWKV_GATE_RANK
    wup_p = jnp.zeros((RWKV_LO_W, w), F32).at[0:r0].set(w_up)
    aup_p = jnp.zeros((RWKV_LO_W, w), F32).at[r0:r1].set(a_up)
    gup_p = jnp.zeros((RWKV_LO_W, w), F32).at[r1:r2].set(g_up)
    bd = jnp.asarray(np.kron(np.eye(RWKV_HEADS), np.ones((HEAD_DIM, HEAD_DIM))), BF16)
    tri = jnp.asarray(np.tril(np.ones((lc, lc))), BF16)
    full = lambda shape: pl.BlockSpec(shape, lambda i: (0,) * len(shape))
    tok = pl.BlockSpec((tm, w), lambda i: (i, 0))
    big = jax.ShapeDtypeStruct((t, w), F32)
    prep = functools.partial(_rwkv_prep_kernel, seq=seq, tm=tm)
    rt, at, bt, kt, v, bh, kh, pc, g, bonus = pl.pallas_call(
        prep,
        grid=(t // tm,),
        in_specs=[
            pl.BlockSpec((tm, RWKV_COLS_PAD), lambda i: (i, col_block)),
            pl.BlockSpec((8, RWKV_COLS_PAD), lambda i: (jnp.maximum(i * (tm // 8) - 1, 0), col_block)),
            full((1, RWKV_COLS_PAD)),
            full((1, w)), full((RWKV_LO_W, w)), full((1, w)), full((RWKV_LO_W, w)), full((RWKV_LO_W, w)),
            full((1, w)), full((1, w)), full((1, w)), full((w, w)), full((lc, lc)),
        ],
        out_specs=[tok, tok, tok, tok, tok, tok, tok, pl.BlockSpec((tm // lc, w), lambda i: (i, 0)), tok, tok],
        out_shape=[big, big, big, big, big, big, big, jax.ShapeDtypeStruct((t // lc, w), F32), big, big],
        compiler_params=_cp(("parallel",)),
        name="rwkv_prep",
    )(cols, cols, mu_p, row2(w0), wup_p, row2(a0), aup_p, gup_p, row2(k_k), row2(k_a), row2(r_k), bd, tri)

    n_ch = 1
    ch = pl.BlockSpec((n_ch * lc, w), lambda i: (i, 0))
    rh, yh, phi, psi = pl.pallas_call(
        _rwkv_chunk_kernel,
        grid=(t // (n_ch * lc),),
        in_specs=[ch, ch, ch, ch, ch, ch, ch, pl.BlockSpec((n_ch, 1, w), lambda i: (i, 0, 0))],
        out_specs=[ch, ch, ch, ch],
        out_shape=[big, big, big, big],
        compiler_params=_cp(("parallel",)),
        name="rwkv_chunk",
    )(rt, at, bt, kt, v, bh, kh, pc.reshape(t // lc, 1, w))

    n_chunks = 4
    ts = n_chunks * lc
    nsb = seq // ts
    blk = pl.BlockSpec((ts, w), lambda b, j: (b * nsb + j, 0))
    scan = functools.partial(_rwkv_scan_kernel, n_chunks=n_chunks)
    return pl.pallas_call(
        scan,
        grid=(batch, nsb),
        in_specs=[blk, blk, blk, blk, blk, blk,
                  pl.BlockSpec((1, w), lambda b, j: (0, 0)), pl.BlockSpec((1, w), lambda b, j: (0, 0)),
                  pl.BlockSpec((w, w), lambda b, j: (0, 0))],
        out_specs=blk,
        out_shape=big,
        scratch_shapes=[pltpu.VMEM((RWKV_HEADS, HEAD_DIM, HEAD_DIM), F32)],
        compiler_params=_cp(("parallel", "arbitrary")),
        name="rwkv_scan",
    )(rh, yh, phi, psi, bonus, g, row2(gn_w), row2(gn_b), bd)


NSA_Q_HEADS = 8
NSA_KV_HEADS = 2
NSA_GROUP = 4
NSA_WIDTH = 512
NSA_KV_WIDTH = 128
NSA_COLS = NSA_WIDTH + 6 * NSA_KV_WIDTH + 3 * NSA_Q_HEADS
CMP_BLOCK = 32
CMP_STRIDE = 16
CMP_HIDDEN = 128
SLC_BLOCK = 64
SLC_TOPN = 16
WINDOW = 512
FORCED_SCORE = 1e6
NSA_TQ = 128
NSA_CB_KC, NSA_CB_VC, NSA_CB_KS, NSA_CB_VS, NSA_CB_KW, NSA_CB_VW, NSA_CB_GATE = 4, 5, 6, 7, 8, 9, 10


def _rope_tables(pos, reps):
    half = ROPE_DIM // 2
    inv = jnp.power(ROPE_THETA, -jnp.arange(half, dtype=F32) * (2.0 / ROPE_DIM))
    ang = pos.astype(F32)[:, None] * inv[None, :]
    cos = jnp.cos(ang)
    sin = jnp.sin(ang)
    p = pos.shape[0]
    ones = jnp.ones((p, HEAD_DIM - ROPE_DIM), F32)
    zeros = jnp.zeros((p, HEAD_DIM - half), F32)
    cos_t = jnp.concatenate([cos, cos, ones], axis=1)
    sin_a = jnp.concatenate([-sin, zeros], axis=1)
    sin_b = jnp.concatenate([jnp.zeros((p, half), F32), sin, jnp.zeros((p, HEAD_DIM - ROPE_DIM), F32)], axis=1)
    tile = lambda x: jnp.tile(x, (1, reps))
    return tile(cos_t), tile(sin_a), tile(sin_b)


def _rope(x, cos_t, sin_a, sin_b):
    n = x.shape[1]
    half = ROPE_DIM // 2
    return x * cos_t + pltpu.roll(x, n - half, axis=1) * sin_a + pltpu.roll(x, half, axis=1) * sin_b


def _compress_kernel(xk_ref, xv_ref, pek_ref, w1k_ref, b1k_ref, w2k_ref, pev_ref, w1v_ref, b1v_ref, w2v_ref,
                     cos_ref, sa_ref, sb_ref, ko_ref, vo_ref):
    half_w = (CMP_BLOCK // 2) * HEAD_DIM
    n_half = xk_ref.shape[2]

    def mlp(x, pe_ref, w1_ref, b1_ref, w2_ref):
        w1 = w1_ref[...]
        top = _dot3(x, w1[:half_w])
        bot = _dot3(x, w1[half_w:])
        pe = jnp.broadcast_to(pe_ref[...], (8, 2 * half_w))
        const = _dot3(pe, w1)[0:1, :] + b1_ref[...]
        hid = top + pltpu.roll(bot, n_half - 1, axis=0) + const
        return _dot3(_silu(hid), w2_ref[...])

    k = mlp(xk_ref[0, 0], pek_ref, w1k_ref, b1k_ref, w2k_ref)
    k = _rope(k, cos_ref[...], sa_ref[...], sb_ref[...])
    ko_ref[0, 0] = k[:, :HEAD_DIM]
    v = mlp(xv_ref[0, 0], pev_ref, w1v_ref, b1v_ref, w2v_ref)
    vo_ref[0, 0] = v[:, :HEAD_DIM]


def nsa_compress(cols, pe_k, w1_k, b1_k, w2_k, pe_v, w1_v, b1_v, w2_v, batch, seq):
    hk, d = NSA_KV_HEADS, HEAD_DIM
    n_half = seq // CMP_STRIDE
    half_w = CMP_STRIDE * d

    def halves(cb):
        x = cols[:, cb * 128:(cb + 1) * 128].reshape(batch, seq, hk, d).transpose(0, 2, 1, 3)
        return x.reshape(batch, hk, n_half, half_w)

    xk = halves(NSA_CB_KC)
    xv = halves(NSA_CB_VC)
    cmp_end = jnp.arange(n_half) * CMP_STRIDE + (CMP_BLOCK - 1)
    cos_t, sin_a, sin_b = _rope_tables(cmp_end, 2)
    pad_w2 = lambda w: jnp.pad(w, ((0, 0), (0, 128 - d)))
    full = lambda shape: pl.BlockSpec(shape, lambda b, h: (0,) * len(shape))
    xspec = pl.BlockSpec((1, 1, n_half, half_w), lambda b, h: (b, h, 0, 0))
    ospec = pl.BlockSpec((1, 1, n_half, d), lambda b, h: (b, h, 0, 0))
    wspecs = [full((1, 2 * half_w)), full((2 * half_w, CMP_HIDDEN)), full((1, CMP_HIDDEN)), full((CMP_HIDDEN, 128))]
    oshape = jax.ShapeDtypeStruct((batch, hk, n_half, d), F32)
    return pl.pallas_call(
        _compress_kernel,
        grid=(batch, hk),
        in_specs=[xspec, xspec] + wspecs + wspecs + [full((n_half, 128))] * 3,
        out_specs=[ospec, ospec],
        out_shape=[oshape, oshape],
        compiler_params=_cp(("parallel", "parallel")),
        name="nsa_compress",
    )(xk, xv, pe_k.reshape(1, -1), w1_k, b1_k.reshape(1, -1), pad_w2(w2_k),
      pe_v.reshape(1, -1), w1_v, b1_v.reshape(1, -1), pad_w2(w2_v), cos_t, sin_a, sin_b)


def _rope_t(x, cos_t, sin_a, sin_b):
    half = ROPE_DIM // 2
    up = jnp.concatenate([x[half:], x[:half]], axis=0)
    down = jnp.concatenate([x[-half:], x[:-half]], axis=0)
    return x * cos_t + up * sin_a + down * sin_b


def _nsa_attn_kernel(q_ref, ks_ref, vs_ref, kw_ref, vw_ref, gate_ref, kc_ref, vc_ref,
                     cosk_ref, sak_ref, sbk_ref, cosq_ref, saq_ref, sbq_ref, ovl_ref, get_ref,
                     o_ref, ks_s, vs_s, kw_s, vw_s, mem_s, *, seq):
    i = pl.program_id(1)
    tq = NSA_TQ
    d = HEAD_DIM
    g = NSA_GROUP
    cols = g * tq
    n_kv = NSA_KV_HEADS
    n_tiles = seq // tq

    @pl.when(i == 0)
    def _():
        ck, sa, sb = cosk_ref[...], sak_ref[...], sbk_ref[...]
        ks = _rope(ks_ref[...], ck, sa, sb)
        kw = _rope(kw_ref[...], ck, sa, sb)
        for hk in range(n_kv):
            hs = slice(hk * d, (hk + 1) * d)
            ks_s[hk] = ks[:, hs].astype(BF16)
            kw_s[hk] = kw[:, hs].astype(BF16)
        for j in range(n_tiles):
            rs = slice(j * tq, (j + 1) * tq)
            vs_s[j] = vs_ref[rs, :].T.astype(BF16)
            vw_s[j] = vw_ref[rs, :].T.astype(BF16)

    qt = _rope_t(q_ref[...].T, cosq_ref[...], saq_ref[...], sbq_ref[...]) * (HEAD_DIM ** -0.5)
    gexp = _dot_xr(get_ref[...], _sigmoid(gate_ref[...].T))

    pos_q = i * tq + (lax.broadcasted_iota(jnp.int32, (1, cols), 1) & (tq - 1))
    sub = lax.broadcasted_iota(jnp.int32, (tq, cols), 0)
    blk = lax.broadcasted_iota(jnp.int32, (32, tq), 0)
    cur = (i * tq + lax.broadcasted_iota(jnp.int32, (32, tq), 1)) // SLC_BLOCK
    sel_valid = blk <= cur
    sel_forced = (blk == 0) | (blk == cur) | (blk == cur - 1)

    q4 = [jnp.concatenate([qt[(hk * g + gg) * d:(hk * g + gg + 1) * d, :] for gg in range(g)], axis=1)
          for hk in range(n_kv)]
    q4b = [x.astype(BF16) for x in q4]

    cend = sub * CMP_STRIDE + (CMP_BLOCK - 1)
    cmask = (cend <= pos_q) & (sub < seq // CMP_STRIDE - 1)
    o_cmp = []
    for hk in range(n_kv):
        s = _dot3(kc_ref[0, hk], q4[hk])
        sm = jnp.where(cmask, s, MASK_NEG)
        e = jnp.exp(sm - jnp.max(sm, axis=0, keepdims=True))
        p = e / jnp.sum(e, axis=0, keepdims=True) * cmask.astype(F32)
        o_cmp.append(_dot1(vc_ref[0, hk], p, 0, 0))
        psum = p[:, 0:tq] + p[:, tq:2 * tq] + p[:, 2 * tq:3 * tq] + p[:, 3 * tq:4 * tq]
        imp = _dot_xr(ovl_ref[...], psum)
        imp = jnp.where(sel_forced, FORCED_SCORE, jnp.where(sel_valid, imp, -FORCED_SCORE))
        members = []
        for jj in range(32):
            rj = imp[jj:jj + 1, :]
            beats = (imp > rj) | ((imp == rj) & (blk < jj))
            cnt = jnp.sum(beats.astype(F32), axis=0, keepdims=True)
            members.append((cnt < SLC_TOPN).astype(F32))
        member = jnp.concatenate(members, axis=0)
        mem_s[hk] = jnp.concatenate([member] * g, axis=1)

    def flash(k_s, vt_s, j_lo, mask_fn):
        def body(j, carry):
            ks = [k_s[hk, pl.ds(j * tq, tq), :] for hk in range(n_kv)]
            vt = vt_s[j]
            ss = [_dg(ks[hk], q4b[hk]) for hk in range(n_kv)]
            rel = (pos_q - j * tq) - sub
            new = []
            ps = []
            for hk in range(n_kv):
                m_run, l_run, acc = carry[hk]
                s = jnp.where(mask_fn(hk, j, rel), ss[hk], MASK_NEG)
                m_new = jnp.maximum(m_run, jnp.max(s, axis=0, keepdims=True))
                alpha = jnp.exp(m_run - m_new)
                p = jnp.exp(s - m_new)
                l_new = alpha * l_run + jnp.sum(p, axis=0, keepdims=True)
                ps.append(p.astype(BF16))
                new.append((m_new, l_new, alpha * acc))
            pv = [_dg(vt[hk * d:(hk + 1) * d, :], ps[hk]) for hk in range(n_kv)]
            return tuple((m, l, a + o) for (m, l, a), o in zip(new, pv))

        one = (jnp.full((1, cols), MASK_NEG, F32), jnp.zeros((1, cols), F32), jnp.zeros((d, cols), F32))
        fin = lax.fori_loop(j_lo, i + 1, body, (one,) * n_kv)
        return [acc / l_fin for _, l_fin, acc in fin]

    first_block = sub < SLC_BLOCK

    def slc_mask(hk, j, rel):
        m0 = mem_s[hk, pl.ds(2 * j, 1), :]
        m1 = mem_s[hk, pl.ds(2 * j + 1, 1), :]
        return (jnp.where(first_block, m0, m1) > 0.5) & (rel >= 0)

    def win_mask(hk, j, rel):
        return (rel >= 0) & (rel < WINDOW)

    o_slc = flash(ks_s, vs_s, 0, slc_mask)
    o_win = flash(kw_s, vw_s, jnp.maximum(i - WINDOW // tq, 0), win_mask)

    outs = []
    for hk in range(n_kv):
        for gg in range(g):
            h = hk * g + gg
            cs = slice(gg * tq, (gg + 1) * tq)
            ge = lambda br: gexp[br * NSA_WIDTH + h * d: br * NSA_WIDTH + (h + 1) * d, :]
            outs.append(ge(0) * o_cmp[hk][:, cs] + ge(1) * o_slc[hk][:, cs] + ge(2) * o_win[hk][:, cs])
    o_ref[...] = jnp.concatenate(outs, axis=0).T


def nsa_mix(cols, pe_k, w1_k, b1_k, w2_k, pe_v, w1_v, b1_v, w2_v, batch, seq):
    t = cols.shape[0]
    tq = NSA_TQ
    nq = seq // tq
    d = HEAD_DIM
    k_cmp, v_cmp = nsa_compress(cols, pe_k, w1_k, b1_k, w2_k, pe_v, w1_v, b1_v, w2_v, batch, seq)
    pos = jnp.arange(seq)
    cos_k, sa_k, sb_k = _rope_tables(pos, 2)
    cos_q, sa_q, sb_q = (x.T for x in _rope_tables(pos, NSA_Q_HEADS))
    n_slc = seq // SLC_BLOCK
    n_half = seq // CMP_STRIDE
    blk_tok = np.arange(n_half - 1)[:, None] * CMP_STRIDE + np.arange(CMP_BLOCK)[None, :]
    overlap = np.any((blk_tok[:, :, None] // SLC_BLOCK) == np.arange(n_slc)[None, None, :], axis=1)
    ovl_t = np.zeros((32, 128), np.float32)
    ovl_t[:n_slc, :n_half - 1] = overlap.T
    gate_e = np.zeros((3 * NSA_WIDTH, 128), np.float32)
    for h in range(NSA_Q_HEADS):
        for br in range(3):
            gate_e[br * NSA_WIDTH + h * d: br * NSA_WIDTH + (h + 1) * d, h * 3 + br] = 1.0
    kvspec = lambda cb: pl.BlockSpec((seq, 128), lambda b, i: (b, cb))
    full = lambda shape: pl.BlockSpec(shape, lambda b, i: (0,) * len(shape))
    cspec = pl.BlockSpec((1, NSA_KV_HEADS, n_half, d), lambda b, i: (b, 0, 0, 0))
    kern = functools.partial(_nsa_attn_kernel, seq=seq)
    return pl.pallas_call(
        kern,
        grid=(batch, nq),
        in_specs=[
            pl.BlockSpec((tq, NSA_WIDTH), lambda b, i: (b * nq + i, 0)),
            kvspec(NSA_CB_KS), kvspec(NSA_CB_VS), kvspec(NSA_CB_KW), kvspec(NSA_CB_VW),
            pl.BlockSpec((tq, 128), lambda b, i: (b * nq + i, NSA_CB_GATE)),
            cspec, cspec,
            full((seq, 128)), full((seq, 128)), full((seq, 128)),
            pl.BlockSpec((NSA_WIDTH, tq), lambda b, i: (0, i)),
            pl.BlockSpec((NSA_WIDTH, tq), lambda b, i: (0, i)),
            pl.BlockSpec((NSA_WIDTH, tq), lambda b, i: (0, i)),
            full((32, 128)), full((3 * NSA_WIDTH, 128)),
        ],
        out_specs=pl.BlockSpec((tq, NSA_WIDTH), lambda b, i: (b * nq + i, 0)),
        out_shape=jax.ShapeDtypeStruct((t, NSA_WIDTH), F32),
        scratch_shapes=[
            pltpu.VMEM((NSA_KV_HEADS, seq, d), BF16),
            pltpu.VMEM((nq, NSA_KV_WIDTH, tq), BF16),
            pltpu.VMEM((NSA_KV_HEADS, seq, d), BF16),
            pltpu.VMEM((nq, NSA_KV_WIDTH, tq), BF16),
            pltpu.VMEM((NSA_KV_HEADS, 32, NSA_GROUP * tq), F32),
        ],
        compiler_params=_cp(("parallel", "arbitrary")),
        name="nsa_attn",
    )(cols, cols, cols, cols, cols, cols, k_cmp, v_cmp, cos_k, sa_k, sb_k, cos_q, sa_q, sb_q,
      jnp.asarray(ovl_t, BF16), jnp.asarray(gate_e, BF16))


HYB_NSA_PAD = 1792
HYB_COLS_PAD = HYB_NSA_PAD + RWKV_COLS_PAD


def _mm2_res_kernel(a1_ref, a2_ref, w1_ref, w2_ref, r_ref, o_ref):
    acc = jnp.dot(a1_ref[...].astype(BF16), w1_ref[...], preferred_element_type=F32)
    acc = acc + jnp.dot(a2_ref[...].astype(BF16), w2_ref[...], preferred_element_type=F32)
    o_ref[...] = r_ref[...] + acc


def matmul2_residual(a1, a2, w_bf16, res, tm, tn):
    t, k1 = a1.shape
    k2 = a2.shape[1]
    n = w_bf16.shape[1]
    return pl.pallas_call(
        _mm2_res_kernel,
        grid=(t // tm, n // tn),
        in_specs=[
            pl.BlockSpec((tm, k1), lambda i, j: (i, 0)),
            pl.BlockSpec((tm, k2), lambda i, j: (i, 0)),
            pl.BlockSpec((k1, tn), lambda i, j: (0, j)),
            pl.BlockSpec((k2, tn), lambda i, j: (k1 // k2, j)),
            pl.BlockSpec((tm, tn), lambda i, j: (i, j)),
        ],
        out_specs=pl.BlockSpec((tm, tn), lambda i, j: (i, j)),
        out_shape=jax.ShapeDtypeStruct((t, n), F32),
        compiler_params=_cp(("parallel", "arbitrary")),
        name="matmul2_residual",
    )(a1, a2, w_bf16, w_bf16, res)


def hybrid_block(h, g_norm, w_in, w_out, rwkv_params, nsa_params, batch, seq):
    w_rwkv = w_in[:, :RWKV_COLS]
    w_nsa = w_in[:, RWKV_COLS:]
    w_p = jnp.concatenate([
        jnp.pad(w_nsa, ((0, 0), (0, HYB_NSA_PAD - NSA_COLS))),
        jnp.pad(w_rwkv, ((0, 0), (0, RWKV_COLS_PAD - RWKV_COLS))),
    ], axis=1).astype(BF16)
    cols = norm_matmul(h, g_norm, w_p, tm=1024, tn=896)
    y_rwkv = rwkv7_mix(cols, *rwkv_params, batch=batch, seq=seq, col_block=1)
    y_nsa = nsa_mix(cols, *nsa_params, batch=batch, seq=seq)
    return matmul2_residual(y_rwkv, y_nsa, w_out.astype(BF16), h, tm=1024, tn=512)


def kernel(x, norm_mix, norm_ffn, norm_final, hyb_w_in, hyb_w_out, rwkv_mu, rwkv_w0, rwkv_w_up, rwkv_a0, rwkv_a_up, rwkv_g_up, rwkv_k_k, rwkv_k_a, rwkv_r_k, rwkv_gn_w, rwkv_gn_b, nsa_pe_k, nsa_w1_k, nsa_b1_k, nsa_w2_k, nsa_pe_v, nsa_w1_v, nsa_b1_v, nsa_w2_v, ssm_w_in, ssm_conv_w, ssm_conv_b, ssm_dt_bias, ssm_a_log, ssm_d, ssm_norm_w, ssm_w_out, moe_wg, moe_bg, moe_we, moe_be, moe_w_gate, moe_w_up, moe_w_down):
    batch, seq, d = x.shape
    depth = norm_mix.shape[0]
    h = x.reshape(batch * seq, d)
    for layer in range(depth):
        i = layer // 2
        if layer % 2 == 0:
            rwkv_params = (rwkv_mu[i], rwkv_w0[i], rwkv_w_up[i], rwkv_a0[i], rwkv_a_up[i], rwkv_g_up[i],
                           rwkv_k_k[i], rwkv_k_a[i], rwkv_r_k[i].reshape(-1), rwkv_gn_w[i], rwkv_gn_b[i])
            nsa_params = (nsa_pe_k[i], nsa_w1_k[i], nsa_b1_k[i], nsa_w2_k[i],
                          nsa_pe_v[i], nsa_w1_v[i], nsa_b1_v[i], nsa_w2_v[i])
            h = hybrid_block(h, norm_mix[layer], hyb_w_in[i], hyb_w_out[i], rwkv_params, nsa_params, batch, seq)
        else:
            h = mamba2_block(h, norm_mix[layer], ssm_w_in[i], ssm_conv_w[i], ssm_conv_b[i], ssm_dt_bias[i],
                             ssm_a_log[i], ssm_d[i], ssm_norm_w[i], ssm_w_out[i], batch, seq)
        h = hier_moe_block(h, norm_ffn[layer], moe_wg[layer], moe_bg[layer], moe_we[layer], moe_be[layer],
                           moe_w_gate[layer], moe_w_up[layer], moe_w_down[layer],
                           g_final=norm_final if layer == depth - 1 else None)
    return h.reshape(batch, seq, d)
```

```python
import functools
import math

import numpy as np
import jax
import jax.numpy as jnp
from jax import lax
from jax.experimental import pallas as pl
from jax.experimental.pallas import tpu as pltpu

F32 = jnp.float32
BF16 = jnp.bfloat16

D_MODEL = 1024
NORM_EPS = 1e-6
HEAD_DIM = 64
ROPE_DIM = 16
ROPE_THETA = 500000.0
MASK_NEG = -1e30

VMEM_LIMIT = 48 * 1024 * 1024


def _cp(sem, vmem=VMEM_LIMIT):
    return pltpu.CompilerParams(dimension_semantics=sem, vmem_limit_bytes=vmem)


def _dg(a, b, ca=1, cb=0):
    return lax.dot_general(a, b, (((ca,), (cb,)), ((), ())), preferred_element_type=F32)


def _split2(x):
    hi = x.astype(BF16)
    lo = (x - hi.astype(F32)).astype(BF16)
    return hi, lo


def _split3(x):
    hi = x.astype(BF16)
    r = x - hi.astype(F32)
    mid = r.astype(BF16)
    lo = (r - mid.astype(F32)).astype(BF16)
    return hi, mid, lo


def _dot1(a, b, ca=1, cb=0):
    return _dg(a.astype(BF16), b.astype(BF16), ca, cb)


def _dot3(a, b, ca=1, cb=0):
    ah, al = _split2(a)
    bh, bl = _split2(b)
    return _dg(ah, bh, ca, cb) + (_dg(ah, bl, ca, cb) + _dg(al, bh, ca, cb))


def _dot_xl(a, b01, ca=1, cb=0):
    h, m, l = _split3(a)
    return _dg(h, b01, ca, cb) + (_dg(m, b01, ca, cb) + _dg(l, b01, ca, cb))


def _dot_xr(a01, b, ca=1, cb=0):
    h, m, l = _split3(b)
    return _dg(a01, h, ca, cb) + (_dg(a01, m, ca, cb) + _dg(a01, l, ca, cb))


def _sigmoid(x):
    return 1.0 / (1.0 + jnp.exp(-x))


def _silu(x):
    return x * _sigmoid(x)


def _softplus(x):
    return jnp.maximum(x, 0.0) + jnp.log(1.0 + jnp.exp(-jnp.abs(x)))


def _norm_mm_kernel(x_ref, g_ref, w_ref, o_ref, xn_ref):
    @pl.when(pl.program_id(1) == 0)
    def _():
        x = x_ref[...]
        ms = jnp.mean(x * x, axis=-1, keepdims=True)
        xn_ref[...] = (x * lax.rsqrt(ms + NORM_EPS) * g_ref[...]).astype(BF16)

    o_ref[...] = jnp.dot(xn_ref[...], w_ref[...], preferred_element_type=F32)


def norm_matmul(x, g, w_bf16, tm, tn):
    t, d = x.shape
    n = w_bf16.shape[1]
    return pl.pallas_call(
        _norm_mm_kernel,
        grid=(t // tm, n // tn),
        in_specs=[
            pl.BlockSpec((tm, d), lambda i, j: (i, 0)),
            pl.BlockSpec((1, d), lambda i, j: (0, 0)),
            pl.BlockSpec((d, tn), lambda i, j: (0, j)),
        ],
        out_specs=pl.BlockSpec((tm, tn), lambda i, j: (i, j)),
        out_shape=jax.ShapeDtypeStruct((t, n), F32),
        scratch_shapes=[pltpu.VMEM((tm, d), BF16)],
        compiler_params=_cp(("parallel", "arbitrary")),
        name="norm_matmul",
    )(x, g.reshape(1, d), w_bf16)


def _mm_res_kernel(a_ref, w_ref, r_ref, o_ref):
    o_ref[...] = r_ref[...] + jnp.dot(a_ref[...].astype(BF16), w_ref[...], preferred_element_type=F32)


def matmul_residual(a, w_bf16, res, tm, tn):
    t, k = a.shape
    n = w_bf16.shape[1]
    return pl.pallas_call(
        _mm_res_kernel,
        grid=(t // tm, n // tn),
        in_specs=[
            pl.BlockSpec((tm, k), lambda i, j: (i, 0)),
            pl.BlockSpec((k, tn), lambda i, j: (0, j)),
            pl.BlockSpec((tm, tn), lambda i, j: (i, j)),
        ],
        out_specs=pl.BlockSpec((tm, tn), lambda i, j: (i, j)),
        out_shape=jax.ShapeDtypeStruct((t, n), F32),
        compiler_params=_cp(("parallel", "arbitrary")),
        name="matmul_residual",
    )(a, w_bf16, res)


MOE_GROUPS = 4
MOE_EPG = 8
MOE_EXPERTS = 32
MOE_HIDDEN = 512
MOE_TM = 256
MOE_ROUTE_TM = 512
MOE_DISPATCH_TM = 256
MOE_COMBINE_TM = 128


def _router_kernel(x_ref, g_ref, w_ref, b_ref, tri_ref, ids_ref, wts_ref, rank_ref, cnt_ref, run_ref):
    @pl.when(pl.program_id(0) == 0)
    def _():
        run_ref[...] = jnp.zeros_like(run_ref)

    x = x_ref[...]
    ms = jnp.mean(x * x, axis=-1, keepdims=True)
    u = x * lax.rsqrt(ms + NORM_EPS) * g_ref[...]
    lg = _dot3(u, w_ref[...]) + b_ref[...]
    tm = x.shape[0]
    lane = lax.broadcasted_iota(jnp.int32, (tm, 128), 1)
    gl = lg[:, :128]
    el = lg[:, 128:]
    gmask = lane < MOE_GROUPS
    gmax = jnp.max(jnp.where(gmask, gl, -jnp.inf), axis=-1, keepdims=True)
    ge = jnp.where(gmask, jnp.exp(gl - gmax), 0.0)
    pg = ge / jnp.sum(ge, axis=-1, keepdims=True)
    g_val = jnp.max(pg, axis=-1, keepdims=True)
    g_idx = jnp.min(jnp.where(gmask & (pg == g_val), lane, 128), axis=-1, keepdims=True)
    emask = (lane >= g_idx * MOE_EPG) & (lane < (g_idx + 1) * MOE_EPG)
    emax = jnp.max(jnp.where(emask, el, -jnp.inf), axis=-1, keepdims=True)
    ee = jnp.where(emask, jnp.exp(el - emax), 0.0)
    pe = ee / jnp.sum(ee, axis=-1, keepdims=True)
    v1 = jnp.max(jnp.where(emask, pe, -1.0), axis=-1, keepdims=True)
    i1 = jnp.min(jnp.where(emask & (pe == v1), lane, 128), axis=-1, keepdims=True)
    m2 = emask & (lane != i1)
    v2 = jnp.max(jnp.where(m2, pe, -1.0), axis=-1, keepdims=True)
    i2 = jnp.min(jnp.where(m2 & (pe == v2), lane, 128), axis=-1, keepdims=True)
    den = v1 + v2
    w1 = v1 / den * g_val
    w2 = v2 / den * g_val
    ids_ref[...] = jnp.where(lane == 0, i1, jnp.where(lane == 1, i2, 0))
    wts_ref[...] = jnp.where(lane == 0, w1, jnp.where(lane == 1, w2, 0.0))
    oh1 = (lane == i1).astype(F32)
    oh2 = (lane == i2).astype(F32)
    oh = oh1 + oh2
    base = run_ref[...] + _dg(tri_ref[...], oh.astype(BF16))
    r1 = jnp.sum(oh1 * base, axis=-1, keepdims=True)
    r2 = jnp.sum(oh2 * base, axis=-1, keepdims=True)
    rank_ref[...] = jnp.where(lane == 0, r1, jnp.where(lane == 1, r2, 0.0)).astype(jnp.int32)
    total = run_ref[...] + jnp.sum(oh, axis=0, keepdims=True)
    run_ref[...] = total
    cnt_ref[...] = total.astype(jnp.int32)


def moe_router(h, g, wg, bg, we, be):
    t, d = h.shape
    tm = MOE_ROUTE_TM
    w = jnp.zeros((d, 256), F32).at[:, :MOE_GROUPS].set(wg).at[:, 128:128 + MOE_EXPERTS].set(we)
    b = jnp.zeros((1, 256), F32).at[0, :MOE_GROUPS].set(bg).at[0, 128:128 + MOE_EXPERTS].set(be)
    tri = jnp.asarray(np.tril(np.ones((tm, tm)), -1), BF16)
    tok = pl.BlockSpec((tm, 128), lambda i: (i, 0))
    return pl.pallas_call(
        _router_kernel,
        grid=(t // tm,),
        in_specs=[
            pl.BlockSpec((tm, d), lambda i: (i, 0)),
            pl.BlockSpec((1, d), lambda i: (0, 0)),
            pl.BlockSpec((d, 256), lambda i: (0, 0)),
            pl.BlockSpec((1, 256), lambda i: (0, 0)),
            pl.BlockSpec((tm, tm), lambda i: (0, 0)),
        ],
        out_specs=[tok, tok, tok, pl.BlockSpec((1, 128), lambda i: (0, 0))],
        out_shape=[
            jax.ShapeDtypeStruct((t, 128), jnp.int32),
            jax.ShapeDtypeStruct((t, 128), F32),
            jax.ShapeDtypeStruct((t, 128), jnp.int32),
            jax.ShapeDtypeStruct((1, 128), jnp.int32),
        ],
        scratch_shapes=[pltpu.VMEM((1, 128), F32)],
        compiler_params=_cp(("arbitrary",)),
        name="moe_router",
    )(h, g.reshape(1, d), w, b, tri)


def _dispatch_kernel(dest_ref, h_ref, xs_in_ref, xs_ref, sem):
    del xs_in_ref
    n_tok = h_ref.shape[0]

    def row_copy(r, a):
        return pltpu.make_async_copy(h_ref.at[pl.ds(r, 1)], xs_ref.at[pl.ds(dest_ref[0, 0, a], 1)], sem)

    def start(r, carry):
        row_copy(r, 2 * r).start()
        row_copy(r, 2 * r + 1).start()
        return carry

    def wait(r, carry):
        row_copy(r, 2 * r).wait()
        row_copy(r, 2 * r + 1).wait()
        return carry

    lax.fori_loop(0, n_tok, start, 0, unroll=8)
    lax.fori_loop(0, n_tok, wait, 0, unroll=8)


def moe_dispatch(h, dest, np_rows):
    t, d = h.shape
    tm = MOE_DISPATCH_TM
    return pl.pallas_call(
        _dispatch_kernel,
        grid=(t // tm,),
        in_specs=[
            pl.BlockSpec((1, 1, 2 * tm), lambda i: (i, 0, 0), memory_space=pltpu.SMEM),
            pl.BlockSpec((tm, d), lambda i: (i, 0)),
            pl.BlockSpec(memory_space=pl.ANY),
        ],
        out_specs=pl.BlockSpec(memory_space=pl.ANY),
        out_shape=jax.ShapeDtypeStruct((np_rows, d), F32),
        scratch_shapes=[pltpu.SemaphoreType.DMA(())],
        input_output_aliases={2: 0},
        compiler_params=_cp(("arbitrary",)),
        name="moe_dispatch",
    )(dest.reshape(t // tm, 1, 2 * tm), h, jnp.zeros((np_rows, d), F32))


def _moe_ffn_kernel(te_ref, tv_ref, x_ref, g_ref, wg_ref, wu_ref, wd_ref, o_ref, wg_s, wu_s, wd_s):
    i = pl.program_id(0)
    prev = te_ref[jnp.maximum(i - 1, 0)]
    changed = (i == 0) | (te_ref[i] != prev)

    @pl.when(changed)
    def _():
        wg_s[...] = wg_ref[0].astype(BF16)
        wu_s[...] = wu_ref[0].astype(BF16)
        wd_s[...] = wd_ref[0].astype(BF16)

    @pl.when(tv_ref[i] > 0)
    def _():
        x = x_ref[...]
        ms = jnp.mean(x * x, axis=-1, keepdims=True)
        x = (x * lax.rsqrt(ms + NORM_EPS) * g_ref[...]).astype(BF16)
        gate = jnp.dot(x, wg_s[...], preferred_element_type=F32)
        up = jnp.dot(x, wu_s[...], preferred_element_type=F32)
        act = _silu(gate) * up
        o_ref[...] = jnp.dot(act.astype(BF16), wd_s[...], preferred_element_type=F32)

    @pl.when(tv_ref[i] == 0)
    def _():
        o_ref[...] = jnp.zeros_like(o_ref)


def moe_ffn(xs, g, tile_expert, tile_valid, w_gate, w_up, w_down):
    np_rows, d = xs.shape
    f = w_gate.shape[-1]
    tm = MOE_TM
    n_tiles = np_rows // tm
    grid_spec = pltpu.PrefetchScalarGridSpec(
        num_scalar_prefetch=2,
        grid=(n_tiles,),
        in_specs=[
            pl.BlockSpec((tm, d), lambda i, te, tv: (i, 0)),
            pl.BlockSpec((1, d), lambda i, te, tv: (0, 0)),
            pl.BlockSpec((1, d, f), lambda i, te, tv: (te[i], 0, 0)),
            pl.BlockSpec((1, d, f), lambda i, te, tv: (te[i], 0, 0)),
            pl.BlockSpec((1, f, d), lambda i, te, tv: (te[i], 0, 0)),
        ],
        out_specs=pl.BlockSpec((tm, d), lambda i, te, tv: (i, 0)),
        scratch_shapes=[pltpu.VMEM((d, f), BF16), pltpu.VMEM((d, f), BF16), pltpu.VMEM((f, d), BF16)],
    )
    return pl.pallas_call(
        _moe_ffn_kernel,
        grid_spec=grid_spec,
        out_shape=jax.ShapeDtypeStruct((np_rows, d), F32),
        compiler_params=_cp(("arbitrary",)),
        name="moe_ffn",
    )(tile_expert, tile_valid, xs, g.reshape(1, d), w_gate, w_up, w_down)


def _combine_kernel(dcur_ref, dnxt_ref, h_ref, w_ref, g_ref, y_ref, o_ref, ybuf, sems, *, final_norm):
    i = pl.program_id(0)
    n = pl.num_programs(0)
    tm = h_ref.shape[0]
    slot = lax.rem(i, 2)

    def row_copy(dref, a, s):
        return pltpu.make_async_copy(y_ref.at[pl.ds(dref[0, 0, a], 1)], ybuf.at[s, pl.ds(a, 1)], sems.at[s])

    def fetch(dref, s):
        def start(a, carry):
            row_copy(dref, a, s).start()
            return carry
        lax.fori_loop(0, 2 * tm, start, 0, unroll=8)

    @pl.when(i == 0)
    def _():
        fetch(dcur_ref, 0)

    @pl.when(i + 1 < n)
    def _():
        fetch(dnxt_ref, 1 - slot)

    def wait(a, carry):
        row_copy(dcur_ref, a, slot).wait()
        return carry

    lax.fori_loop(0, 2 * tm, wait, 0, unroll=8)
    w = w_ref[...]
    x = h_ref[...] + w[:, 0:1] * ybuf[slot, 0:tm, :] + w[:, 1:2] * ybuf[slot, tm:2 * tm, :]
    if final_norm:
        ms = jnp.mean(x * x, axis=-1, keepdims=True)
        x = x * lax.rsqrt(ms + NORM_EPS) * g_ref[...]
    o_ref[...] = x


def moe_combine(h, y, dest, wts, g_final=None):
    t, d = h.shape
    tm = MOE_COMBINE_TM
    n = t // tm
    dtile = dest.reshape(n, tm, 2).transpose(0, 2, 1).reshape(n, 1, 2 * tm)
    g = jnp.ones((1, d), F32) if g_final is None else g_final.reshape(1, d)
    kern = functools.partial(_combine_kernel, final_norm=g_final is not None)
    return pl.pallas_call(
        kern,
        grid=(n,),
        in_specs=[
            pl.BlockSpec((1, 1, 2 * tm), lambda i: (i, 0, 0), memory_space=pltpu.SMEM),
            pl.BlockSpec((1, 1, 2 * tm), lambda i: (jnp.minimum(i + 1, n - 1), 0, 0), memory_space=pltpu.SMEM),
            pl.BlockSpec((tm, d), lambda i: (i, 0)),
            pl.BlockSpec((tm, 128), lambda i: (i, 0)),
            pl.BlockSpec((1, d), lambda i: (0, 0)),
            pl.BlockSpec(memory_space=pl.ANY),
        ],
        out_specs=pl.BlockSpec((tm, d), lambda i: (i, 0)),
        out_shape=jax.ShapeDtypeStruct((t, d), F32),
        scratch_shapes=[pltpu.VMEM((2, 2 * tm, d), F32), pltpu.SemaphoreType.DMA((2,))],
        compiler_params=_cp(("arbitrary",)),
        name="moe_combine",
    )(dtile, dtile, h, wts, g, y)


def hier_moe_block(h, g, wg, bg, we, be, w_gate, w_up, w_down, g_final=None):
    t, d = h.shape
    tm = MOE_TM
    ids, wts, rank, counts = moe_router(h, g, wg, bg, we, be)
    counts = counts[0, :MOE_EXPERTS]
    padded = ((counts + tm - 1) // tm) * tm
    pad_end = jnp.cumsum(padded)
    pad_off = pad_end - padded
    dest = jnp.take(pad_off, ids[:, :2]) + rank[:, :2]
    np_rows = 2 * t + MOE_EXPERTS * tm
    tile_start = jnp.arange(np_rows // tm, dtype=jnp.int32) * tm
    tile_expert = jnp.minimum(jnp.sum((tile_start[:, None] >= pad_end[None, :]).astype(jnp.int32), axis=1),
                              MOE_EXPERTS - 1)
    tile_valid = (tile_start < pad_end[-1]).astype(jnp.int32)
    xs = moe_dispatch(h, dest, np_rows)
    y = moe_ffn(xs, g, tile_expert, tile_valid, w_gate, w_up, w_down)
    return moe_combine(h, y, dest, wts, g_final)


SSM_D_INNER = 2048
SSM_HEADS = 32
SSM_GROUPS = 4
SSM_HPG = 8
SSM_STATE = 128
SSM_CONV = 4
SSM_CHUNK = 256
SSM_NORM_EPS = 1e-5
SSM_CONV_CH = SSM_D_INNER + 2 * SSM_GROUPS * SSM_STATE
SSM_IN_COLS = SSM_D_INNER + SSM_CONV_CH + SSM_HEADS
SSM_IN_PAD = 5376
SSM_GW = SSM_HPG * HEAD_DIM
SSD_X0 = SSM_D_INNER


def _ssd_kernel(x_ref, b_ref, c_ref, z_ref, dt_ref, cwx_ref, cwb_ref, cwc_ref, cbx_ref, cbb_ref, cbc_ref,
                dtb_ref, alog_ref, dsk_ref, nw_ref, sel_ref, e8_ref, tri_ref, o_ref, h_ref, tail_ref):
    c = pl.program_id(2)
    lc = SSM_CHUNK

    @pl.when(c == 0)
    def _():
        h_ref[...] = jnp.zeros_like(h_ref)
        tail_ref[...] = jnp.zeros_like(tail_ref)

    def conv_silu(raw_ref, w_ref, bias_ref, lo):
        raw = raw_ref[...]
        hi = lo + raw.shape[1]
        xx = jnp.concatenate([tail_ref[:, lo:hi], raw], axis=0)
        w = w_ref[...]
        acc = bias_ref[...] + w[SSM_CONV - 1:SSM_CONV, :] * raw
        for k in range(1, SSM_CONV):
            acc = acc + w[SSM_CONV - 1 - k:SSM_CONV - k, :] * xx[8 - k:8 - k + lc, :]
        tail_ref[:, lo:hi] = raw[lc - 8:lc, :]
        return _silu(acc)

    x = conv_silu(x_ref, cwx_ref, cbx_ref, 0)
    bm = conv_silu(b_ref, cwb_ref, cbb_ref, SSM_GW)
    cm = conv_silu(c_ref, cwc_ref, cbc_ref, SSM_GW + SSM_STATE)

    sel = sel_ref[0]
    e8 = e8_ref[...]
    tri = tri_ref[...]
    dt_all = _softplus(dt_ref[...] + dtb_ref[...])
    a_all = dt_all * (-jnp.exp(alog_ref[...]))
    dt = _dot_xl(dt_all, sel)
    cs = _dot_xl(_dot_xr(tri, a_all), sel)
    cst = cs.T
    cs_last = cs[lc - 1:lc, :]

    xdt = x * _dot_xl(dt, e8)
    cb = _dot1(cm, bm, 1, 1)
    row = lax.broadcasted_iota(jnp.int32, (lc, lc), 0)
    col = lax.broadcasted_iota(jnp.int32, (lc, lc), 1)
    causal = row >= col
    ys = []
    for hh in range(SSM_HPG):
        lmat = jnp.where(causal, jnp.exp(cs[:, hh:hh + 1] - cst[hh:hh + 1, :]), 0.0)
        ys.append(_dot1(cb * lmat, xdt[:, hh * HEAD_DIM:(hh + 1) * HEAD_DIM]))
    y = jnp.concatenate(ys, axis=1)

    h_in = h_ref[...]
    y = y + _dot1(cm, h_in) * _dot_xl(jnp.exp(cs), e8)
    xdec = xdt * _dot_xl(jnp.exp(cs_last - cs), e8)
    h_ref[...] = h_in * _dot_xl(jnp.exp(cs_last), e8) + _dot1(bm, xdec, 0, 0)

    y = y + x * dsk_ref[...]
    zz = z_ref[...]
    y = y * _silu(zz)
    y = y * lax.rsqrt(jnp.mean(y * y, axis=-1, keepdims=True) + SSM_NORM_EPS) * nw_ref[...]
    o_ref[...] = y.astype(BF16)


def ssd_mix(zx, conv_w, conv_b, dt_bias, a_log, d_skip, norm_w, batch, seq):
    t = zx.shape[0]
    lc = SSM_CHUNK
    nc = seq // lc
    g, hp = SSM_GROUPS, SSM_HPG
    lanes = 128
    pad_row = lambda v: jnp.pad(v, (0, lanes - v.shape[0])).reshape(1, lanes)
    sel = np.zeros((g, lanes, lanes), np.float32)
    for gg in range(g):
        sel[gg, gg * hp + np.arange(hp), np.arange(hp)] = 1.0
    e8 = np.zeros((lanes, SSM_GW), np.float32)
    e8[:hp] = np.kron(np.eye(hp), np.ones((1, HEAD_DIM)))
    tri = jnp.asarray(np.tril(np.ones((lc, lc))), BF16)
    d_exp = jnp.repeat(d_skip, HEAD_DIM).reshape(1, SSM_D_INNER)
    xb = SSD_X0 // SSM_GW
    bb = (SSD_X0 + SSM_D_INNER) // SSM_STATE
    cb = bb + g
    wb = SSM_D_INNER // SSM_STATE
    row = lambda w, f: pl.BlockSpec((lc, w), lambda b, gg, c: (b * nc + c, f(gg)))
    par = lambda r, w, f: pl.BlockSpec((r, w), lambda b, gg, c: (0, f(gg)))
    const = lambda shape: pl.BlockSpec(shape, lambda b, gg, c: (0,) * len(shape))
    cb2 = conv_b.reshape(1, -1)
    return pl.pallas_call(
        _ssd_kernel,
        grid=(batch, g, nc),
        in_specs=[
            row(SSM_GW, lambda gg: xb + gg),
            row(SSM_STATE, lambda gg: bb + gg),
            row(SSM_STATE, lambda gg: cb + gg),
            row(SSM_GW, lambda gg: gg),
            row(lanes, lambda gg: (SSD_X0 + SSM_CONV_CH) // lanes),
            par(SSM_CONV, SSM_GW, lambda gg: gg),
            par(SSM_CONV, SSM_STATE, lambda gg: wb + gg),
            par(SSM_CONV, SSM_STATE, lambda gg: wb + g + gg),
            par(1, SSM_GW, lambda gg: gg),
            par(1, SSM_STATE, lambda gg: wb + gg),
            par(1, SSM_STATE, lambda gg: wb + g + gg),
            const((1, lanes)), const((1, lanes)),
            par(1, SSM_GW, lambda gg: gg),
            par(1, SSM_GW, lambda gg: gg),
            pl.BlockSpec((1, lanes, lanes), lambda b, gg, c: (gg, 0, 0)),
            const((lanes, SSM_GW)), const((lc, lc)),
        ],
        out_specs=pl.BlockSpec((lc, SSM_GW), lambda b, gg, c: (b * nc + c, gg)),
        out_shape=jax.ShapeDtypeStruct((t, SSM_D_INNER), BF16),
        scratch_shapes=[pltpu.VMEM((SSM_STATE, SSM_GW), F32), pltpu.VMEM((8, SSM_GW + 2 * SSM_STATE), F32)],
        compiler_params=_cp(("parallel", "parallel", "arbitrary")),
        name="ssd",
    )(zx, zx, zx, zx, zx, conv_w, conv_w, conv_w, cb2, cb2, cb2, pad_row(dt_bias), pad_row(a_log),
      d_exp, norm_w.reshape(1, -1), jnp.asarray(sel, BF16), jnp.asarray(e8, BF16), tri)


def mamba2_block(h, g_norm, w_in, conv_w, conv_b, dt_bias, a_log, d_skip, norm_w, w_out, batch, seq):
    w_in_p = jnp.pad(w_in, ((0, 0), (0, SSM_IN_PAD - SSM_IN_COLS))).astype(BF16)
    zx = norm_matmul(h, g_norm, w_in_p, tm=1024, tn=768)
    y = ssd_mix(zx, conv_w, conv_b, dt_bias, a_log, d_skip, norm_w, batch, seq)
    return matmul_residual(y, w_out.astype(BF16), h, tm=1024, tn=512)


RWKV_HEADS = 8
RWKV_WIDTH = 512
RWKV_DECAY_RANK = 32
RWKV_A_RANK = 32
RWKV_GATE_RANK = 96
RWKV_GN_EPS = 64e-5
RWKV_COLS = 3 * RWKV_WIDTH + RWKV_DECAY_RANK + RWKV_A_RANK + RWKV_GATE_RANK
RWKV_COLS_PAD = 1792
RWKV_LO = 3 * RWKV_WIDTH
RWKV_LO_W = RWKV_COLS_PAD - RWKV_LO
RWKV_CHUNK = 64


def _rwkv_prep_kernel(c_ref, p_ref, mu_ref, w0_ref, wup_ref, a0_ref, aup_ref, gup_ref, kk_ref, ka_ref, rk_ref,
                      bd_ref, tri_ref,
                      rt_ref, at_ref, bt_ref, kt_ref, v_ref, bh_ref, kh_ref, pc_ref, g_ref, bonus_ref, *, seq, tm):
    i = pl.program_id(0)
    w = RWKV_WIDTH
    c = c_ref[...]
    first = (i * tm) % seq == 0
    prev_row = jnp.where(first, 0.0, p_ref[7:8, :])
    row = lax.broadcasted_iota(jnp.int32, c.shape, 0)
    shifted = jnp.where(row == 0, prev_row, pltpu.roll(c, 1, axis=0))
    z = c + (shifted - c) * mu_ref[...]
    r = z[:, 0:w]
    k = z[:, w:2 * w]
    v = z[:, 2 * w:3 * w]
    lo = z[:, RWKV_LO:RWKV_COLS_PAD]
    w_log = -_softplus(-(w0_ref[...] + _dot3(jnp.tanh(lo), wup_ref[...]))) - 0.5
    lw = -jnp.exp(w_log)
    a = _sigmoid(a0_ref[...] + _dot3(lo, aup_ref[...]))
    g_ref[...] = _dot3(_sigmoid(lo), gup_ref[...])
    bd = bd_ref[...]
    kk = k * kk_ref[...]
    kk = kk * lax.rsqrt(jnp.maximum(_dot_xl(kk * kk, bd), 1e-24))
    k = k * (1.0 + (a - 1.0) * ka_ref[...])
    bonus_ref[...] = _dot_xl(r * k * rk_ref[...], bd) * v
    v_ref[...] = v
    av = -kk
    bv = kk * a
    tri = tri_ref[...]
    lc = RWKV_CHUNK
    pcs = []
    for cc in range(tm // lc):
        sl = slice(cc * lc, (cc + 1) * lc)
        lwc = lw[sl]
        cs = _dot_xr(tri, lwc)
        cs_end = cs[lc - 1:lc, :]
        e_in = jnp.exp(cs)
        e_out = jnp.exp(-cs)
        e_end = jnp.exp(cs_end - cs)
        rt_ref[sl, :] = r[sl] * e_in
        at_ref[sl, :] = av[sl] * jnp.exp(cs - lwc)
        bt_ref[sl, :] = bv[sl] * e_out
        kt_ref[sl, :] = k[sl] * e_out
        bh_ref[sl, :] = bv[sl] * e_end
        kh_ref[sl, :] = k[sl] * e_end
        pcs.append(jnp.exp(cs_end))
    pc_ref[...] = jnp.concatenate(pcs, axis=0)


def _rwkv_chunk_kernel(rt_ref, at_ref, bt_ref, kt_ref, v_ref, bh_ref, kh_ref, pc_ref,
                       rh_ref, yh_ref, phi_ref, psi_ref):
    lc = RWKV_CHUNK
    d = HEAD_DIM
    row = lax.broadcasted_iota(jnp.int32, (lc, lc), 0)
    col = lax.broadcasted_iota(jnp.int32, (lc, lc), 1)
    strict = row > col
    incl = row >= col
    eye = (row == col).astype(F32)
    n_ch = rt_ref.shape[0] // lc
    probs = [(cc, hh) for cc in range(n_ch) for hh in range(RWKV_HEADS)]

    def load(ref):
        return [ref[cc * lc:(cc + 1) * lc, hh * d:(hh + 1) * d] for cc, hh in probs]

    def each(fn, *lists):
        return [fn(*xs) for xs in zip(*lists)]

    rt, at, bt, kt, vv, bh, kh = (load(r) for r in (rt_ref, at_ref, bt_ref, kt_ref, v_ref, bh_ref, kh_ref))
    lowp = lambda xs: [x.astype(BF16) for x in xs]
    rtb, atb, btb, ktb, vvb, bhb, khb = (lowp(x) for x in (rt, at, bt, kt, vv, bh, kh))
    nt = lambda a, b: _dg(a, b, 1, 1)
    tn = lambda a, b: _dg(a, b, 0, 0)
    lab = each(lambda a, b: jnp.where(strict, nt(a, b), 0.0), atb, btb)
    lak = each(lambda a, b: jnp.where(strict, nt(a, b), 0.0), atb, ktb)
    mrb = lowp(each(lambda a, b: jnp.where(incl, nt(a, b), 0.0), rtb, btb))
    mrk = lowp(each(lambda a, b: jnp.where(incl, nt(a, b), 0.0), rtb, ktb))
    tinv = each(lambda l: eye + l, lab)
    lp = lowp(lab)
    for _ in range(5):
        lp = lowp(each(_dg, lp, lp))
        tinv = each(lambda t, l: t + _dg(t.astype(BF16), l), tinv, lp)
    tinv = lowp(tinv)
    lv = lowp(each(_dg, lowp(lak), vvb))
    uh = lowp(each(_dg, tinv, lv))
    ah = lowp(each(_dg, tinv, atb))
    rh = each(lambda r, m, a: r + _dg(m, a), rt, mrb, ah)
    yh = each(lambda m, u, mk, v: _dg(m, u) + _dg(mk, v), mrb, uh, mrk, vvb)
    phi = each(lambda b, a: tn(b, a), bhb, ah)
    psi = each(lambda b, u, k, v: tn(b, u) + tn(k, v), bhb, uh, khb, vvb)
    for cc in range(n_ch):
        sl = slice(cc * lc, (cc + 1) * lc)
        ps = slice(cc * RWKV_HEADS, (cc + 1) * RWKV_HEADS)
        pc_row = pc_ref[cc]
        rh_ref[sl, :] = jnp.concatenate(rh[ps], axis=1)
        yh_ref[sl, :] = jnp.concatenate(yh[ps], axis=1)
        phi_ref[sl, :] = jnp.concatenate(phi[ps], axis=1) + jnp.concatenate([eye] * RWKV_HEADS, axis=1) * pc_row
        psi_ref[sl, :] = jnp.concatenate(psi[ps], axis=1)


def _rwkv_scan_kernel(rh_ref, yh_ref, phi_ref, psi_ref, bonus_ref, g_ref, gw_ref, gb_ref, bd_ref, o_ref, st_ref,
                      *, n_chunks):
    lc = RWKV_CHUNK
    d = HEAD_DIM

    @pl.when(pl.program_id(1) == 0)
    def _():
        st_ref[...] = jnp.zeros_like(st_ref)

    for cc in range(n_chunks):
        sl = slice(cc * lc, (cc + 1) * lc)
        hsl = [slice(hh * d, (hh + 1) * d) for hh in range(RWKV_HEADS)]
        stb = [st_ref[hh].astype(BF16) for hh in range(RWKV_HEADS)]
        ys = [_dg(rh_ref[sl, hs].astype(BF16), st) + yh_ref[sl, hs] for hs, st in zip(hsl, stb)]
        for hh, (hs, st) in enumerate(zip(hsl, stb)):
            st_ref[hh] = _dg(phi_ref[sl, hs].astype(BF16), st) + psi_ref[sl, hs]
        y = jnp.concatenate(ys, axis=1)
        bd = bd_ref[...]
        mean = _dot_xl(y, bd) * (1.0 / d)
        yc = y - mean
        var = _dot_xl(yc * yc, bd) * (1.0 / d)
        yn = yc * lax.rsqrt(var + RWKV_GN_EPS) * gw_ref[...] + gb_ref[...]
        o_ref[sl, :] = (yn + bonus_ref[sl, :]) * g_ref[sl, :]


def rwkv7_mix(cols, mu, w0, w_up, a0, a_up, g_up, k_k, k_a, r_k, gn_w, gn_b, batch, seq, col_block=0):
    t = cols.shape[0]
    w = RWKV_WIDTH
    lc = RWKV_CHUNK
    tm = 512
    row2 = lambda x: x.reshape(1, -1)
    mu_p = jnp.pad(mu, (0, RWKV_COLS_PAD - RWKV_COLS)).reshape(1, -1)
    r0 = RWKV_DECAY_RANK
    r1 = r0 + RWKV_A_RANK
    r2 = r1 + RWKV_GATE_RANK
    wup_p = jnp.zeros((RWKV_LO_W, w), F32).at[0:r0].set(w_up)
    aup_p = jnp.zeros((RWKV_LO_W, w), F32).at[r0:r1].set(a_up)
    gup_p = jnp.zeros((RWKV_LO_W, w), F32).at[r1:r2].set(g_up)
    bd = jnp.asarray(np.kron(np.eye(RWKV_HEADS), np.ones((HEAD_DIM, HEAD_DIM))), BF16)
    tri = jnp.asarray(np.tril(np.ones((lc, lc))), BF16)
    full = lambda shape: pl.BlockSpec(shape, lambda i: (0,) * len(shape))
    tok = pl.BlockSpec((tm, w), lambda i: (i, 0))
    big = jax.ShapeDtypeStruct((t, w), F32)
    prep = functools.partial(_rwkv_prep_kernel, seq=seq, tm=tm)
    rt, at, bt, kt, v, bh, kh, pc, g, bonus = pl.pallas_call(
        prep,
        grid=(t // tm,),
        in_specs=[
            pl.BlockSpec((tm, RWKV_COLS_PAD), lambda i: (i, col_block)),
            pl.BlockSpec((8, RWKV_COLS_PAD), lambda i: (jnp.maximum(i * (tm // 8) - 1, 0), col_block)),
            full((1, RWKV_COLS_PAD)),
            full((1, w)), full((RWKV_LO_W, w)), full((1, w)), full((RWKV_LO_W, w)), full((RWKV_LO_W, w)),
            full((1, w)), full((1, w)), full((1, w)), full((w, w)), full((lc, lc)),
        ],
        out_specs=[tok, tok, tok, tok, tok, tok, tok, pl.BlockSpec((tm // lc, w), lambda i: (i, 0)), tok, tok],
        out_shape=[big, big, big, big, big, big, big, jax.ShapeDtypeStruct((t // lc, w), F32), big, big],
        compiler_params=_cp(("parallel",)),
        name="rwkv_prep",
    )(cols, cols, mu_p, row2(w0), wup_p, row2(a0), aup_p, gup_p, row2(k_k), row2(k_a), row2(r_k), bd, tri)

    n_ch = 1
    ch = pl.BlockSpec((n_ch * lc, w), lambda i: (i, 0))
    rh, yh, phi, psi = pl.pallas_call(
        _rwkv_chunk_kernel,
        grid=(t // (n_ch * lc),),
        in_specs=[ch, ch, ch, ch, ch, ch, ch, pl.BlockSpec((n_ch, 1, w), lambda i: (i, 0, 0))],
        out_specs=[ch, ch, ch, ch],
        out_shape=[big, big, big, big],
        compiler_params=_cp(("parallel",)),
        name="rwkv_chunk",
    )(rt, at, bt, kt, v, bh, kh, pc.reshape(t // lc, 1, w))

    n_chunks = 4
    ts = n_chunks * lc
    nsb = seq // ts
    blk = pl.BlockSpec((ts, w), lambda b, j: (b * nsb + j, 0))
    scan = functools.partial(_rwkv_scan_kernel, n_chunks=n_chunks)
    return pl.pallas_call(
        scan,
        grid=(batch, nsb),
        in_specs=[blk, blk, blk, blk, blk, blk,
                  pl.BlockSpec((1, w), lambda b, j: (0, 0)), pl.BlockSpec((1, w), lambda b, j: (0, 0)),
                  pl.BlockSpec((w, w), lambda b, j: (0, 0))],
        out_specs=blk,
        out_shape=big,
        scratch_shapes=[pltpu.VMEM((RWKV_HEADS, HEAD_DIM, HEAD_DIM), F32)],
        compiler_params=_cp(("parallel", "arbitrary")),
        name="rwkv_scan",
    )(rh, yh, phi, psi, bonus, g, row2(gn_w), row2(gn_b), bd)


NSA_Q_HEADS = 8
NSA_KV_HEADS = 2
NSA_GROUP = 4
NSA_WIDTH = 512
NSA_KV_WIDTH = 128
NSA_COLS = NSA_WIDTH + 6 * NSA_KV_WIDTH + 3 * NSA_Q_HEADS
CMP_BLOCK = 32
CMP_STRIDE = 16
CMP_HIDDEN = 128
SLC_BLOCK = 64
SLC_TOPN = 16
WINDOW = 512
FORCED_SCORE = 1e6
NSA_TQ = 128
NSA_CB_KC, NSA_CB_VC, NSA_CB_KS, NSA_CB_VS, NSA_CB_KW, NSA_CB_VW, NSA_CB_GATE = 4, 5, 6, 7, 8, 9, 10


def _rope_tables(pos, reps):
    half = ROPE_DIM // 2
    inv = jnp.power(ROPE_THETA, -jnp.arange(half, dtype=F32) * (2.0 / ROPE_DIM))
    ang = pos.astype(F32)[:, None] * inv[None, :]
    cos = jnp.cos(ang)
    sin = jnp.sin(ang)
    p = pos.shape[0]
    ones = jnp.ones((p, HEAD_DIM - ROPE_DIM), F32)
    zeros = jnp.zeros((p, HEAD_DIM - half), F32)
    cos_t = jnp.concatenate([cos, cos, ones], axis=1)
    sin_a = jnp.concatenate([-sin, zeros], axis=1)
    sin_b = jnp.concatenate([jnp.zeros((p, half), F32), sin, jnp.zeros((p, HEAD_DIM - ROPE_DIM), F32)], axis=1)
    tile = lambda x: jnp.tile(x, (1, reps))
    return tile(cos_t), tile(sin_a), tile(sin_b)


def _rope(x, cos_t, sin_a, sin_b):
    n = x.shape[1]
    half = ROPE_DIM // 2
    return x * cos_t + pltpu.roll(x, n - half, axis=1) * sin_a + pltpu.roll(x, half, axis=1) * sin_b


def _compress_kernel(xk_ref, xv_ref, pek_ref, w1k_ref, b1k_ref, w2k_ref, pev_ref, w1v_ref, b1v_ref, w2v_ref,
                     cos_ref, sa_ref, sb_ref, ko_ref, vo_ref):
    n_half = ko_ref.shape[1]
    half = CMP_BLOCK // 2

    def mlp(x_ref, pe_ref, w1_ref, b1_ref, w2_ref):
        xs = [x_ref[pl.ds(l, n_half, stride=CMP_STRIDE), :] for l in range(half)]
        top = [_dot3(x + pe_ref[l:l + 1, :], w1_ref[l]) for l, x in enumerate(xs)]
        bot = [_dot3(x + pe_ref[half + l:half + l + 1, :], w1_ref[half + l]) for l, x in enumerate(xs)]
        hid = sum(top[1:], top[0]) + pltpu.roll(sum(bot[1:], bot[0]), n_half - 1, axis=0) + b1_ref[...]
        return _dot3(_silu(hid), w2_ref[...])

    k = mlp(xk_ref, pek_ref, w1k_ref, b1k_ref, w2k_ref)
    ko_ref[0] = _rope(k, cos_ref[...], sa_ref[...], sb_ref[...])
    vo_ref[0] = mlp(xv_ref, pev_ref, w1v_ref, b1v_ref, w2v_ref)


def nsa_compress(cols, pe_k, w1_k, b1_k, w2_k, pe_v, w1_v, b1_v, w2_v, batch, seq):
    hk, d = NSA_KV_HEADS, HEAD_DIM
    n_half = seq // CMP_STRIDE
    cmp_end = jnp.arange(n_half) * CMP_STRIDE + (CMP_BLOCK - 1)
    cos_t, sin_a, sin_b = _rope_tables(cmp_end, hk)
    eye = jnp.eye(hk, dtype=F32)
    pair = lambda w: jnp.kron(eye, w)
    w1_pair = lambda w: jax.vmap(pair)(w.reshape(CMP_BLOCK, d, CMP_HIDDEN))
    lanes2 = lambda v: jnp.tile(v, (1, hk))
    full = lambda shape: pl.BlockSpec(shape, lambda b: (0,) * len(shape))
    xspec = lambda cb: pl.BlockSpec((seq, hk * d), lambda b: (b, cb))
    ospec = pl.BlockSpec((1, n_half, hk * d), lambda b: (b, 0, 0))
    wspecs = [full((CMP_BLOCK, hk * d)), full((CMP_BLOCK, hk * d, hk * CMP_HIDDEN)), full((1, hk * CMP_HIDDEN)),
              full((hk * CMP_HIDDEN, hk * d))]
    oshape = jax.ShapeDtypeStruct((batch, n_half, hk * d), F32)
    return pl.pallas_call(
        _compress_kernel,
        grid=(batch,),
        in_specs=[xspec(NSA_CB_KC), xspec(NSA_CB_VC)] + wspecs + wspecs + [full((n_half, hk * d))] * 3,
        out_specs=[ospec, ospec],
        out_shape=[oshape, oshape],
        compiler_params=_cp(("parallel",)),
        name="nsa_compress",
    )(cols, cols, lanes2(pe_k), w1_pair(w1_k), lanes2(b1_k.reshape(1, -1)), pair(w2_k),
      lanes2(pe_v), w1_pair(w1_v), lanes2(b1_v.reshape(1, -1)), pair(w2_v), cos_t, sin_a, sin_b)


def _rope_t(x, cos_t, sin_a, sin_b):
    half = ROPE_DIM // 2
    up = jnp.concatenate([x[half:], x[:half]], axis=0)
    down = jnp.concatenate([x[-half:], x[:-half]], axis=0)
    return x * cos_t + up * sin_a + down * sin_b


def _nsa_attn_kernel(q_ref, ks_ref, vs_ref, kw_ref, vw_ref, gate_ref, kc_ref, vc_ref,
                     cosk_ref, sak_ref, sbk_ref, cosq_ref, saq_ref, sbq_ref, ovl_ref, get_ref,
                     o_ref, ks_s, vs_s, kw_s, vw_s, mem_s, *, seq):
    i = pl.program_id(1)
    tq = NSA_TQ
    d = HEAD_DIM
    g = NSA_GROUP
    cols = g * tq
    n_kv = NSA_KV_HEADS
    n_tiles = seq // tq

    @pl.when(i == 0)
    def _():
        ck, sa, sb = cosk_ref[...], sak_ref[...], sbk_ref[...]
        ks = _rope(ks_ref[...], ck, sa, sb)
        kw = _rope(kw_ref[...], ck, sa, sb)
        for hk in range(n_kv):
            hs = slice(hk * d, (hk + 1) * d)
            ks_s[hk] = ks[:, hs].astype(BF16)
            kw_s[hk] = kw[:, hs].astype(BF16)
        for j in range(n_tiles):
            rs = slice(j * tq, (j + 1) * tq)
            vs_s[j] = vs_ref[rs, :].T.astype(BF16)
            vw_s[j] = vw_ref[rs, :].T.astype(BF16)

    qt = _rope_t(q_ref[...].T, cosq_ref[...], saq_ref[...], sbq_ref[...]) * (HEAD_DIM ** -0.5)
    gexp = _dot_xr(get_ref[...], _sigmoid(gate_ref[...].T))

    pos_q = i * tq + (lax.broadcasted_iota(jnp.int32, (1, cols), 1) & (tq - 1))
    sub = lax.broadcasted_iota(jnp.int32, (tq, cols), 0)
    blk = lax.broadcasted_iota(jnp.int32, (32, tq), 0)
    cur = (i * tq + lax.broadcasted_iota(jnp.int32, (32, tq), 1)) // SLC_BLOCK
    sel_valid = blk <= cur
    sel_forced = (blk == 0) | (blk == cur) | (blk == cur - 1)

    q4 = [jnp.concatenate([qt[(hk * g + gg) * d:(hk * g + gg + 1) * d, :] for gg in range(g)], axis=1)
          for hk in range(n_kv)]
    q4b = [x.astype(BF16) for x in q4]

    cend = sub * CMP_STRIDE + (CMP_BLOCK - 1)
    cmask = (cend <= pos_q) & (sub < seq // CMP_STRIDE - 1)
    o_cmp = []
    for hk in range(n_kv):
        s = _dot3(kc_ref[0, :, hk * d:(hk + 1) * d], q4[hk])
        sm = jnp.where(cmask, s, MASK_NEG)
        e = jnp.exp(sm - jnp.max(sm, axis=0, keepdims=True))
        p = e / jnp.sum(e, axis=0, keepdims=True) * cmask.astype(F32)
        o_cmp.append(_dot1(vc_ref[0, :, hk * d:(hk + 1) * d], p, 0, 0))
        psum = p[:, 0:tq] + p[:, tq:2 * tq] + p[:, 2 * tq:3 * tq] + p[:, 3 * tq:4 * tq]
        imp = _dot_xr(ovl_ref[...], psum)
        imp = jnp.where(sel_forced, FORCED_SCORE, jnp.where(sel_valid, imp, -FORCED_SCORE))
        members = []
        for jj in range(32):
            rj = imp[jj:jj + 1, :]
            beats = (imp > rj) | ((imp == rj) & (blk < jj))
            cnt = jnp.sum(beats.astype(F32), axis=0, keepdims=True)
            members.append((cnt < SLC_TOPN).astype(F32))
        member = jnp.concatenate(members, axis=0)
        mem_s[hk] = jnp.concatenate([member] * g, axis=1)

    def flash(k_s, vt_s, j_lo, mask_fn):
        def body(j, carry):
            ks = [k_s[hk, pl.ds(j * tq, tq), :] for hk in range(n_kv)]
            vt = vt_s[j]
            ss = [_dg(ks[hk], q4b[hk]) for hk in range(n_kv)]
            rel = (pos_q - j * tq) - sub
            new = []
            ps = []
            for hk in range(n_kv):
                m_run, l_run, acc = carry[hk]
                s = jnp.where(mask_fn(hk, j, rel), ss[hk], MASK_NEG)
                m_new = jnp.maximum(m_run, jnp.max(s, axis=0, keepdims=True))
                alpha = jnp.exp(m_run - m_new)
                p = jnp.exp(s - m_new)
                l_new = alpha * l_run + jnp.sum(p, axis=0, keepdims=True)
                ps.append(p.astype(BF16))
                new.append((m_new, l_new, alpha * acc))
            pv = [_dg(vt[hk * d:(hk + 1) * d, :], ps[hk]) for hk in range(n_kv)]
            return tuple((m, l, a + o) for (m, l, a), o in zip(new, pv))

        one = (jnp.full((1, cols), MASK_NEG, F32), jnp.zeros((1, cols), F32), jnp.zeros((d, cols), F32))
        fin = lax.fori_loop(j_lo, i + 1, body, (one,) * n_kv)
        return [acc / l_fin for _, l_fin, acc in fin]

    first_block = sub < SLC_BLOCK

    def slc_mask(hk, j, rel):
        m0 = mem_s[hk, pl.ds(2 * j, 1), :]
        m1 = mem_s[hk, pl.ds(2 * j + 1, 1), :]
        return (jnp.where(first_block, m0, m1) > 0.5) & (rel >= 0)

    def win_mask(hk, j, rel):
        return (rel >= 0) & (rel < WINDOW)

    o_slc = flash(ks_s, vs_s, 0, slc_mask)
    o_win = flash(kw_s, vw_s, jnp.maximum(i - WINDOW // tq, 0), win_mask)

    outs = []
    for hk in range(n_kv):
        for gg in range(g):
            h = hk * g + gg
            cs = slice(gg * tq, (gg + 1) * tq)
            ge = lambda br: gexp[br * NSA_WIDTH + h * d: br * NSA_WIDTH + (h + 1) * d, :]
            outs.append(ge(0) * o_cmp[hk][:, cs] + ge(1) * o_slc[hk][:, cs] + ge(2) * o_win[hk][:, cs])
    o_ref[...] = jnp.concatenate(outs, axis=0).T


def nsa_mix(cols, pe_k, w1_k, b1_k, w2_k, pe_v, w1_v, b1_v, w2_v, batch, seq):
    t = cols.shape[0]
    tq = NSA_TQ
    nq = seq // tq
    d = HEAD_DIM
    k_cmp, v_cmp = nsa_compress(cols, pe_k, w1_k, b1_k, w2_k, pe_v, w1_v, b1_v, w2_v, batch, seq)
    pos = jnp.arange(seq)
    cos_k, sa_k, sb_k = _rope_tables(pos, 2)
    cos_q, sa_q, sb_q = (x.T for x in _rope_tables(pos, NSA_Q_HEADS))
    n_slc = seq // SLC_BLOCK
    n_half = seq // CMP_STRIDE
    blk_tok = np.arange(n_half - 1)[:, None] * CMP_STRIDE + np.arange(CMP_BLOCK)[None, :]
    overlap = np.any((blk_tok[:, :, None] // SLC_BLOCK) == np.arange(n_slc)[None, None, :], axis=1)
    ovl_t = np.zeros((32, 128), np.float32)
    ovl_t[:n_slc, :n_half - 1] = overlap.T
    gate_e = np.zeros((3 * NSA_WIDTH, 128), np.float32)
    for h in range(NSA_Q_HEADS):
        for br in range(3):
            gate_e[br * NSA_WIDTH + h * d: br * NSA_WIDTH + (h + 1) * d, h * 3 + br] = 1.0
    kvspec = lambda cb: pl.BlockSpec((seq, 128), lambda b, i: (b, cb))
    full = lambda shape: pl.BlockSpec(shape, lambda b, i: (0,) * len(shape))
    cspec = pl.BlockSpec((1, n_half, NSA_KV_WIDTH), lambda b, i: (b, 0, 0))
    kern = functools.partial(_nsa_attn_kernel, seq=seq)
    return pl.pallas_call(
        kern,
        grid=(batch, nq),
        in_specs=[
            pl.BlockSpec((tq, NSA_WIDTH), lambda b, i: (b * nq + i, 0)),
            kvspec(NSA_CB_KS), kvspec(NSA_CB_VS), kvspec(NSA_CB_KW), kvspec(NSA_CB_VW),
            pl.BlockSpec((tq, 128), lambda b, i: (b * nq + i, NSA_CB_GATE)),
            cspec, cspec,
            full((seq, 128)), full((seq, 128)), full((seq, 128)),
            pl.BlockSpec((NSA_WIDTH, tq), lambda b, i: (0, i)),
            pl.BlockSpec((NSA_WIDTH, tq), lambda b, i: (0, i)),
            pl.BlockSpec((NSA_WIDTH, tq), lambda b, i: (0, i)),
            full((32, 128)), full((3 * NSA_WIDTH, 128)),
        ],
        out_specs=pl.BlockSpec((tq, NSA_WIDTH), lambda b, i: (b * nq + i, 0)),
        out_shape=jax.ShapeDtypeStruct((t, NSA_WIDTH), F32),
        scratch_shapes=[
            pltpu.VMEM((NSA_KV_HEADS, seq, d), BF16),
            pltpu.VMEM((nq, NSA_KV_WIDTH, tq), BF16),
            pltpu.VMEM((NSA_KV_HEADS, seq, d), BF16),
            pltpu.VMEM((nq, NSA_KV_WIDTH, tq), BF16),
            pltpu.VMEM((NSA_KV_HEADS, 32, NSA_GROUP * tq), F32),
        ],
        compiler_params=_cp(("parallel", "arbitrary")),
        name="nsa_attn",
    )(cols, cols, cols, cols, cols, cols, k_cmp, v_cmp, cos_k, sa_k, sb_k, cos_q, sa_q, sb_q,
      jnp.asarray(ovl_t, BF16), jnp.asarray(gate_e, BF16))


HYB_NSA_PAD = 1792
HYB_COLS_PAD = HYB_NSA_PAD + RWKV_COLS_PAD


def _mm2_res_kernel(a1_ref, a2_ref, w1_ref, w2_ref, r_ref, o_ref):
    acc = jnp.dot(a1_ref[...].astype(BF16), w1_ref[...], preferred_element_type=F32)
    acc = acc + jnp.dot(a2_ref[...].astype(BF16), w2_ref[...], preferred_element_type=F32)
    o_ref[...] = r_ref[...] + acc


def matmul2_residual(a1, a2, w_bf16, res, tm, tn):
    t, k1 = a1.shape
    k2 = a2.shape[1]
    n = w_bf16.shape[1]
    return pl.pallas_call(
        _mm2_res_kernel,
        grid=(t // tm, n // tn),
        in_specs=[
            pl.BlockSpec((tm, k1), lambda i, j: (i, 0)),
            pl.BlockSpec((tm, k2), lambda i, j: (i, 0)),
            pl.BlockSpec((k1, tn), lambda i, j: (0, j)),
            pl.BlockSpec((k2, tn), lambda i, j: (k1 // k2, j)),
            pl.BlockSpec((tm, tn), lambda i, j: (i, j)),
        ],
        out_specs=pl.BlockSpec((tm, tn), lambda i, j: (i, j)),
        out_shape=jax.ShapeDtypeStruct((t, n), F32),
        compiler_params=_cp(("parallel", "arbitrary")),
        name="matmul2_residual",
    )(a1, a2, w_bf16, w_bf16, res)


def hybrid_block(h, g_norm, w_in, w_out, rwkv_params, nsa_params, batch, seq):
    w_rwkv = w_in[:, :RWKV_COLS]
    w_nsa = w_in[:, RWKV_COLS:]
    w_p = jnp.concatenate([
        jnp.pad(w_nsa, ((0, 0), (0, HYB_NSA_PAD - NSA_COLS))),
        jnp.pad(w_rwkv, ((0, 0), (0, RWKV_COLS_PAD - RWKV_COLS))),
    ], axis=1).astype(BF16)
    cols = norm_matmul(h, g_norm, w_p, tm=1024, tn=896)
    y_rwkv = rwkv7_mix(cols, *rwkv_params, batch=batch, seq=seq, col_block=1)
    y_nsa = nsa_mix(cols, *nsa_params, batch=batch, seq=seq)
    return matmul2_residual(y_rwkv, y_nsa, w_out.astype(BF16), h, tm=1024, tn=512)


def kernel(x, norm_mix, norm_ffn, norm_final, hyb_w_in, hyb_w_out, rwkv_mu, rwkv_w0, rwkv_w_up, rwkv_a0, rwkv_a_up, rwkv_g_up, rwkv_k_k, rwkv_k_a, rwkv_r_k, rwkv_gn_w, rwkv_gn_b, nsa_pe_k, nsa_w1_k, nsa_b1_k, nsa_w2_k, nsa_pe_v, nsa_w1_v, nsa_b1_v, nsa_w2_v, ssm_w_in, ssm_conv_w, ssm_conv_b, ssm_dt_bias, ssm_a_log, ssm_d, ssm_norm_w, ssm_w_out, moe_wg, moe_bg, moe_we, moe_be, moe_w_gate, moe_w_up, moe_w_down):
    batch, seq, d = x.shape
    depth = norm_mix.shape[0]
    h = x.reshape(batch * seq, d)
    for layer in range(depth):
        i = layer // 2
        if layer % 2 == 0:
            rwkv_params = (rwkv_mu[i], rwkv_w0[i], rwkv_w_up[i], rwkv_a0[i], rwkv_a_up[i], rwkv_g_up[i],
                           rwkv_k_k[i], rwkv_k_a[i], rwkv_r_k[i].reshape(-1), rwkv_gn_w[i], rwkv_gn_b[i])
            nsa_params = (nsa_pe_k[i], nsa_w1_k[i], nsa_b1_k[i], nsa_w2_k[i],
                          nsa_pe_v[i], nsa_w1_v[i], nsa_b1_v[i], nsa_w2_v[i])
            h = hybrid_block(h, norm_mix[layer], hyb_w_in[i], hyb_w_out[i], rwkv_params, nsa_params, batch, seq)
        else:
            h = mamba2_block(h, norm_mix[layer], ssm_w_in[i], ssm_conv_w[i], ssm_conv_b[i], ssm_dt_bias[i],
                             ssm_a_log[i], ssm_d[i], ssm_norm_w[i], ssm_w_out[i], batch, seq)
        h = hier_moe_block(h, norm_ffn[layer], moe_wg[layer], moe_bg[layer], moe_we[layer], moe_be[layer],
                           moe_w_gate[layer], moe_w_up[layer], moe_w_down[layer],
                           g_final=norm_final if layer == depth - 1 else None)
    return h.reshape(batch, seq, d)
```

```python
import functools
import math

import numpy as np
import jax
import jax.numpy as jnp
from jax import lax
from jax.experimental import pallas as pl
from jax.experimental.pallas import tpu as pltpu

F32 = jnp.float32
BF16 = jnp.bfloat16

D_MODEL = 1024
NORM_EPS = 1e-6
HEAD_DIM = 64
ROPE_DIM = 16
ROPE_THETA = 500000.0
MASK_NEG = -1e30

VMEM_LIMIT = 48 * 1024 * 1024


def _cp(sem, vmem=VMEM_LIMIT):
    return pltpu.CompilerParams(dimension_semantics=sem, vmem_limit_bytes=vmem)


def _dg(a, b, ca=1, cb=0):
    return lax.dot_general(a, b, (((ca,), (cb,)), ((), ())), preferred_element_type=F32)


def _split2(x):
    hi = x.astype(BF16)
    lo = (x - hi.astype(F32)).astype(BF16)
    return hi, lo


def _split3(x):
    hi = x.astype(BF16)
    r = x - hi.astype(F32)
    mid = r.astype(BF16)
    lo = (r - mid.astype(F32)).astype(BF16)
    return hi, mid, lo


def _dot1(a, b, ca=1, cb=0):
    return _dg(a.astype(BF16), b.astype(BF16), ca, cb)


def _dot3(a, b, ca=1, cb=0):
    ah, al = _split2(a)
    bh, bl = _split2(b)
    return _dg(ah, bh, ca, cb) + (_dg(ah, bl, ca, cb) + _dg(al, bh, ca, cb))


def _dot_xl(a, b01, ca=1, cb=0):
    h, m, l = _split3(a)
    return _dg(h, b01, ca, cb) + (_dg(m, b01, ca, cb) + _dg(l, b01, ca, cb))


def _dot_xr(a01, b, ca=1, cb=0):
    h, m, l = _split3(b)
    return _dg(a01, h, ca, cb) + (_dg(a01, m, ca, cb) + _dg(a01, l, ca, cb))


def _sigmoid(x):
    return 1.0 / (1.0 + jnp.exp(-x))


def _silu(x):
    return x * _sigmoid(x)


def _softplus(x):
    return jnp.maximum(x, 0.0) + jnp.log(1.0 + jnp.exp(-jnp.abs(x)))


def _norm_mm_kernel(x_ref, g_ref, w_ref, o_ref, xn_ref):
    @pl.when(pl.program_id(1) == 0)
    def _():
        x = x_ref[...]
        ms = jnp.mean(x * x, axis=-1, keepdims=True)
        xn_ref[...] = (x * lax.rsqrt(ms + NORM_EPS) * g_ref[...]).astype(BF16)

    o_ref[...] = jnp.dot(xn_ref[...], w_ref[...], preferred_element_type=F32)


def norm_matmul(x, g, w_bf16, tm, tn):
    t, d = x.shape
    n = w_bf16.shape[1]
    return pl.pallas_call(
        _norm_mm_kernel,
        grid=(t // tm, n // tn),
        in_specs=[
            pl.BlockSpec((tm, d), lambda i, j: (i, 0)),
            pl.BlockSpec((1, d), lambda i, j: (0, 0)),
            pl.BlockSpec((d, tn), lambda i, j: (0, j)),
        ],
        out_specs=pl.BlockSpec((tm, tn), lambda i, j: (i, j)),
        out_shape=jax.ShapeDtypeStruct((t, n), F32),
        scratch_shapes=[pltpu.VMEM((tm, d), BF16)],
        compiler_params=_cp(("parallel", "arbitrary")),
        name="norm_matmul",
    )(x, g.reshape(1, d), w_bf16)


def _mm_res_kernel(a_ref, w_ref, r_ref, o_ref):
    o_ref[...] = r_ref[...] + jnp.dot(a_ref[...].astype(BF16), w_ref[...], preferred_element_type=F32)


def matmul_residual(a, w_bf16, res, tm, tn):
    t, k = a.shape
    n = w_bf16.shape[1]
    return pl.pallas_call(
        _mm_res_kernel,
        grid=(t // tm, n // tn),
        in_specs=[
            pl.BlockSpec((tm, k), lambda i, j: (i, 0)),
            pl.BlockSpec((k, tn), lambda i, j: (0, j)),
            pl.BlockSpec((tm, tn), lambda i, j: (i, j)),
        ],
        out_specs=pl.BlockSpec((tm, tn), lambda i, j: (i, j)),
        out_shape=jax.ShapeDtypeStruct((t, n), F32),
        compiler_params=_cp(("parallel", "arbitrary")),
        name="matmul_residual",
    )(a, w_bf16, res)


MOE_GROUPS = 4
MOE_EPG = 8
MOE_EXPERTS = 32
MOE_HIDDEN = 512
MOE_TM = 256
MOE_ROUTE_TM = 512
MOE_DISPATCH_TM = 256
MOE_COMBINE_TM = 128


def _router_kernel(x_ref, g_ref, w_ref, b_ref, tri_ref, ids_ref, wts_ref, rank_ref, cnt_ref, run_ref):
    @pl.when(pl.program_id(0) == 0)
    def _():
        run_ref[...] = jnp.zeros_like(run_ref)

    x = x_ref[...]
    ms = jnp.mean(x * x, axis=-1, keepdims=True)
    u = x * lax.rsqrt(ms + NORM_EPS) * g_ref[...]
    lg = _dot3(u, w_ref[...]) + b_ref[...]
    tm = x.shape[0]
    lane = lax.broadcasted_iota(jnp.int32, (tm, 128), 1)
    gl = lg[:, :128]
    el = lg[:, 128:]
    gmask = lane < MOE_GROUPS
    gmax = jnp.max(jnp.where(gmask, gl, -jnp.inf), axis=-1, keepdims=True)
    ge = jnp.where(gmask, jnp.exp(gl - gmax), 0.0)
    pg = ge / jnp.sum(ge, axis=-1, keepdims=True)
    g_val = jnp.max(pg, axis=-1, keepdims=True)
    g_idx = jnp.min(jnp.where(gmask & (pg == g_val), lane, 128), axis=-1, keepdims=True)
    emask = (lane >= g_idx * MOE_EPG) & (lane < (g_idx + 1) * MOE_EPG)
    emax = jnp.max(jnp.where(emask, el, -jnp.inf), axis=-1, keepdims=True)
    ee = jnp.where(emask, jnp.exp(el - emax), 0.0)
    pe = ee / jnp.sum(ee, axis=-1, keepdims=True)
    v1 = jnp.max(jnp.where(emask, pe, -1.0), axis=-1, keepdims=True)
    i1 = jnp.min(jnp.where(emask & (pe == v1), lane, 128), axis=-1, keepdims=True)
    m2 = emask & (lane != i1)
    v2 = jnp.max(jnp.where(m2, pe, -1.0), axis=-1, keepdims=True)
    i2 = jnp.min(jnp.where(m2 & (pe == v2), lane, 128), axis=-1, keepdims=True)
    den = v1 + v2
    w1 = v1 / den * g_val
    w2 = v2 / den * g_val
    ids_ref[...] = jnp.where(lane == 0, i1, jnp.where(lane == 1, i2, 0))
    wts_ref[...] = jnp.where(lane == 0, w1, jnp.where(lane == 1, w2, 0.0))
    oh1 = (lane == i1).astype(F32)
    oh2 = (lane == i2).astype(F32)
    oh = oh1 + oh2
    base = run_ref[...] + _dg(tri_ref[...], oh.astype(BF16))
    r1 = jnp.sum(oh1 * base, axis=-1, keepdims=True)
    r2 = jnp.sum(oh2 * base, axis=-1, keepdims=True)
    rank_ref[...] = jnp.where(lane == 0, r1, jnp.where(lane == 1, r2, 0.0)).astype(jnp.int32)
    total = run_ref[...] + jnp.sum(oh, axis=0, keepdims=True)
    run_ref[...] = total
    cnt_ref[...] = total.astype(jnp.int32)


def moe_router(h, g, wg, bg, we, be):
    t, d = h.shape
    tm = MOE_ROUTE_TM
    w = jnp.zeros((d, 256), F32).at[:, :MOE_GROUPS].set(wg).at[:, 128:128 + MOE_EXPERTS].set(we)
    b = jnp.zeros((1, 256), F32).at[0, :MOE_GROUPS].set(bg).at[0, 128:128 + MOE_EXPERTS].set(be)
    tri = jnp.asarray(np.tril(np.ones((tm, tm)), -1), BF16)
    tok = pl.BlockSpec((tm, 128), lambda i: (i, 0))
    return pl.pallas_call(
        _router_kernel,
        grid=(t // tm,),
        in_specs=[
            pl.BlockSpec((tm, d), lambda i: (i, 0)),
            pl.BlockSpec((1, d), lambda i: (0, 0)),
            pl.BlockSpec((d, 256), lambda i: (0, 0)),
            pl.BlockSpec((1, 256), lambda i: (0, 0)),
            pl.BlockSpec((tm, tm), lambda i: (0, 0)),
        ],
        out_specs=[tok, tok, tok, pl.BlockSpec((1, 128), lambda i: (0, 0))],
        out_shape=[
            jax.ShapeDtypeStruct((t, 128), jnp.int32),
            jax.ShapeDtypeStruct((t, 128), F32),
            jax.ShapeDtypeStruct((t, 128), jnp.int32),
            jax.ShapeDtypeStruct((1, 128), jnp.int32),
        ],
        scratch_shapes=[pltpu.VMEM((1, 128), F32)],
        compiler_params=_cp(("arbitrary",)),
        name="moe_router",
    )(h, g.reshape(1, d), w, b, tri)


def _dispatch_kernel(dest_ref, h_ref, xs_in_ref, xs_ref, sem):
    del xs_in_ref
    n_tok = h_ref.shape[0]

    def row_copy(r, a):
        return pltpu.make_async_copy(h_ref.at[pl.ds(r, 1)], xs_ref.at[pl.ds(dest_ref[0, 0, a], 1)], sem)

    def start(r, carry):
        row_copy(r, 2 * r).start()
        row_copy(r, 2 * r + 1).start()
        return carry

    def wait(r, carry):
        row_copy(r, 2 * r).wait()
        row_copy(r, 2 * r + 1).wait()
        return carry

    lax.fori_loop(0, n_tok, start, 0, unroll=8)
    lax.fori_loop(0, n_tok, wait, 0, unroll=8)


def moe_dispatch(h, dest, np_rows):
    t, d = h.shape
    tm = MOE_DISPATCH_TM
    return pl.pallas_call(
        _dispatch_kernel,
        grid=(t // tm,),
        in_specs=[
            pl.BlockSpec((1, 1, 2 * tm), lambda i: (i, 0, 0), memory_space=pltpu.SMEM),
            pl.BlockSpec((tm, d), lambda i: (i, 0)),
            pl.BlockSpec(memory_space=pl.ANY),
        ],
        out_specs=pl.BlockSpec(memory_space=pl.ANY),
        out_shape=jax.ShapeDtypeStruct((np_rows, d), F32),
        scratch_shapes=[pltpu.SemaphoreType.DMA(())],
        input_output_aliases={2: 0},
        compiler_params=_cp(("arbitrary",)),
        name="moe_dispatch",
    )(dest.reshape(t // tm, 1, 2 * tm), h, jnp.zeros((np_rows, d), F32))


def _moe_ffn_kernel(te_ref, tv_ref, x_ref, g_ref, wg_ref, wu_ref, wd_ref, o_ref, wg_s, wu_s, wd_s):
    i = pl.program_id(0)
    prev = te_ref[jnp.maximum(i - 1, 0)]
    changed = (i == 0) | (te_ref[i] != prev)

    @pl.when(changed)
    def _():
        wg_s[...] = wg_ref[0, 0].astype(BF16)
        wu_s[...] = wu_ref[0, 0].astype(BF16)
        wd_s[...] = wd_ref[0, 0].astype(BF16)

    @pl.when(tv_ref[i] > 0)
    def _():
        x = x_ref[...]
        ms = jnp.mean(x * x, axis=-1, keepdims=True)
        x = (x * lax.rsqrt(ms + NORM_EPS) * g_ref[...]).astype(BF16)
        gate = jnp.dot(x, wg_s[...], preferred_element_type=F32)
        up = jnp.dot(x, wu_s[...], preferred_element_type=F32)
        act = _silu(gate) * up
        o_ref[...] = jnp.dot(act.astype(BF16), wd_s[...], preferred_element_type=F32)

    @pl.when(tv_ref[i] == 0)
    def _():
        o_ref[...] = jnp.zeros_like(o_ref)


def moe_ffn(xs, g, tile_expert, tile_valid, w_gate, w_up, w_down, layer):
    np_rows, d = xs.shape
    f = w_gate.shape[-1]
    tm = MOE_TM
    n_tiles = np_rows // tm
    grid_spec = pltpu.PrefetchScalarGridSpec(
        num_scalar_prefetch=2,
        grid=(n_tiles,),
        in_specs=[
            pl.BlockSpec((tm, d), lambda i, te, tv: (i, 0)),
            pl.BlockSpec((1, d), lambda i, te, tv: (0, 0)),
            pl.BlockSpec((1, 1, d, f), lambda i, te, tv: (layer, te[i], 0, 0)),
            pl.BlockSpec((1, 1, d, f), lambda i, te, tv: (layer, te[i], 0, 0)),
            pl.BlockSpec((1, 1, f, d), lambda i, te, tv: (layer, te[i], 0, 0)),
        ],
        out_specs=pl.BlockSpec((tm, d), lambda i, te, tv: (i, 0)),
        scratch_shapes=[pltpu.VMEM((d, f), BF16), pltpu.VMEM((d, f), BF16), pltpu.VMEM((f, d), BF16)],
    )
    return pl.pallas_call(
        _moe_ffn_kernel,
        grid_spec=grid_spec,
        out_shape=jax.ShapeDtypeStruct((np_rows, d), F32),
        compiler_params=_cp(("arbitrary",)),
        name="moe_ffn",
    )(tile_expert, tile_valid, xs, g.reshape(1, d), w_gate, w_up, w_down)


def _combine_kernel(dcur_ref, dnxt_ref, h_ref, w_ref, g_ref, y_ref, o_ref, ybuf, sems, *, final_norm):
    i = pl.program_id(0)
    n = pl.num_programs(0)
    tm = h_ref.shape[0]
    slot = lax.rem(i, 2)

    def row_copy(dref, a, s):
        return pltpu.make_async_copy(y_ref.at[pl.ds(dref[0, 0, a], 1)], ybuf.at[s, pl.ds(a, 1)], sems.at[s])

    def fetch(dref, s):
        def start(a, carry):
            row_copy(dref, a, s).start()
            return carry
        lax.fori_loop(0, 2 * tm, start, 0, unroll=8)

    @pl.when(i == 0)
    def _():
        fetch(dcur_ref, 0)

    @pl.when(i + 1 < n)
    def _():
        fetch(dnxt_ref, 1 - slot)

    def wait(a, carry):
        row_copy(dcur_ref, a, slot).wait()
        return carry

    lax.fori_loop(0, 2 * tm, wait, 0, unroll=8)
    w = w_ref[...]
    x = h_ref[...] + w[:, 0:1] * ybuf[slot, 0:tm, :] + w[:, 1:2] * ybuf[slot, tm:2 * tm, :]
    if final_norm:
        ms = jnp.mean(x * x, axis=-1, keepdims=True)
        x = x * lax.rsqrt(ms + NORM_EPS) * g_ref[...]
    o_ref[...] = x


def moe_combine(h, y, dest, wts, g_final=None):
    t, d = h.shape
    tm = MOE_COMBINE_TM
    n = t // tm
    dtile = dest.reshape(n, tm, 2).transpose(0, 2, 1).reshape(n, 1, 2 * tm)
    g = jnp.ones((1, d), F32) if g_final is None else g_final.reshape(1, d)
    kern = functools.partial(_combine_kernel, final_norm=g_final is not None)
    return pl.pallas_call(
        kern,
        grid=(n,),
        in_specs=[
            pl.BlockSpec((1, 1, 2 * tm), lambda i: (i, 0, 0), memory_space=pltpu.SMEM),
            pl.BlockSpec((1, 1, 2 * tm), lambda i: (jnp.minimum(i + 1, n - 1), 0, 0), memory_space=pltpu.SMEM),
            pl.BlockSpec((tm, d), lambda i: (i, 0)),
            pl.BlockSpec((tm, 128), lambda i: (i, 0)),
            pl.BlockSpec((1, d), lambda i: (0, 0)),
            pl.BlockSpec(memory_space=pl.ANY),
        ],
        out_specs=pl.BlockSpec((tm, d), lambda i: (i, 0)),
        out_shape=jax.ShapeDtypeStruct((t, d), F32),
        scratch_shapes=[pltpu.VMEM((2, 2 * tm, d), F32), pltpu.SemaphoreType.DMA((2,))],
        compiler_params=_cp(("arbitrary",)),
        name="moe_combine",
    )(dtile, dtile, h, wts, g, y)


def hier_moe_block(h, g, wg, bg, we, be, w_gate, w_up, w_down, layer, g_final=None):
    t, d = h.shape
    tm = MOE_TM
    ids, wts, rank, counts = moe_router(h, g, wg, bg, we, be)
    counts = counts[0, :MOE_EXPERTS]
    padded = ((counts + tm - 1) // tm) * tm
    pad_end = jnp.cumsum(padded)
    pad_off = pad_end - padded
    dest = jnp.take(pad_off, ids[:, :2]) + rank[:, :2]
    np_rows = 2 * t + MOE_EXPERTS * tm
    tile_start = jnp.arange(np_rows // tm, dtype=jnp.int32) * tm
    tile_expert = jnp.minimum(jnp.sum((tile_start[:, None] >= pad_end[None, :]).astype(jnp.int32), axis=1),
                              MOE_EXPERTS - 1)
    tile_valid = (tile_start < pad_end[-1]).astype(jnp.int32)
    xs = moe_dispatch(h, dest, np_rows)
    y = moe_ffn(xs, g, tile_expert, tile_valid, w_gate, w_up, w_down, layer)
    return moe_combine(h, y, dest, wts, g_final)


SSM_D_INNER = 2048
SSM_HEADS = 32
SSM_GROUPS = 4
SSM_HPG = 8
SSM_STATE = 128
SSM_CONV = 4
SSM_CHUNK = 256
SSM_SUBCHUNK = 256
SSM_NORM_EPS = 1e-5
SSM_CONV_CH = SSM_D_INNER + 2 * SSM_GROUPS * SSM_STATE
SSM_IN_COLS = SSM_D_INNER + SSM_CONV_CH + SSM_HEADS
SSM_IN_PAD = 5376
SSM_GW = SSM_HPG * HEAD_DIM
SSD_X0 = SSM_D_INNER


def _ssd_kernel(x_ref, b_ref, c_ref, z_ref, dt_ref, cwx_ref, cwb_ref, cwc_ref, cbx_ref, cbb_ref, cbc_ref,
                dtb_ref, alog_ref, dsk_ref, nw_ref, sel_ref, e8_ref, tri_ref, o_ref, h_ref, tail_ref):
    c = pl.program_id(2)
    lc = SSM_CHUNK

    @pl.when(c == 0)
    def _():
        h_ref[...] = jnp.zeros_like(h_ref)
        tail_ref[...] = jnp.zeros_like(tail_ref)

    def conv_silu(raw_ref, w_ref, bias_ref, lo):
        raw = raw_ref[...]
        hi = lo + raw.shape[1]
        xx = jnp.concatenate([tail_ref[:, lo:hi], raw], axis=0)
        w = w_ref[...]
        acc = bias_ref[...] + w[SSM_CONV - 1:SSM_CONV, :] * raw
        for k in range(1, SSM_CONV):
            acc = acc + w[SSM_CONV - 1 - k:SSM_CONV - k, :] * xx[8 - k:8 - k + lc, :]
        tail_ref[:, lo:hi] = raw[lc - 8:lc, :]
        return _silu(acc)

    x = conv_silu(x_ref, cwx_ref, cbx_ref, 0)
    bm = conv_silu(b_ref, cwb_ref, cbb_ref, SSM_GW)
    cm = conv_silu(c_ref, cwc_ref, cbc_ref, SSM_GW + SSM_STATE)

    sel = sel_ref[0]
    e8 = e8_ref[...]
    tri = tri_ref[...]
    ls = tri.shape[0]
    dt_all = _softplus(dt_ref[...] + dtb_ref[...])
    a_all = dt_all * (-jnp.exp(alog_ref[...]))
    dt_e = _dot_xl(_dot_xl(dt_all, sel), e8)
    xdt_all = x * dt_e
    row = lax.broadcasted_iota(jnp.int32, (ls, ls), 0)
    col = lax.broadcasted_iota(jnp.int32, (ls, ls), 1)
    causal = row >= col

    for sc in range(lc // ls):
        rs = slice(sc * ls, (sc + 1) * ls)
        xs, bs, cs_m, xdt = x[rs], bm[rs], cm[rs], xdt_all[rs]
        cs = _dot_xl(_dot_xr(tri, a_all[rs]), sel)
        cst = cs.T
        cs_last = cs[ls - 1:ls, :]
        cb = _dot1(cs_m, bs, 1, 1)
        ys = []
        for hh in range(SSM_HPG):
            lmat = jnp.where(causal, jnp.exp(cs[:, hh:hh + 1] - cst[hh:hh + 1, :]), 0.0)
            ys.append(_dot1(cb * lmat, xdt[:, hh * HEAD_DIM:(hh + 1) * HEAD_DIM]))
        y = jnp.concatenate(ys, axis=1)

        h_in = h_ref[...]
        y = y + _dot1(cs_m, h_in) * _dot_xl(jnp.exp(cs), e8)
        xdec = xdt * _dot_xl(jnp.exp(cs_last - cs), e8)
        h_ref[...] = h_in * _dot_xl(jnp.exp(cs_last), e8) + _dot1(bs, xdec, 0, 0)

        y = y + xs * dsk_ref[...]
        y = y * _silu(z_ref[rs, :])
        y = y * lax.rsqrt(jnp.mean(y * y, axis=-1, keepdims=True) + SSM_NORM_EPS) * nw_ref[...]
        o_ref[rs, :] = y.astype(BF16)


def ssd_mix(zx, conv_w, conv_b, dt_bias, a_log, d_skip, norm_w, batch, seq):
    t = zx.shape[0]
    lc = SSM_CHUNK
    nc = seq // lc
    g, hp = SSM_GROUPS, SSM_HPG
    lanes = 128
    pad_row = lambda v: jnp.pad(v, (0, lanes - v.shape[0])).reshape(1, lanes)
    sel = np.zeros((g, lanes, lanes), np.float32)
    for gg in range(g):
        sel[gg, gg * hp + np.arange(hp), np.arange(hp)] = 1.0
    e8 = np.zeros((lanes, SSM_GW), np.float32)
    e8[:hp] = np.kron(np.eye(hp), np.ones((1, HEAD_DIM)))
    ls = SSM_SUBCHUNK
    tri = jnp.asarray(np.tril(np.ones((ls, ls))), BF16)
    d_exp = jnp.repeat(d_skip, HEAD_DIM).reshape(1, SSM_D_INNER)
    xb = SSD_X0 // SSM_GW
    bb = (SSD_X0 + SSM_D_INNER) // SSM_STATE
    cb = bb + g
    wb = SSM_D_INNER // SSM_STATE
    row = lambda w, f: pl.BlockSpec((lc, w), lambda b, gg, c: (b * nc + c, f(gg)))
    par = lambda r, w, f: pl.BlockSpec((r, w), lambda b, gg, c: (0, f(gg)))
    const = lambda shape: pl.BlockSpec(shape, lambda b, gg, c: (0,) * len(shape))
    cb2 = conv_b.reshape(1, -1)
    return pl.pallas_call(
        _ssd_kernel,
        grid=(batch, g, nc),
        in_specs=[
            row(SSM_GW, lambda gg: xb + gg),
            row(SSM_STATE, lambda gg: bb + gg),
            row(SSM_STATE, lambda gg: cb + gg),
            row(SSM_GW, lambda gg: gg),
            row(lanes, lambda gg: (SSD_X0 + SSM_CONV_CH) // lanes),
            par(SSM_CONV, SSM_GW, lambda gg: gg),
            par(SSM_CONV, SSM_STATE, lambda gg: wb + gg),
            par(SSM_CONV, SSM_STATE, lambda gg: wb + g + gg),
            par(1, SSM_GW, lambda gg: gg),
            par(1, SSM_STATE, lambda gg: wb + gg),
            par(1, SSM_STATE, lambda gg: wb + g + gg),
            const((1, lanes)), const((1, lanes)),
            par(1, SSM_GW, lambda gg: gg),
            par(1, SSM_GW, lambda gg: gg),
            pl.BlockSpec((1, lanes, lanes), lambda b, gg, c: (gg, 0, 0)),
            const((lanes, SSM_GW)), const((ls, ls)),
        ],
        out_specs=pl.BlockSpec((lc, SSM_GW), lambda b, gg, c: (b * nc + c, gg)),
        out_shape=jax.ShapeDtypeStruct((t, SSM_D_INNER), BF16),
        scratch_shapes=[pltpu.VMEM((SSM_STATE, SSM_GW), F32), pltpu.VMEM((8, SSM_GW + 2 * SSM_STATE), F32)],
        compiler_params=_cp(("parallel", "parallel", "arbitrary")),
        name="ssd",
    )(zx, zx, zx, zx, zx, conv_w, conv_w, conv_w, cb2, cb2, cb2, pad_row(dt_bias), pad_row(a_log),
      d_exp, norm_w.reshape(1, -1), jnp.asarray(sel, BF16), jnp.asarray(e8, BF16), tri)


def mamba2_block(h, g_norm, w_in, conv_w, conv_b, dt_bias, a_log, d_skip, norm_w, w_out, batch, seq):
    w_in_p = jnp.pad(w_in, ((0, 0), (0, SSM_IN_PAD - SSM_IN_COLS))).astype(BF16)
    zx = norm_matmul(h, g_norm, w_in_p, tm=1024, tn=768)
    y = ssd_mix(zx, conv_w, conv_b, dt_bias, a_log, d_skip, norm_w, batch, seq)
    return matmul_residual(y, w_out.astype(BF16), h, tm=1024, tn=512)


RWKV_HEADS = 8
RWKV_WIDTH = 512
RWKV_DECAY_RANK = 32
RWKV_A_RANK = 32
RWKV_GATE_RANK = 96
RWKV_GN_EPS = 64e-5
RWKV_COLS = 3 * RWKV_WIDTH + RWKV_DECAY_RANK + RWKV_A_RANK + RWKV_GATE_RANK
RWKV_COLS_PAD = 1792
RWKV_LO = 3 * RWKV_WIDTH
RWKV_LO_W = RWKV_COLS_PAD - RWKV_LO
RWKV_CHUNK = 64


def _rwkv_prep_kernel(c_ref, p_ref, mu_ref, w0_ref, wup_ref, a0_ref, aup_ref, gup_ref, kk_ref, ka_ref, rk_ref,
                      bd_ref, tri_ref,
                      rt_ref, at_ref, bt_ref, kt_ref, v_ref, bh_ref, kh_ref, pc_ref, g_ref, bonus_ref, *, seq, tm):
    i = pl.program_id(0)
    w = RWKV_WIDTH
    c = c_ref[...]
    first = (i * tm) % seq == 0
    prev_row = jnp.where(first, 0.0, p_ref[7:8, :])
    row = lax.broadcasted_iota(jnp.int32, c.shape, 0)
    shifted = jnp.where(row == 0, prev_row, pltpu.roll(c, 1, axis=0))
    z = c + (shifted - c) * mu_ref[...]
    r = z[:, 0:w]
    k = z[:, w:2 * w]
    v = z[:, 2 * w:3 * w]
    lo = z[:, RWKV_LO:RWKV_COLS_PAD]
    w_log = -_softplus(-(w0_ref[...] + _dot3(jnp.tanh(lo), wup_ref[...]))) - 0.5
    lw = -jnp.exp(w_log)
    a = _sigmoid(a0_ref[...] + _dot3(lo, aup_ref[...]))
    g_ref[...] = _dot3(_sigmoid(lo), gup_ref[...])
    bd = bd_ref[...]
    kk = k * kk_ref[...]
    kk = kk * lax.rsqrt(jnp.maximum(_dot_xl(kk * kk, bd), 1e-24))
    k = k * (1.0 + (a - 1.0) * ka_ref[...])
    bonus_ref[...] = _dot_xl(r * k * rk_ref[...], bd) * v
    v_ref[...] = v
    av = -kk
    bv = kk * a
    tri = tri_ref[...]
    lc = RWKV_CHUNK
    pcs = []
    for cc in range(tm // lc):
        sl = slice(cc * lc, (cc + 1) * lc)
        lwc = lw[sl]
        cs = _dot_xr(tri, lwc)
        cs_end = cs[lc - 1:lc, :]
        e_in = jnp.exp(cs)
        e_out = jnp.exp(-cs)
        e_end = jnp.exp(cs_end - cs)
        rt_ref[sl, :] = r[sl] * e_in
        at_ref[sl, :] = av[sl] * jnp.exp(cs - lwc)
        bt_ref[sl, :] = bv[sl] * e_out
        kt_ref[sl, :] = k[sl] * e_out
        bh_ref[sl, :] = bv[sl] * e_end
        kh_ref[sl, :] = k[sl] * e_end
        pcs.append(jnp.exp(cs_end))
    pc_ref[...] = jnp.concatenate(pcs, axis=0)


def _rwkv_chunk_kernel(rt_ref, at_ref, bt_ref, kt_ref, v_ref, bh_ref, kh_ref, pc_ref,
                       rh_ref, yh_ref, phi_ref, psi_ref):
    lc = RWKV_CHUNK
    d = HEAD_DIM
    row = lax.broadcasted_iota(jnp.int32, (lc, lc), 0)
    col = lax.broadcasted_iota(jnp.int32, (lc, lc), 1)
    strict = row > col
    incl = row >= col
    eye = (row == col).astype(F32)
    n_ch = rt_ref.shape[0] // lc
    probs = [(cc, hh) for cc in range(n_ch) for hh in range(RWKV_HEADS)]

    def load(ref):
        return [ref[cc * lc:(cc + 1) * lc, hh * d:(hh + 1) * d] for cc, hh in probs]

    def each(fn, *lists):
        return [fn(*xs) for xs in zip(*lists)]

    rt, at, bt, kt, vv, bh, kh = (load(r) for r in (rt_ref, at_ref, bt_ref, kt_ref, v_ref, bh_ref, kh_ref))
    lowp = lambda xs: [x.astype(BF16) for x in xs]
    rtb, atb, btb, ktb, vvb, bhb, khb = (lowp(x) for x in (rt, at, bt, kt, vv, bh, kh))
    nt = lambda a, b: _dg(a, b, 1, 1)
    tn = lambda a, b: _dg(a, b, 0, 0)
    arb = each(lambda a, r: jnp.concatenate([a, r], axis=0), atb, rtb)
    gb = each(nt, arb, btb)
    gk = each(nt, arb, ktb)
    lab = each(lambda x: jnp.where(strict, x[:lc], 0.0), gb)
    lak = each(lambda x: jnp.where(strict, x[:lc], 0.0), gk)
    mrb = lowp(each(lambda x: jnp.where(incl, x[lc:], 0.0), gb))
    mrk = lowp(each(lambda x: jnp.where(incl, x[lc:], 0.0), gk))
    tinv = each(lambda l: eye + l, lab)
    lp = lowp(lab)
    lp = lowp(each(_dg, lp, lp))
    for _ in range(4):
        both = each(lambda l, t: _dg(jnp.concatenate([l, t.astype(BF16)], axis=0), l), lp, tinv)
        tinv = each(lambda t, x: t + x[lc:], tinv, both)
        lp = lowp(each(lambda x: x[:lc], both))
    tinv = lowp(each(lambda t, l: t + _dg(t.astype(BF16), l), tinv, lp))
    lv = lowp(each(_dg, lowp(lak), vvb))
    uh = lowp(each(_dg, tinv, lv))
    ah = lowp(each(_dg, tinv, atb))
    rh = each(lambda r, m, a: r + _dg(m, a), rt, mrb, ah)
    yh = each(lambda m, u, mk, v: _dg(m, u) + _dg(mk, v), mrb, uh, mrk, vvb)
    phi = each(lambda b, a: tn(b, a), bhb, ah)
    psi = each(lambda b, u, k, v: tn(b, u) + tn(k, v), bhb, uh, khb, vvb)
    for cc in range(n_ch):
        sl = slice(cc * lc, (cc + 1) * lc)
        ps = slice(cc * RWKV_HEADS, (cc + 1) * RWKV_HEADS)
        pc_row = pc_ref[cc]
        rh_ref[sl, :] = jnp.concatenate(rh[ps], axis=1)
        yh_ref[sl, :] = jnp.concatenate(yh[ps], axis=1)
        phi_ref[sl, :] = jnp.concatenate(phi[ps], axis=1) + jnp.concatenate([eye] * RWKV_HEADS, axis=1) * pc_row
        psi_ref[sl, :] = jnp.concatenate(psi[ps], axis=1)


def _rwkv_scan_kernel(rh_ref, yh_ref, phi_ref, psi_ref, bonus_ref, g_ref, gw_ref, gb_ref, bd_ref, o_ref, st_ref,
                      *, n_chunks):
    lc = RWKV_CHUNK
    d = HEAD_DIM

    @pl.when(pl.program_id(1) == 0)
    def _():
        st_ref[...] = jnp.zeros_like(st_ref)

    for cc in range(n_chunks):
        sl = slice(cc * lc, (cc + 1) * lc)
        hsl = [slice(hh * d, (hh + 1) * d) for hh in range(RWKV_HEADS)]
        stb = [st_ref[hh].astype(BF16) for hh in range(RWKV_HEADS)]
        ys = [_dg(rh_ref[sl, hs].astype(BF16), st) + yh_ref[sl, hs] for hs, st in zip(hsl, stb)]
        for hh, (hs, st) in enumerate(zip(hsl, stb)):
            st_ref[hh] = _dg(phi_ref[sl, hs].astype(BF16), st) + psi_ref[sl, hs]
        y = jnp.concatenate(ys, axis=1)
        bd = bd_ref[...]
        mean = _dot_xl(y, bd) * (1.0 / d)
        yc = y - mean
        var = _dot_xl(yc * yc, bd) * (1.0 / d)
        yn = yc * lax.rsqrt(var + RWKV_GN_EPS) * gw_ref[...] + gb_ref[...]
        o_ref[sl, :] = (yn + bonus_ref[sl, :]) * g_ref[sl, :]


def rwkv7_mix(cols, mu, w0, w_up, a0, a_up, g_up, k_k, k_a, r_k, gn_w, gn_b, batch, seq, col_block=0):
    t = cols.shape[0]
    w = RWKV_WIDTH
    lc = RWKV_CHUNK
    tm = 512
    row2 = lambda x: x.reshape(1, -1)
    mu_p = jnp.pad(mu, (0, RWKV_COLS_PAD - RWKV_COLS)).reshape(1, -1)
    r0 = RWKV_DECAY_RANK
    r1 = r0 + RWKV_A_RANK
    r2 = r1 + RWKV_GATE_RANK
    wup_p = jnp.zeros((RWKV_LO_W, w), F32).at[0:r0].set(w_up)
    aup_p = jnp.zeros((RWKV_LO_W, w), F32).at[r0:r1].set(a_up)
    gup_p = jnp.zeros((RWKV_LO_W, w), F32).at[r1:r2].set(g_up)
    bd = jnp.asarray(np.kron(np.eye(RWKV_HEADS), np.ones((HEAD_DIM, HEAD_DIM))), BF16)
    tri = jnp.asarray(np.tril(np.ones((lc, lc))), BF16)
    full = lambda shape: pl.BlockSpec(shape, lambda i: (0,) * len(shape))
    tok = pl.BlockSpec((tm, w), lambda i: (i, 0))
    big = jax.ShapeDtypeStruct((t, w), F32)
    prep = functools.partial(_rwkv_prep_kernel, seq=seq, tm=tm)
    rt, at, bt, kt, v, bh, kh, pc, g, bonus = pl.pallas_call(
        prep,
        grid=(t // tm,),
        in_specs=[
            pl.BlockSpec((tm, RWKV_COLS_PAD), lambda i: (i, col_block)),
            pl.BlockSpec((8, RWKV_COLS_PAD), lambda i: (jnp.maximum(i * (tm // 8) - 1, 0), col_block)),
            full((1, RWKV_COLS_PAD)),
            full((1, w)), full((RWKV_LO_W, w)), full((1, w)), full((RWKV_LO_W, w)), full((RWKV_LO_W, w)),
            full((1, w)), full((1, w)), full((1, w)), full((w, w)), full((lc, lc)),
        ],
        out_specs=[tok, tok, tok, tok, tok, tok, tok, pl.BlockSpec((tm // lc, w), lambda i: (i, 0)), tok, tok],
        out_shape=[big, big, big, big, big, big, big, jax.ShapeDtypeStruct((t // lc, w), F32), big, big],
        compiler_params=_cp(("parallel",)),
        name="rwkv_prep",
    )(cols, cols, mu_p, row2(w0), wup_p, row2(a0), aup_p, gup_p, row2(k_k), row2(k_a), row2(r_k), bd, tri)

    n_ch = 1
    ch = pl.BlockSpec((n_ch * lc, w), lambda i: (i, 0))
    rh, yh, phi, psi = pl.pallas_call(
        _rwkv_chunk_kernel,
        grid=(t // (n_ch * lc),),
        in_specs=[ch, ch, ch, ch, ch, ch, ch, pl.BlockSpec((n_ch, 1, w), lambda i: (i, 0, 0))],
        out_specs=[ch, ch, ch, ch],
        out_shape=[big, big, big, big],
        compiler_params=_cp(("parallel",)),
        name="rwkv_chunk",
    )(rt, at, bt, kt, v, bh, kh, pc.reshape(t // lc, 1, w))

    n_chunks = 4
    ts = n_chunks * lc
    nsb = seq // ts
    blk = pl.BlockSpec((ts, w), lambda b, j: (b * nsb + j, 0))
    scan = functools.partial(_rwkv_scan_kernel, n_chunks=n_chunks)
    return pl.pallas_call(
        scan,
        grid=(batch, nsb),
        in_specs=[blk, blk, blk, blk, blk, blk,
                  pl.BlockSpec((1, w), lambda b, j: (0, 0)), pl.BlockSpec((1, w), lambda b, j: (0, 0)),
                  pl.BlockSpec((w, w), lambda b, j: (0, 0))],
        out_specs=blk,
        out_shape=big,
        scratch_shapes=[pltpu.VMEM((RWKV_HEADS, HEAD_DIM, HEAD_DIM), F32)],
        compiler_params=_cp(("parallel", "arbitrary")),
        name="rwkv_scan",
    )(rh, yh, phi, psi, bonus, g, row2(gn_w), row2(gn_b), bd)


NSA_Q_HEADS = 8
NSA_KV_HEADS = 2
NSA_GROUP = 4
NSA_WIDTH = 512
NSA_KV_WIDTH = 128
NSA_COLS = NSA_WIDTH + 6 * NSA_KV_WIDTH + 3 * NSA_Q_HEADS
CMP_BLOCK = 32
CMP_STRIDE = 16
CMP_HIDDEN = 128
SLC_BLOCK = 64
SLC_TOPN = 16
WINDOW = 512
FORCED_SCORE = 1e6
NSA_TQ = 128
NSA_CB_KC, NSA_CB_VC, NSA_CB_KS, NSA_CB_VS, NSA_CB_KW, NSA_CB_VW, NSA_CB_GATE = 4, 5, 6, 7, 8, 9, 10


def _rope_tables(pos, reps):
    half = ROPE_DIM // 2
    inv = jnp.power(ROPE_THETA, -jnp.arange(half, dtype=F32) * (2.0 / ROPE_DIM))
    ang = pos.astype(F32)[:, None] * inv[None, :]
    cos = jnp.cos(ang)
    sin = jnp.sin(ang)
    p = pos.shape[0]
    ones = jnp.ones((p, HEAD_DIM - ROPE_DIM), F32)
    zeros = jnp.zeros((p, HEAD_DIM - half), F32)
    cos_t = jnp.concatenate([cos, cos, ones], axis=1)
    sin_a = jnp.concatenate([-sin, zeros], axis=1)
    sin_b = jnp.concatenate([jnp.zeros((p, half), F32), sin, jnp.zeros((p, HEAD_DIM - ROPE_DIM), F32)], axis=1)
    tile = lambda x: jnp.tile(x, (1, reps))
    return tile(cos_t), tile(sin_a), tile(sin_b)


def _rope(x, cos_t, sin_a, sin_b):
    n = x.shape[1]
    half = ROPE_DIM // 2
    return x * cos_t + pltpu.roll(x, n - half, axis=1) * sin_a + pltpu.roll(x, half, axis=1) * sin_b


def _compress_kernel(xk_ref, xv_ref, pek_ref, w1k_ref, b1k_ref, w2k_ref, pev_ref, w1v_ref, b1v_ref, w2v_ref,
                     cos_ref, sa_ref, sb_ref, ko_ref, vo_ref):
    n_half = ko_ref.shape[1]
    half = CMP_BLOCK // 2

    def mlp(x_ref, pe_ref, w1_ref, b1_ref, w2_ref):
        xs = [x_ref[pl.ds(l, n_half, stride=CMP_STRIDE), :] for l in range(half)]
        top = [_dot3(x + pe_ref[l:l + 1, :], w1_ref[l]) for l, x in enumerate(xs)]
        bot = [_dot3(x + pe_ref[half + l:half + l + 1, :], w1_ref[half + l]) for l, x in enumerate(xs)]
        hid = sum(top[1:], top[0]) + pltpu.roll(sum(bot[1:], bot[0]), n_half - 1, axis=0) + b1_ref[...]
        return _dot3(_silu(hid), w2_ref[...])

    k = mlp(xk_ref, pek_ref, w1k_ref, b1k_ref, w2k_ref)
    ko_ref[0] = _rope(k, cos_ref[...], sa_ref[...], sb_ref[...])
    vo_ref[0] = mlp(xv_ref, pev_ref, w1v_ref, b1v_ref, w2v_ref)


def nsa_compress(cols, pe_k, w1_k, b1_k, w2_k, pe_v, w1_v, b1_v, w2_v, batch, seq):
    hk, d = NSA_KV_HEADS, HEAD_DIM
    n_half = seq // CMP_STRIDE
    cmp_end = jnp.arange(n_half) * CMP_STRIDE + (CMP_BLOCK - 1)
    cos_t, sin_a, sin_b = _rope_tables(cmp_end, hk)
    eye = jnp.eye(hk, dtype=F32)
    pair = lambda w: jnp.kron(eye, w)
    w1_pair = lambda w: jax.vmap(pair)(w.reshape(CMP_BLOCK, d, CMP_HIDDEN))
    lanes2 = lambda v: jnp.tile(v, (1, hk))
    full = lambda shape: pl.BlockSpec(shape, lambda b: (0,) * len(shape))
    xspec = lambda cb: pl.BlockSpec((seq, hk * d), lambda b: (b, cb))
    ospec = pl.BlockSpec((1, n_half, hk * d), lambda b: (b, 0, 0))
    wspecs = [full((CMP_BLOCK, hk * d)), full((CMP_BLOCK, hk * d, hk * CMP_HIDDEN)), full((1, hk * CMP_HIDDEN)),
              full((hk * CMP_HIDDEN, hk * d))]
    oshape = jax.ShapeDtypeStruct((batch, n_half, hk * d), F32)
    return pl.pallas_call(
        _compress_kernel,
        grid=(batch,),
        in_specs=[xspec(NSA_CB_KC), xspec(NSA_CB_VC)] + wspecs + wspecs + [full((n_half, hk * d))] * 3,
        out_specs=[ospec, ospec],
        out_shape=[oshape, oshape],
        compiler_params=_cp(("parallel",)),
        name="nsa_compress",
    )(cols, cols, lanes2(pe_k), w1_pair(w1_k), lanes2(b1_k.reshape(1, -1)), pair(w2_k),
      lanes2(pe_v), w1_pair(w1_v), lanes2(b1_v.reshape(1, -1)), pair(w2_v), cos_t, sin_a, sin_b)


def _rope_t(x, cos_t, sin_a, sin_b):
    half = ROPE_DIM // 2
    up = jnp.concatenate([x[half:], x[:half]], axis=0)
    down = jnp.concatenate([x[-half:], x[:-half]], axis=0)
    return x * cos_t + up * sin_a + down * sin_b


def _nsa_attn_kernel(q_ref, ks_ref, vs_ref, kw_ref, vw_ref, gate_ref, kc_ref, vc_ref,
                     cosk_ref, sak_ref, sbk_ref, cosq_ref, saq_ref, sbq_ref, ovl_ref, get_ref,
                     o_ref, ks_s, vs_s, kw_s, vw_s, mem_s, *, seq):
    i = pl.program_id(1)
    tq = NSA_TQ
    d = HEAD_DIM
    g = NSA_GROUP
    cols = g * tq
    n_kv = NSA_KV_HEADS
    n_tiles = seq // tq

    @pl.when(i == 0)
    def _():
        ck, sa, sb = cosk_ref[...], sak_ref[...], sbk_ref[...]
        ks = _rope(ks_ref[...], ck, sa, sb)
        kw = _rope(kw_ref[...], ck, sa, sb)
        for hk in range(n_kv):
            hs = slice(hk * d, (hk + 1) * d)
            ks_s[hk] = ks[:, hs].astype(BF16)
            kw_s[hk] = kw[:, hs].astype(BF16)
        for j in range(n_tiles):
            rs = slice(j * tq, (j + 1) * tq)
            vs_s[j] = vs_ref[rs, :].T.astype(BF16)
            vw_s[j] = vw_ref[rs, :].T.astype(BF16)

    qt = _rope_t(q_ref[...].T, cosq_ref[...], saq_ref[...], sbq_ref[...]) * (HEAD_DIM ** -0.5)
    gexp = _dot_xr(get_ref[...], _sigmoid(gate_ref[...].T))

    pos_q = i * tq + (lax.broadcasted_iota(jnp.int32, (1, cols), 1) & (tq - 1))
    sub = lax.broadcasted_iota(jnp.int32, (tq, cols), 0)
    blk = lax.broadcasted_iota(jnp.int32, (32, tq), 0)
    cur = (i * tq + lax.broadcasted_iota(jnp.int32, (32, tq), 1)) // SLC_BLOCK
    sel_valid = blk <= cur
    sel_forced = (blk == 0) | (blk == cur) | (blk == cur - 1)

    q4 = [jnp.concatenate([qt[(hk * g + gg) * d:(hk * g + gg + 1) * d, :] for gg in range(g)], axis=1)
          for hk in range(n_kv)]
    q4b = [x.astype(BF16) for x in q4]

    cend = sub * CMP_STRIDE + (CMP_BLOCK - 1)
    cmask = (cend <= pos_q) & (sub < seq // CMP_STRIDE - 1)
    o_cmp = []
    for hk in range(n_kv):
        s = _dot3(kc_ref[0, :, hk * d:(hk + 1) * d], q4[hk])
        sm = jnp.where(cmask, s, MASK_NEG)
        e = jnp.exp(sm - jnp.max(sm, axis=0, keepdims=True))
        p = e / jnp.sum(e, axis=0, keepdims=True) * cmask.astype(F32)
        o_cmp.append(_dot1(vc_ref[0, :, hk * d:(hk + 1) * d], p, 0, 0))
        psum = p[:, 0:tq] + p[:, tq:2 * tq] + p[:, 2 * tq:3 * tq] + p[:, 3 * tq:4 * tq]
        imp = _dot_xr(ovl_ref[...], psum)
        imp = jnp.where(sel_forced, FORCED_SCORE, jnp.where(sel_valid, imp, -FORCED_SCORE))
        members = []
        for jj in range(32):
            rj = imp[jj:jj + 1, :]
            beats = (imp > rj) | ((imp == rj) & (blk < jj))
            cnt = jnp.sum(beats.astype(F32), axis=0, keepdims=True)
            members.append((cnt < SLC_TOPN).astype(F32))
        member = jnp.concatenate(members + [jnp.zeros((8, tq), F32)], axis=0)
        mem_s[hk] = jnp.concatenate([member] * g, axis=1)

    def flash(k_s, vt_s, j_lo, mask_fn):
        def scores(j):
            jl = jnp.minimum(j, n_tiles - 1)
            ks = [k_s[hk, pl.ds(jl * tq, tq), :] for hk in range(n_kv)]
            return [_dg(ks[hk], q4b[hk]) for hk in range(n_kv)], vt_s[jl]

        def update(j, ss, vt, carry):
            rel = (pos_q - j * tq) - sub
            new = []
            ps = []
            for hk in range(n_kv):
                m_run, l_run, acc = carry[hk]
                s = jnp.where(mask_fn(hk, j, rel), ss[hk], MASK_NEG)
                m_new = jnp.maximum(m_run, jnp.max(s, axis=0, keepdims=True))
                alpha = jnp.exp(m_run - m_new)
                p = jnp.exp(s - m_new)
                l_new = alpha * l_run + jnp.sum(p, axis=0, keepdims=True)
                ps.append(p.astype(BF16))
                new.append((m_new, l_new, alpha * acc))
            pv = [_dg(vt[hk * d:(hk + 1) * d, :], ps[hk]) for hk in range(n_kv)]
            return tuple((m, l, a + o) for (m, l, a), o in zip(new, pv))

        def body(pair, carry):
            ja = 2 * pair
            ssa, vta = scores(ja)
            ssb, vtb = scores(ja + 1)
            return update(ja + 1, ssb, vtb, update(ja, ssa, vta, carry))

        one = (jnp.full((1, cols), MASK_NEG, F32), jnp.zeros((1, cols), F32), jnp.zeros((d, cols), F32))
        fin = lax.fori_loop(j_lo // 2, i // 2 + 1, body, (one,) * n_kv)
        return [acc / l_fin for _, l_fin, acc in fin]

    first_block = sub < SLC_BLOCK

    def slc_mask(hk, j, rel):
        m0 = mem_s[hk, pl.ds(2 * j, 1), :]
        m1 = mem_s[hk, pl.ds(2 * j + 1, 1), :]
        return (jnp.where(first_block, m0, m1) > 0.5) & (rel >= 0)

    def win_mask(hk, j, rel):
        return (rel >= 0) & (rel < WINDOW)

    o_slc = flash(ks_s, vs_s, 0, slc_mask)
    o_win = flash(kw_s, vw_s, jnp.maximum(i - WINDOW // tq, 0), win_mask)

    outs = []
    for hk in range(n_kv):
        for gg in range(g):
            h = hk * g + gg
            cs = slice(gg * tq, (gg + 1) * tq)
            ge = lambda br: gexp[br * NSA_WIDTH + h * d: br * NSA_WIDTH + (h + 1) * d, :]
            outs.append(ge(0) * o_cmp[hk][:, cs] + ge(1) * o_slc[hk][:, cs] + ge(2) * o_win[hk][:, cs])
    o_ref[...] = jnp.concatenate(outs, axis=0).T


def nsa_mix(cols, pe_k, w1_k, b1_k, w2_k, pe_v, w1_v, b1_v, w2_v, batch, seq):
    t = cols.shape[0]
    tq = NSA_TQ
    nq = seq // tq
    d = HEAD_DIM
    k_cmp, v_cmp = nsa_compress(cols, pe_k, w1_k, b1_k, w2_k, pe_v, w1_v, b1_v, w2_v, batch, seq)
    pos = jnp.arange(seq)
    cos_k, sa_k, sb_k = _rope_tables(pos, 2)
    cos_q, sa_q, sb_q = (x.T for x in _rope_tables(pos, NSA_Q_HEADS))
    n_slc = seq // SLC_BLOCK
    n_half = seq // CMP_STRIDE
    blk_tok = np.arange(n_half - 1)[:, None] * CMP_STRIDE + np.arange(CMP_BLOCK)[None, :]
    overlap = np.any((blk_tok[:, :, None] // SLC_BLOCK) == np.arange(n_slc)[None, None, :], axis=1)
    ovl_t = np.zeros((32, 128), np.float32)
    ovl_t[:n_slc, :n_half - 1] = overlap.T
    gate_e = np.zeros((3 * NSA_WIDTH, 128), np.float32)
    for h in range(NSA_Q_HEADS):
        for br in range(3):
            gate_e[br * NSA_WIDTH + h * d: br * NSA_WIDTH + (h + 1) * d, h * 3 + br] = 1.0
    kvspec = lambda cb: pl.BlockSpec((seq, 128), lambda b, i: (b, cb))
    full = lambda shape: pl.BlockSpec(shape, lambda b, i: (0,) * len(shape))
    cspec = pl.BlockSpec((1, n_half, NSA_KV_WIDTH), lambda b, i: (b, 0, 0))
    kern = functools.partial(_nsa_attn_kernel, seq=seq)
    return pl.pallas_call(
        kern,
        grid=(batch, nq),
        in_specs=[
            pl.BlockSpec((tq, NSA_WIDTH), lambda b, i: (b * nq + i, 0)),
            kvspec(NSA_CB_KS), kvspec(NSA_CB_VS), kvspec(NSA_CB_KW), kvspec(NSA_CB_VW),
            pl.BlockSpec((tq, 128), lambda b, i: (b * nq + i, NSA_CB_GATE)),
            cspec, cspec,
            full((seq, 128)), full((seq, 128)), full((seq, 128)),
            pl.BlockSpec((NSA_WIDTH, tq), lambda b, i: (0, i)),
            pl.BlockSpec((NSA_WIDTH, tq), lambda b, i: (0, i)),
            pl.BlockSpec((NSA_WIDTH, tq), lambda b, i: (0, i)),
            full((32, 128)), full((3 * NSA_WIDTH, 128)),
        ],
        out_specs=pl.BlockSpec((tq, NSA_WIDTH), lambda b, i: (b * nq + i, 0)),
        out_shape=jax.ShapeDtypeStruct((t, NSA_WIDTH), F32),
        scratch_shapes=[
            pltpu.VMEM((NSA_KV_HEADS, seq, d), BF16),
            pltpu.VMEM((nq, NSA_KV_WIDTH, tq), BF16),
            pltpu.VMEM((NSA_KV_HEADS, seq, d), BF16),
            pltpu.VMEM((nq, NSA_KV_WIDTH, tq), BF16),
            pltpu.VMEM((NSA_KV_HEADS, 40, NSA_GROUP * tq), F32),
        ],
        compiler_params=_cp(("parallel", "arbitrary")),
        name="nsa_attn",
    )(cols, cols, cols, cols, cols, cols, k_cmp, v_cmp, cos_k, sa_k, sb_k, cos_q, sa_q, sb_q,
      jnp.asarray(ovl_t, BF16), jnp.asarray(gate_e, BF16))


HYB_NSA_PAD = 1792
HYB_COLS_PAD = HYB_NSA_PAD + RWKV_COLS_PAD


def _mm2_res_kernel(a1_ref, a2_ref, w1_ref, w2_ref, r_ref, o_ref):
    acc = jnp.dot(a1_ref[...].astype(BF16), w1_ref[...], preferred_element_type=F32)
    acc = acc + jnp.dot(a2_ref[...].astype(BF16), w2_ref[...], preferred_element_type=F32)
    o_ref[...] = r_ref[...] + acc


def matmul2_residual(a1, a2, w_bf16, res, tm, tn):
    t, k1 = a1.shape
    k2 = a2.shape[1]
    n = w_bf16.shape[1]
    return pl.pallas_call(
        _mm2_res_kernel,
        grid=(t // tm, n // tn),
        in_specs=[
            pl.BlockSpec((tm, k1), lambda i, j: (i, 0)),
            pl.BlockSpec((tm, k2), lambda i, j: (i, 0)),
            pl.BlockSpec((k1, tn), lambda i, j: (0, j)),
            pl.BlockSpec((k2, tn), lambda i, j: (k1 // k2, j)),
            pl.BlockSpec((tm, tn), lambda i, j: (i, j)),
        ],
        out_specs=pl.BlockSpec((tm, tn), lambda i, j: (i, j)),
        out_shape=jax.ShapeDtypeStruct((t, n), F32),
        compiler_params=_cp(("parallel", "arbitrary")),
        name="matmul2_residual",
    )(a1, a2, w_bf16, w_bf16, res)


def hybrid_block(h, g_norm, w_in, w_out, rwkv_params, nsa_params, batch, seq):
    w_rwkv = w_in[:, :RWKV_COLS]
    w_nsa = w_in[:, RWKV_COLS:]
    w_p = jnp.concatenate([
        jnp.pad(w_nsa, ((0, 0), (0, HYB_NSA_PAD - NSA_COLS))),
        jnp.pad(w_rwkv, ((0, 0), (0, RWKV_COLS_PAD - RWKV_COLS))),
    ], axis=1).astype(BF16)
    cols = norm_matmul(h, g_norm, w_p, tm=1024, tn=896)
    y_rwkv = rwkv7_mix(cols, *rwkv_params, batch=batch, seq=seq, col_block=1)
    y_nsa = nsa_mix(cols, *nsa_params, batch=batch, seq=seq)
    return matmul2_residual(y_rwkv, y_nsa, w_out.astype(BF16), h, tm=1024, tn=512)


def kernel(x, norm_mix, norm_ffn, norm_final, hyb_w_in, hyb_w_out, rwkv_mu, rwkv_w0, rwkv_w_up, rwkv_a0, rwkv_a_up, rwkv_g_up, rwkv_k_k, rwkv_k_a, rwkv_r_k, rwkv_gn_w, rwkv_gn_b, nsa_pe_k, nsa_w1_k, nsa_b1_k, nsa_w2_k, nsa_pe_v, nsa_w1_v, nsa_b1_v, nsa_w2_v, ssm_w_in, ssm_conv_w, ssm_conv_b, ssm_dt_bias, ssm_a_log, ssm_d, ssm_norm_w, ssm_w_out, moe_wg, moe_bg, moe_we, moe_be, moe_w_gate, moe_w_up, moe_w_down):
    batch, seq, d = x.shape
    depth = norm_mix.shape[0]
    h = x.reshape(batch * seq, d)
    for layer in range(depth):
        i = layer // 2
        if layer % 2 == 0:
            rwkv_params = (rwkv_mu[i], rwkv_w0[i], rwkv_w_up[i], rwkv_a0[i], rwkv_a_up[i], rwkv_g_up[i],
                           rwkv_k_k[i], rwkv_k_a[i], rwkv_r_k[i].reshape(-1), rwkv_gn_w[i], rwkv_gn_b[i])
            nsa_params = (nsa_pe_k[i], nsa_w1_k[i], nsa_b1_k[i], nsa_w2_k[i],
                          nsa_pe_v[i], nsa_w1_v[i], nsa_b1_v[i], nsa_w2_v[i])
            h = hybrid_block(h, norm_mix[layer], hyb_w_in[i], hyb_w_out[i], rwkv_params, nsa_params, batch, seq)
        else:
            h = mamba2_block(h, norm_mix[layer], ssm_w_in[i], ssm_conv_w[i], ssm_conv_b[i], ssm_dt_bias[i],
                             ssm_a_log[i], ssm_d[i], ssm_norm_w[i], ssm_w_out[i], batch, seq)
        h = hier_moe_block(h, norm_ffn[layer], moe_wg[layer], moe_bg[layer], moe_we[layer], moe_be[layer],
                           moe_w_gate, moe_w_up, moe_w_down, layer,
                           g_final=norm_final if layer == depth - 1 else None)
    return h.reshape(batch, seq, d)
```

```python
import functools
import math

import numpy as np
import jax
import jax.numpy as jnp
from jax import lax
from jax.experimental import pallas as pl
from jax.experimental.pallas import tpu as pltpu

F32 = jnp.float32
BF16 = jnp.bfloat16

D_MODEL = 1024
NORM_EPS = 1e-6
HEAD_DIM = 64
ROPE_DIM = 16
ROPE_THETA = 500000.0
MASK_NEG = -1e30

VMEM_LIMIT = 48 * 1024 * 1024


def _cp(sem, vmem=VMEM_LIMIT):
    return pltpu.CompilerParams(dimension_semantics=sem, vmem_limit_bytes=vmem)


def _dg(a, b, ca=1, cb=0):
    return lax.dot_general(a, b, (((ca,), (cb,)), ((), ())), preferred_element_type=F32)


def _split2(x):
    hi = x.astype(BF16)
    lo = (x - hi.astype(F32)).astype(BF16)
    return hi, lo


def _split3(x):
    hi = x.astype(BF16)
    r = x - hi.astype(F32)
    mid = r.astype(BF16)
    lo = (r - mid.astype(F32)).astype(BF16)
    return hi, mid, lo


def _dot1(a, b, ca=1, cb=0):
    return _dg(a.astype(BF16), b.astype(BF16), ca, cb)


def _dot3(a, b, ca=1, cb=0):
    ah, al = _split2(a)
    bh, bl = _split2(b)
    return _dg(ah, bh, ca, cb) + (_dg(ah, bl, ca, cb) + _dg(al, bh, ca, cb))


def _dot_xl(a, b01, ca=1, cb=0):
    h, m, l = _split3(a)
    return _dg(h, b01, ca, cb) + (_dg(m, b01, ca, cb) + _dg(l, b01, ca, cb))


def _dot_xr(a01, b, ca=1, cb=0):
    h, m, l = _split3(b)
    return _dg(a01, h, ca, cb) + (_dg(a01, m, ca, cb) + _dg(a01, l, ca, cb))


def _sigmoid(x):
    return 1.0 / (1.0 + jnp.exp(-x))


def _silu(x):
    return x * _sigmoid(x)


def _softplus(x):
    return jnp.maximum(x, 0.0) + jnp.log(1.0 + jnp.exp(-jnp.abs(x)))


def _norm_mm_kernel(x_ref, g_ref, w_ref, o_ref, xn_ref):
    @pl.when(pl.program_id(1) == 0)
    def _():
        x = x_ref[...]
        ms = jnp.mean(x * x, axis=-1, keepdims=True)
        xn_ref[...] = (x * lax.rsqrt(ms + NORM_EPS) * g_ref[...]).astype(BF16)

    o_ref[...] = jnp.dot(xn_ref[...], w_ref[...], preferred_element_type=F32)


def norm_matmul(x, g, w_bf16, tm, tn):
    t, d = x.shape
    n = w_bf16.shape[1]
    return pl.pallas_call(
        _norm_mm_kernel,
        grid=(t // tm, n // tn),
        in_specs=[
            pl.BlockSpec((tm, d), lambda i, j: (i, 0)),
            pl.BlockSpec((1, d), lambda i, j: (0, 0)),
            pl.BlockSpec((d, tn), lambda i, j: (0, j)),
        ],
        out_specs=pl.BlockSpec((tm, tn), lambda i, j: (i, j)),
        out_shape=jax.ShapeDtypeStruct((t, n), F32),
        scratch_shapes=[pltpu.VMEM((tm, d), BF16)],
        compiler_params=_cp(("parallel", "arbitrary")),
        name="norm_matmul",
    )(x, g.reshape(1, d), w_bf16)


def _mm_res_kernel(a_ref, w_ref, r_ref, o_ref):
    o_ref[...] = r_ref[...] + jnp.dot(a_ref[...].astype(BF16), w_ref[...], preferred_element_type=F32)


def matmul_residual(a, w_bf16, res, tm, tn):
    t, k = a.shape
    n = w_bf16.shape[1]
    return pl.pallas_call(
        _mm_res_kernel,
        grid=(t // tm, n // tn),
        in_specs=[
            pl.BlockSpec((tm, k), lambda i, j: (i, 0)),
            pl.BlockSpec((k, tn), lambda i, j: (0, j)),
            pl.BlockSpec((tm, tn), lambda i, j: (i, j)),
        ],
        out_specs=pl.BlockSpec((tm, tn), lambda i, j: (i, j)),
        out_shape=jax.ShapeDtypeStruct((t, n), F32),
        compiler_params=_cp(("parallel", "arbitrary")),
        name="matmul_residual",
    )(a, w_bf16, res)


MOE_GROUPS = 4
MOE_EPG = 8
MOE_EXPERTS = 32
MOE_HIDDEN = 512
MOE_TM = 256
MOE_ROUTE_TM = 512
MOE_DISPATCH_TM = 256
MOE_COMBINE_TM = 128


def _router_kernel(x_ref, g_ref, w_ref, b_ref, tri_ref, ids_ref, wts_ref, rank_ref, cnt_ref, run_ref):
    @pl.when(pl.program_id(0) == 0)
    def _():
        run_ref[...] = jnp.zeros_like(run_ref)

    x = x_ref[...]
    ms = jnp.mean(x * x, axis=-1, keepdims=True)
    u = x * lax.rsqrt(ms + NORM_EPS) * g_ref[...]
    lg = _dot3(u, w_ref[...]) + b_ref[...]
    tm = x.shape[0]
    lane = lax.broadcasted_iota(jnp.int32, (tm, 128), 1)
    gl = lg[:, :128]
    el = lg[:, 128:]
    gmask = lane < MOE_GROUPS
    gmax = jnp.max(jnp.where(gmask, gl, -jnp.inf), axis=-1, keepdims=True)
    ge = jnp.where(gmask, jnp.exp(gl - gmax), 0.0)
    pg = ge / jnp.sum(ge, axis=-1, keepdims=True)
    g_val = jnp.max(pg, axis=-1, keepdims=True)
    g_idx = jnp.min(jnp.where(gmask & (pg == g_val), lane, 128), axis=-1, keepdims=True)
    emask = (lane >= g_idx * MOE_EPG) & (lane < (g_idx + 1) * MOE_EPG)
    emax = jnp.max(jnp.where(emask, el, -jnp.inf), axis=-1, keepdims=True)
    ee = jnp.where(emask, jnp.exp(el - emax), 0.0)
    pe = ee / jnp.sum(ee, axis=-1, keepdims=True)
    v1 = jnp.max(jnp.where(emask, pe, -1.0), axis=-1, keepdims=True)
    i1 = jnp.min(jnp.where(emask & (pe == v1), lane, 128), axis=-1, keepdims=True)
    m2 = emask & (lane != i1)
    v2 = jnp.max(jnp.where(m2, pe, -1.0), axis=-1, keepdims=True)
    i2 = jnp.min(jnp.where(m2 & (pe == v2), lane, 128), axis=-1, keepdims=True)
    den = v1 + v2
    w1 = v1 / den * g_val
    w2 = v2 / den * g_val
    ids_ref[...] = jnp.where(lane == 0, i1, jnp.where(lane == 1, i2, 0))
    wts_ref[...] = jnp.where(lane == 0, w1, jnp.where(lane == 1, w2, 0.0))
    oh1 = (lane == i1).astype(F32)
    oh2 = (lane == i2).astype(F32)
    oh = oh1 + oh2
    base = run_ref[...] + _dg(tri_ref[...], oh.astype(BF16))
    r1 = jnp.sum(oh1 * base, axis=-1, keepdims=True)
    r2 = jnp.sum(oh2 * base, axis=-1, keepdims=True)
    rank_ref[...] = jnp.where(lane == 0, r1, jnp.where(lane == 1, r2, 0.0)).astype(jnp.int32)
    total = run_ref[...] + jnp.sum(oh, axis=0, keepdims=True)
    run_ref[...] = total
    cnt_ref[...] = total.astype(jnp.int32)


def moe_router(h, g, wg, bg, we, be):
    t, d = h.shape
    tm = MOE_ROUTE_TM
    w = jnp.zeros((d, 256), F32).at[:, :MOE_GROUPS].set(wg).at[:, 128:128 + MOE_EXPERTS].set(we)
    b = jnp.zeros((1, 256), F32).at[0, :MOE_GROUPS].set(bg).at[0, 128:128 + MOE_EXPERTS].set(be)
    tri = jnp.asarray(np.tril(np.ones((tm, tm)), -1), BF16)
    tok = pl.BlockSpec((tm, 128), lambda i: (i, 0))
    return pl.pallas_call(
        _router_kernel,
        grid=(t // tm,),
        in_specs=[
            pl.BlockSpec((tm, d), lambda i: (i, 0)),
            pl.BlockSpec((1, d), lambda i: (0, 0)),
            pl.BlockSpec((d, 256), lambda i: (0, 0)),
            pl.BlockSpec((1, 256), lambda i: (0, 0)),
            pl.BlockSpec((tm, tm), lambda i: (0, 0)),
        ],
        out_specs=[tok, tok, tok, pl.BlockSpec((1, 128), lambda i: (0, 0))],
        out_shape=[
            jax.ShapeDtypeStruct((t, 128), jnp.int32),
            jax.ShapeDtypeStruct((t, 128), F32),
            jax.ShapeDtypeStruct((t, 128), jnp.int32),
            jax.ShapeDtypeStruct((1, 128), jnp.int32),
        ],
        scratch_shapes=[pltpu.VMEM((1, 128), F32)],
        compiler_params=_cp(("arbitrary",)),
        name="moe_router",
    )(h, g.reshape(1, d), w, b, tri)


def _dispatch_kernel(dest_ref, pend_ref, h_ref, xs_ref, zbuf, sem, zsem):
    n_tok = h_ref.shape[0]
    tile = zbuf.shape[0]

    @pl.when(pl.program_id(0) == 0)
    def _():
        zbuf[...] = jnp.zeros_like(zbuf)

        def zero_copy(row0):
            return pltpu.make_async_copy(zbuf, xs_ref.at[pl.ds(pl.multiple_of(row0, tile), tile)], zsem)

        def has_rows(e):
            return pend_ref[e] > (pend_ref[e - 1] if e > 0 else 0)

        for e in range(MOE_EXPERTS):
            @pl.when(has_rows(e))
            def _(e=e):
                zero_copy(pend_ref[e] - tile).start()

        used = pend_ref[MOE_EXPERTS - 1] // tile
        n_tiles = xs_ref.shape[0] // tile

        def start_unused(k, carry):
            zero_copy(k * tile).start()
            return carry

        def wait_unused(k, carry):
            zero_copy(k * tile).wait()
            return carry

        lax.fori_loop(used, n_tiles, start_unused, 0)
        for e in range(MOE_EXPERTS):
            @pl.when(has_rows(e))
            def _(e=e):
                zero_copy(pend_ref[e] - tile).wait()
        lax.fori_loop(used, n_tiles, wait_unused, 0)

    def row_copy(r, a):
        return pltpu.make_async_copy(h_ref.at[pl.ds(r, 1)], xs_ref.at[pl.ds(dest_ref[0, 0, a], 1)], sem)

    def start(r, carry):
        row_copy(r, 2 * r).start()
        row_copy(r, 2 * r + 1).start()
        return carry

    def wait(r, carry):
        row_copy(r, 2 * r).wait()
        row_copy(r, 2 * r + 1).wait()
        return carry

    lax.fori_loop(0, n_tok, start, 0, unroll=8)
    lax.fori_loop(0, n_tok, wait, 0, unroll=8)


def moe_dispatch(h, dest, pad_end, np_rows):
    t, d = h.shape
    tm = MOE_DISPATCH_TM
    return pl.pallas_call(
        _dispatch_kernel,
        grid=(t // tm,),
        in_specs=[
            pl.BlockSpec((1, 1, 2 * tm), lambda i: (i, 0, 0), memory_space=pltpu.SMEM),
            pl.BlockSpec(memory_space=pltpu.SMEM),
            pl.BlockSpec((tm, d), lambda i: (i, 0)),
        ],
        out_specs=pl.BlockSpec(memory_space=pl.ANY),
        out_shape=jax.ShapeDtypeStruct((np_rows, d), F32),
        scratch_shapes=[pltpu.VMEM((MOE_TM, d), F32), pltpu.SemaphoreType.DMA(()), pltpu.SemaphoreType.DMA(())],
        compiler_params=_cp(("arbitrary",)),
        name="moe_dispatch",
    )(dest.reshape(t // tm, 1, 2 * tm), pad_end, h)


def _moe_ffn_kernel(te_ref, tv_ref, x_ref, g_ref, wg_ref, wu_ref, wd_ref, o_ref, wg_s, wu_s, wd_s):
    i = pl.program_id(0)
    prev = te_ref[jnp.maximum(i - 1, 0)]
    changed = (i == 0) | (te_ref[i] != prev)

    @pl.when(changed)
    def _():
        wg_s[...] = wg_ref[0, 0].astype(BF16)
        wu_s[...] = wu_ref[0, 0].astype(BF16)
        wd_s[...] = wd_ref[0, 0].astype(BF16)

    @pl.when(tv_ref[i] > 0)
    def _():
        x = x_ref[...]
        ms = jnp.mean(x * x, axis=-1, keepdims=True)
        x = (x * lax.rsqrt(ms + NORM_EPS) * g_ref[...]).astype(BF16)
        gate = jnp.dot(x, wg_s[...], preferred_element_type=F32)
        up = jnp.dot(x, wu_s[...], preferred_element_type=F32)
        act = _silu(gate) * up
        o_ref[...] = jnp.dot(act.astype(BF16), wd_s[...], preferred_element_type=F32)

    @pl.when(tv_ref[i] == 0)
    def _():
        o_ref[...] = jnp.zeros_like(o_ref)


def moe_ffn(xs, g, tile_expert, tile_valid, w_gate, w_up, w_down, layer):
    np_rows, d = xs.shape
    f = w_gate.shape[-1]
    tm = MOE_TM
    n_tiles = np_rows // tm
    grid_spec = pltpu.PrefetchScalarGridSpec(
        num_scalar_prefetch=2,
        grid=(n_tiles,),
        in_specs=[
            pl.BlockSpec((tm, d), lambda i, te, tv: (i, 0)),
            pl.BlockSpec((1, d), lambda i, te, tv: (0, 0)),
            pl.BlockSpec((1, 1, d, f), lambda i, te, tv: (layer, te[i], 0, 0)),
            pl.BlockSpec((1, 1, d, f), lambda i, te, tv: (layer, te[i], 0, 0)),
            pl.BlockSpec((1, 1, f, d), lambda i, te, tv: (layer, te[i], 0, 0)),
        ],
        out_specs=pl.BlockSpec((tm, d), lambda i, te, tv: (i, 0)),
        scratch_shapes=[pltpu.VMEM((d, f), BF16), pltpu.VMEM((d, f), BF16), pltpu.VMEM((f, d), BF16)],
    )
    return pl.pallas_call(
        _moe_ffn_kernel,
        grid_spec=grid_spec,
        out_shape=jax.ShapeDtypeStruct((np_rows, d), F32),
        compiler_params=_cp(("arbitrary",)),
        name="moe_ffn",
    )(tile_expert, tile_valid, xs, g.reshape(1, d), w_gate, w_up, w_down)


def _combine_kernel(dcur_ref, dnxt_ref, h_ref, w_ref, g_ref, y_ref, o_ref, ybuf, sems, *, final_norm):
    i = pl.program_id(0)
    n = pl.num_programs(0)
    tm = h_ref.shape[0]
    slot = lax.rem(i, 2)

    def row_copy(dref, a, s):
        return pltpu.make_async_copy(y_ref.at[pl.ds(dref[0, 0, a], 1)], ybuf.at[s, pl.ds(a, 1)], sems.at[s])

    def fetch(dref, s):
        def start(a, carry):
            row_copy(dref, a, s).start()
            return carry
        lax.fori_loop(0, 2 * tm, start, 0, unroll=8)

    @pl.when(i == 0)
    def _():
        fetch(dcur_ref, 0)

    @pl.when(i + 1 < n)
    def _():
        fetch(dnxt_ref, 1 - slot)

    def wait(a, carry):
        row_copy(dcur_ref, a, slot).wait()
        return carry

    lax.fori_loop(0, 2 * tm, wait, 0, unroll=8)
    w = w_ref[...]
    x = h_ref[...] + w[:, 0:1] * ybuf[slot, 0:tm, :] + w[:, 1:2] * ybuf[slot, tm:2 * tm, :]
    if final_norm:
        ms = jnp.mean(x * x, axis=-1, keepdims=True)
        x = x * lax.rsqrt(ms + NORM_EPS) * g_ref[...]
    o_ref[...] = x


def moe_combine(h, y, dest, wts, g_final=None):
    t, d = h.shape
    tm = MOE_COMBINE_TM
    n = t // tm
    dtile = dest.reshape(n, tm, 2).transpose(0, 2, 1).reshape(n, 1, 2 * tm)
    g = jnp.ones((1, d), F32) if g_final is None else g_final.reshape(1, d)
    kern = functools.partial(_combine_kernel, final_norm=g_final is not None)
    return pl.pallas_call(
        kern,
        grid=(n,),
        in_specs=[
            pl.BlockSpec((1, 1, 2 * tm), lambda i: (i, 0, 0), memory_space=pltpu.SMEM),
            pl.BlockSpec((1, 1, 2 * tm), lambda i: (jnp.minimum(i + 1, n - 1), 0, 0), memory_space=pltpu.SMEM),
            pl.BlockSpec((tm, d), lambda i: (i, 0)),
            pl.BlockSpec((tm, 128), lambda i: (i, 0)),
            pl.BlockSpec((1, d), lambda i: (0, 0)),
            pl.BlockSpec(memory_space=pl.ANY),
        ],
        out_specs=pl.BlockSpec((tm, d), lambda i: (i, 0)),
        out_shape=jax.ShapeDtypeStruct((t, d), F32),
        scratch_shapes=[pltpu.VMEM((2, 2 * tm, d), F32), pltpu.SemaphoreType.DMA((2,))],
        compiler_params=_cp(("arbitrary",)),
        name="moe_combine",
    )(dtile, dtile, h, wts, g, y)


def hier_moe_block(h, g, wg, bg, we, be, w_gate, w_up, w_down, layer, g_final=None):
    t, d = h.shape
    tm = MOE_TM
    ids, wts, rank, counts = moe_router(h, g, wg, bg, we, be)
    counts = counts[0, :MOE_EXPERTS]
    padded = ((counts + tm - 1) // tm) * tm
    pad_end = jnp.cumsum(padded)
    pad_off = pad_end - padded
    dest = jnp.take(pad_off, ids[:, :2]) + rank[:, :2]
    np_rows = 2 * t + MOE_EXPERTS * tm
    tile_start = jnp.arange(np_rows // tm, dtype=jnp.int32) * tm
    tile_expert = jnp.minimum(jnp.sum((tile_start[:, None] >= pad_end[None, :]).astype(jnp.int32), axis=1),
                              MOE_EXPERTS - 1)
    tile_valid = (tile_start < pad_end[-1]).astype(jnp.int32)
    xs = moe_dispatch(h, dest, pad_end.astype(jnp.int32), np_rows)
    y = moe_ffn(xs, g, tile_expert, tile_valid, w_gate, w_up, w_down, layer)
    return moe_combine(h, y, dest, wts, g_final)


SSM_D_INNER = 2048
SSM_HEADS = 32
SSM_GROUPS = 4
SSM_HPG = 8
SSM_STATE = 128
SSM_CONV = 4
SSM_CHUNK = 256
SSM_NORM_EPS = 1e-5
SSM_CONV_CH = SSM_D_INNER + 2 * SSM_GROUPS * SSM_STATE
SSM_IN_COLS = SSM_D_INNER + SSM_CONV_CH + SSM_HEADS
SSM_IN_PAD = 5376
SSM_GW = SSM_HPG * HEAD_DIM
SSD_X0 = SSM_D_INNER


def _ssd_kernel(x_ref, b_ref, c_ref, z_ref, dt_ref, cwx_ref, cwb_ref, cwc_ref, cbx_ref, cbb_ref, cbc_ref,
                dtb_ref, alog_ref, dsk_ref, nw_ref, sel_ref, e8_ref, tri_ref, o_ref, h_ref, tail_ref):
    c = pl.program_id(2)
    lc = SSM_CHUNK

    @pl.when(c == 0)
    def _():
        h_ref[...] = jnp.zeros_like(h_ref)
        tail_ref[...] = jnp.zeros_like(tail_ref)

    def conv_silu(raw_ref, w_ref, bias_ref, lo):
        raw = raw_ref[...]
        hi = lo + raw.shape[1]
        xx = jnp.concatenate([tail_ref[:, lo:hi], raw], axis=0)
        w = w_ref[...]
        acc = bias_ref[...] + w[SSM_CONV - 1:SSM_CONV, :] * raw
        for k in range(1, SSM_CONV):
            acc = acc + w[SSM_CONV - 1 - k:SSM_CONV - k, :] * xx[8 - k:8 - k + lc, :]
        tail_ref[:, lo:hi] = raw[lc - 8:lc, :]
        return _silu(acc)

    x = conv_silu(x_ref, cwx_ref, cbx_ref, 0)
    bm = conv_silu(b_ref, cwb_ref, cbb_ref, SSM_GW)
    cm = conv_silu(c_ref, cwc_ref, cbc_ref, SSM_GW + SSM_STATE)

    sel = sel_ref[0]
    e8 = e8_ref[...]
    tri = tri_ref[...]
    dt_all = _softplus(dt_ref[...] + dtb_ref[...])
    a_all = dt_all * (-jnp.exp(alog_ref[...]))
    dt = _dot_xl(dt_all, sel)
    cs = _dot_xl(_dot_xr(tri, a_all), sel)
    cst = cs.T
    cs_last = cs[lc - 1:lc, :]

    xdt = x * _dot_xl(dt, e8)
    cb = _dot1(cm, bm, 1, 1)
    row = lax.broadcasted_iota(jnp.int32, (lc, lc), 0)
    col = lax.broadcasted_iota(jnp.int32, (lc, lc), 1)
    causal = row >= col
    ys = []
    for hh in range(SSM_HPG):
        lmat = jnp.where(causal, jnp.exp(cs[:, hh:hh + 1] - cst[hh:hh + 1, :]), 0.0)
        ys.append(_dot1(cb * lmat, xdt[:, hh * HEAD_DIM:(hh + 1) * HEAD_DIM]))
    y = jnp.concatenate(ys, axis=1)

    h_in = h_ref[...]
    y = y + _dot1(cm, h_in) * _dot_xl(jnp.exp(cs), e8)
    xdec = xdt * _dot_xl(jnp.exp(cs_last - cs), e8)
    h_ref[...] = h_in * _dot_xl(jnp.exp(cs_last), e8) + _dot1(bm, xdec, 0, 0)

    y = y + x * dsk_ref[...]
    zz = z_ref[...]
    y = y * _silu(zz)
    y = y * lax.rsqrt(jnp.mean(y * y, axis=-1, keepdims=True) + SSM_NORM_EPS) * nw_ref[...]
    o_ref[...] = y.astype(BF16)


def ssd_mix(zx, conv_w, conv_b, dt_bias, a_log, d_skip, norm_w, batch, seq):
    t = zx.shape[0]
    lc = SSM_CHUNK
    nc = seq // lc
    g, hp = SSM_GROUPS, SSM_HPG
    lanes = 128
    pad_row = lambda v: jnp.pad(v, (0, lanes - v.shape[0])).reshape(1, lanes)
    sel = np.zeros((g, lanes, lanes), np.float32)
    for gg in range(g):
        sel[gg, gg * hp + np.arange(hp), np.arange(hp)] = 1.0
    e8 = np.zeros((lanes, SSM_GW), np.float32)
    e8[:hp] = np.kron(np.eye(hp), np.ones((1, HEAD_DIM)))
    tri = jnp.asarray(np.tril(np.ones((lc, lc))), BF16)
    d_exp = jnp.repeat(d_skip, HEAD_DIM).reshape(1, SSM_D_INNER)
    xb = SSD_X0 // SSM_GW
    bb = (SSD_X0 + SSM_D_INNER) // SSM_STATE
    cb = bb + g
    wb = SSM_D_INNER // SSM_STATE
    row = lambda w, f: pl.BlockSpec((lc, w), lambda b, gg, c: (b * nc + c, f(gg)))
    par = lambda r, w, f: pl.BlockSpec((r, w), lambda b, gg, c: (0, f(gg)))
    const = lambda shape: pl.BlockSpec(shape, lambda b, gg, c: (0,) * len(shape))
    cb2 = conv_b.reshape(1, -1)
    return pl.pallas_call(
        _ssd_kernel,
        grid=(batch, g, nc),
        in_specs=[
            row(SSM_GW, lambda gg: xb + gg),
            row(SSM_STATE, lambda gg: bb + gg),
            row(SSM_STATE, lambda gg: cb + gg),
            row(SSM_GW, lambda gg: gg),
            row(lanes, lambda gg: (SSD_X0 + SSM_CONV_CH) // lanes),
            par(SSM_CONV, SSM_GW, lambda gg: gg),
            par(SSM_CONV, SSM_STATE, lambda gg: wb + gg),
            par(SSM_CONV, SSM_STATE, lambda gg: wb + g + gg),
            par(1, SSM_GW, lambda gg: gg),
            par(1, SSM_STATE, lambda gg: wb + gg),
            par(1, SSM_STATE, lambda gg: wb + g + gg),
            const((1, lanes)), const((1, lanes)),
            par(1, SSM_GW, lambda gg: gg),
            par(1, SSM_GW, lambda gg: gg),
            pl.BlockSpec((1, lanes, lanes), lambda b, gg, c: (gg, 0, 0)),
            const((lanes, SSM_GW)), const((lc, lc)),
        ],
        out_specs=pl.BlockSpec((lc, SSM_GW), lambda b, gg, c: (b * nc + c, gg)),
        out_shape=jax.ShapeDtypeStruct((t, SSM_D_INNER), BF16),
        scratch_shapes=[pltpu.VMEM((SSM_STATE, SSM_GW), F32), pltpu.VMEM((8, SSM_GW + 2 * SSM_STATE), F32)],
        compiler_params=_cp(("parallel", "parallel", "arbitrary")),
        name="ssd",
    )(zx, zx, zx, zx, zx, conv_w, conv_w, conv_w, cb2, cb2, cb2, pad_row(dt_bias), pad_row(a_log),
      d_exp, norm_w.reshape(1, -1), jnp.asarray(sel, BF16), jnp.asarray(e8, BF16), tri)


def mamba2_block(h, g_norm, w_in, conv_w, conv_b, dt_bias, a_log, d_skip, norm_w, w_out, batch, seq):
    w_in_p = jnp.pad(w_in, ((0, 0), (0, SSM_IN_PAD - SSM_IN_COLS))).astype(BF16)
    zx = norm_matmul(h, g_norm, w_in_p, tm=1024, tn=768)
    y = ssd_mix(zx, conv_w, conv_b, dt_bias, a_log, d_skip, norm_w, batch, seq)
    return matmul_residual(y, w_out.astype(BF16), h, tm=1024, tn=512)


RWKV_HEADS = 8
RWKV_WIDTH = 512
RWKV_DECAY_RANK = 32
RWKV_A_RANK = 32
RWKV_GATE_RANK = 96
RWKV_GN_EPS = 64e-5
RWKV_COLS = 3 * RWKV_WIDTH + RWKV_DECAY_RANK + RWKV_A_RANK + RWKV_GATE_RANK
RWKV_COLS_PAD = 1792
RWKV_LO = 3 * RWKV_WIDTH
RWKV_LO_W = RWKV_COLS_PAD - RWKV_LO
RWKV_CHUNK = 64


def _rwkv_prep_kernel(c_ref, p_ref, mu_ref, w0_ref, wup_ref, a0_ref, aup_ref, gup_ref, kk_ref, ka_ref, rk_ref,
                      bd_ref, tri_ref,
                      rt_ref, at_ref, bt_ref, kt_ref, v_ref, bh_ref, kh_ref, pc_ref, g_ref, bonus_ref, *, seq, tm):
    i = pl.program_id(0)
    w = RWKV_WIDTH
    c = c_ref[...]
    first = (i * tm) % seq == 0
    prev_row = jnp.where(first, 0.0, p_ref[7:8, :])
    row = lax.broadcasted_iota(jnp.int32, c.shape, 0)
    shifted = jnp.where(row == 0, prev_row, pltpu.roll(c, 1, axis=0))
    z = c + (shifted - c) * mu_ref[...]
    r = z[:, 0:w]
    k = z[:, w:2 * w]
    v = z[:, 2 * w:3 * w]
    lo = z[:, RWKV_LO:RWKV_COLS_PAD]
    w_log = -_softplus(-(w0_ref[...] + _dot3(jnp.tanh(lo), wup_ref[...]))) - 0.5
    lw = -jnp.exp(w_log)
    a = _sigmoid(a0_ref[...] + _dot3(lo, aup_ref[...]))
    g_ref[...] = _dot3(_sigmoid(lo), gup_ref[...])
    bd = bd_ref[...]
    kk = k * kk_ref[...]
    kk = kk * lax.rsqrt(jnp.maximum(_dot_xl(kk * kk, bd), 1e-24))
    k = k * (1.0 + (a - 1.0) * ka_ref[...])
    bonus_ref[...] = _dot_xl(r * k * rk_ref[...], bd) * v
    v_ref[...] = v
    av = -kk
    bv = kk * a
    tri = tri_ref[...]
    lc = RWKV_CHUNK
    pcs = []
    for cc in range(tm // lc):
        sl = slice(cc * lc, (cc + 1) * lc)
        lwc = lw[sl]
        cs = _dot_xr(tri, lwc)
        cs_end = cs[lc - 1:lc, :]
        e_in = jnp.exp(cs)
        e_out = jnp.exp(-cs)
        e_end = jnp.exp(cs_end - cs)
        rt_ref[sl, :] = r[sl] * e_in
        at_ref[sl, :] = av[sl] * jnp.exp(cs - lwc)
        bt_ref[sl, :] = bv[sl] * e_out
        kt_ref[sl, :] = k[sl] * e_out
        bh_ref[sl, :] = bv[sl] * e_end
        kh_ref[sl, :] = k[sl] * e_end
        pcs.append(jnp.exp(cs_end))
    pc_ref[...] = jnp.concatenate(pcs, axis=0)


def _rwkv_chunk_kernel(rt_ref, at_ref, bt_ref, kt_ref, v_ref, bh_ref, kh_ref, pc_ref,
                       rh_ref, yh_ref, phi_ref, psi_ref):
    lc = RWKV_CHUNK
    d = HEAD_DIM
    row = lax.broadcasted_iota(jnp.int32, (lc, lc), 0)
    col = lax.broadcasted_iota(jnp.int32, (lc, lc), 1)
    strict = row > col
    incl = row >= col
    eye = (row == col).astype(F32)
    n_ch = rt_ref.shape[0] // lc
    probs = [(cc, hh) for cc in range(n_ch) for hh in range(RWKV_HEADS)]

    def load(ref):
        return [ref[cc * lc:(cc + 1) * lc, hh * d:(hh + 1) * d] for cc, hh in probs]

    def each(fn, *lists):
        return [fn(*xs) for xs in zip(*lists)]

    rt, at, bt, kt, vv, bh, kh = (load(r) for r in (rt_ref, at_ref, bt_ref, kt_ref, v_ref, bh_ref, kh_ref))
    lowp = lambda xs: [x.astype(BF16) for x in xs]
    rtb, atb, btb, ktb, vvb, bhb, khb = (lowp(x) for x in (rt, at, bt, kt, vv, bh, kh))
    nt = lambda a, b: _dg(a, b, 1, 1)
    tn = lambda a, b: _dg(a, b, 0, 0)
    arb = each(lambda a, r: jnp.concatenate([a, r], axis=0), atb, rtb)
    gb = each(nt, arb, btb)
    gk = each(nt, arb, ktb)
    lab = each(lambda x: jnp.where(strict, x[:lc], 0.0), gb)
    lak = each(lambda x: jnp.where(strict, x[:lc], 0.0), gk)
    mrb = lowp(each(lambda x: jnp.where(incl, x[lc:], 0.0), gb))
    mrk = lowp(each(lambda x: jnp.where(incl, x[lc:], 0.0), gk))
    tinv = each(lambda l: eye + l, lab)
    lp = lowp(lab)
    lp = lowp(each(_dg, lp, lp))
    for _ in range(4):
        both = each(lambda l, t: _dg(jnp.concatenate([l, t.astype(BF16)], axis=0), l), lp, tinv)
        tinv = each(lambda t, x: t + x[lc:], tinv, both)
        lp = lowp(each(lambda x: x[:lc], both))
    tinv = lowp(each(lambda t, l: t + _dg(t.astype(BF16), l), tinv, lp))
    lv = lowp(each(_dg, lowp(lak), vvb))
    uh = lowp(each(_dg, tinv, lv))
    ah = lowp(each(_dg, tinv, atb))
    rh = each(lambda r, m, a: r + _dg(m, a), rt, mrb, ah)
    yh = each(lambda m, u, mk, v: _dg(m, u) + _dg(mk, v), mrb, uh, mrk, vvb)
    phi = each(lambda b, a: tn(b, a), bhb, ah)
    psi = each(lambda b, u, k, v: tn(b, u) + tn(k, v), bhb, uh, khb, vvb)
    for cc in range(n_ch):
        sl = slice(cc * lc, (cc + 1) * lc)
        ps = slice(cc * RWKV_HEADS, (cc + 1) * RWKV_HEADS)
        pc_row = pc_ref[cc]
        rh_ref[sl, :] = jnp.concatenate(rh[ps], axis=1)
        yh_ref[sl, :] = jnp.concatenate(yh[ps], axis=1)
        phi_ref[sl, :] = jnp.concatenate(phi[ps], axis=1) + jnp.concatenate([eye] * RWKV_HEADS, axis=1) * pc_row
        psi_ref[sl, :] = jnp.concatenate(psi[ps], axis=1)


def _rwkv_scan_kernel(rh_ref, yh_ref, phi_ref, psi_ref, bonus_ref, g_ref, gw_ref, gb_ref, bd_ref, o_ref, st_ref,
                      *, n_chunks):
    lc = RWKV_CHUNK
    d = HEAD_DIM

    @pl.when(pl.program_id(1) == 0)
    def _():
        st_ref[...] = jnp.zeros_like(st_ref)

    hsl = [slice(hh * d, (hh + 1) * d) for hh in range(RWKV_HEADS)]
    rows = []
    for cc in range(n_chunks):
        sl = slice(cc * lc, (cc + 1) * lc)
        stb = [st_ref[hh].astype(BF16) for hh in range(RWKV_HEADS)]
        ys = [_dg(rh_ref[sl, hs].astype(BF16), st) + yh_ref[sl, hs] for hs, st in zip(hsl, stb)]
        for hh, (hs, st) in enumerate(zip(hsl, stb)):
            st_ref[hh] = _dg(phi_ref[sl, hs].astype(BF16), st) + psi_ref[sl, hs]
        rows.append(jnp.concatenate(ys, axis=1))
    y = jnp.concatenate(rows, axis=0)
    bd = bd_ref[...]
    mean = _dot_xl(y, bd) * (1.0 / d)
    yc = y - mean
    var = _dot_xl(yc * yc, bd) * (1.0 / d)
    yn = yc * lax.rsqrt(var + RWKV_GN_EPS) * gw_ref[...] + gb_ref[...]
    o_ref[...] = (yn + bonus_ref[...]) * g_ref[...]


def rwkv7_mix(cols, mu, w0, w_up, a0, a_up, g_up, k_k, k_a, r_k, gn_w, gn_b, batch, seq, col_block=0):
    t = cols.shape[0]
    w = RWKV_WIDTH
    lc = RWKV_CHUNK
    tm = 512
    row2 = lambda x: x.reshape(1, -1)
    mu_p = jnp.pad(mu, (0, RWKV_COLS_PAD - RWKV_COLS)).reshape(1, -1)
    r0 = RWKV_DECAY_RANK
    r1 = r0 + RWKV_A_RANK
    r2 = r1 + RWKV_GATE_RANK
    wup_p = jnp.zeros((RWKV_LO_W, w), F32).at[0:r0].set(w_up)
    aup_p = jnp.zeros((RWKV_LO_W, w), F32).at[r0:r1].set(a_up)
    gup_p = jnp.zeros((RWKV_LO_W, w), F32).at[r1:r2].set(g_up)
    bd = jnp.asarray(np.kron(np.eye(RWKV_HEADS), np.ones((HEAD_DIM, HEAD_DIM))), BF16)
    tri = jnp.asarray(np.tril(np.ones((lc, lc))), BF16)
    full = lambda shape: pl.BlockSpec(shape, lambda i: (0,) * len(shape))
    tok = pl.BlockSpec((tm, w), lambda i: (i, 0))
    big = jax.ShapeDtypeStruct((t, w), F32)
    prep = functools.partial(_rwkv_prep_kernel, seq=seq, tm=tm)
    rt, at, bt, kt, v, bh, kh, pc, g, bonus = pl.pallas_call(
        prep,
        grid=(t // tm,),
        in_specs=[
            pl.BlockSpec((tm, RWKV_COLS_PAD), lambda i: (i, col_block)),
            pl.BlockSpec((8, RWKV_COLS_PAD), lambda i: (jnp.maximum(i * (tm // 8) - 1, 0), col_block)),
            full((1, RWKV_COLS_PAD)),
            full((1, w)), full((RWKV_LO_W, w)), full((1, w)), full((RWKV_LO_W, w)), full((RWKV_LO_W, w)),
            full((1, w)), full((1, w)), full((1, w)), full((w, w)), full((lc, lc)),
        ],
        out_specs=[tok, tok, tok, tok, tok, tok, tok, pl.BlockSpec((tm // lc, w), lambda i: (i, 0)), tok, tok],
        out_shape=[big, big, big, big, big, big, big, jax.ShapeDtypeStruct((t // lc, w), F32), big, big],
        compiler_params=_cp(("parallel",)),
        name="rwkv_prep",
    )(cols, cols, mu_p, row2(w0), wup_p, row2(a0), aup_p, gup_p, row2(k_k), row2(k_a), row2(r_k), bd, tri)

    n_ch = 1
    ch = pl.BlockSpec((n_ch * lc, w), lambda i: (i, 0))
    rh, yh, phi, psi = pl.pallas_call(
        _rwkv_chunk_kernel,
        grid=(t // (n_ch * lc),),
        in_specs=[ch, ch, ch, ch, ch, ch, ch, pl.BlockSpec((n_ch, 1, w), lambda i: (i, 0, 0))],
        out_specs=[ch, ch, ch, ch],
        out_shape=[big, big, big, big],
        compiler_params=_cp(("parallel",)),
        name="rwkv_chunk",
    )(rt, at, bt, kt, v, bh, kh, pc.reshape(t // lc, 1, w))

    n_chunks = 4
    ts = n_chunks * lc
    nsb = seq // ts
    blk = pl.BlockSpec((ts, w), lambda b, j: (b * nsb + j, 0))
    scan = functools.partial(_rwkv_scan_kernel, n_chunks=n_chunks)
    return pl.pallas_call(
        scan,
        grid=(batch, nsb),
        in_specs=[blk, blk, blk, blk, blk, blk,
                  pl.BlockSpec((1, w), lambda b, j: (0, 0)), pl.BlockSpec((1, w), lambda b, j: (0, 0)),
                  pl.BlockSpec((w, w), lambda b, j: (0, 0))],
        out_specs=blk,
        out_shape=big,
        scratch_shapes=[pltpu.VMEM((RWKV_HEADS, HEAD_DIM, HEAD_DIM), F32)],
        compiler_params=_cp(("parallel", "arbitrary")),
        name="rwkv_scan",
    )(rh, yh, phi, psi, bonus, g, row2(gn_w), row2(gn_b), bd)


NSA_Q_HEADS = 8
NSA_KV_HEADS = 2
NSA_GROUP = 4
NSA_WIDTH = 512
NSA_KV_WIDTH = 128
NSA_COLS = NSA_WIDTH + 6 * NSA_KV_WIDTH + 3 * NSA_Q_HEADS
CMP_BLOCK = 32
CMP_STRIDE = 16
CMP_HIDDEN = 128
SLC_BLOCK = 64
SLC_TOPN = 16
WINDOW = 512
FORCED_SCORE = 1e6
NSA_TQ = 128
NSA_CB_KC, NSA_CB_VC, NSA_CB_KS, NSA_CB_VS, NSA_CB_KW, NSA_CB_VW, NSA_CB_GATE = 4, 5, 6, 7, 8, 9, 10


def _rope_tables(pos, reps):
    half = ROPE_DIM // 2
    inv = jnp.power(ROPE_THETA, -jnp.arange(half, dtype=F32) * (2.0 / ROPE_DIM))
    ang = pos.astype(F32)[:, None] * inv[None, :]
    cos = jnp.cos(ang)
    sin = jnp.sin(ang)
    p = pos.shape[0]
    ones = jnp.ones((p, HEAD_DIM - ROPE_DIM), F32)
    zeros = jnp.zeros((p, HEAD_DIM - half), F32)
    cos_t = jnp.concatenate([cos, cos, ones], axis=1)
    sin_a = jnp.concatenate([-sin, zeros], axis=1)
    sin_b = jnp.concatenate([jnp.zeros((p, half), F32), sin, jnp.zeros((p, HEAD_DIM - ROPE_DIM), F32)], axis=1)
    tile = lambda x: jnp.tile(x, (1, reps))
    return tile(cos_t), tile(sin_a), tile(sin_b)


def _rope(x, cos_t, sin_a, sin_b):
    n = x.shape[1]
    half = ROPE_DIM // 2
    return x * cos_t + pltpu.roll(x, n - half, axis=1) * sin_a + pltpu.roll(x, half, axis=1) * sin_b


def _compress_kernel(xk_ref, xv_ref, pek_ref, w1k_ref, b1k_ref, w2k_ref, pev_ref, w1v_ref, b1v_ref, w2v_ref,
                     cos_ref, sa_ref, sb_ref, ko_ref, vo_ref):
    n_half = ko_ref.shape[1]
    half = CMP_BLOCK // 2

    def mlp(x_ref, pe_ref, w1_ref, b1_ref, w2_ref):
        xs = [x_ref[pl.ds(l, n_half, stride=CMP_STRIDE), :] for l in range(half)]
        top = [_dot3(x + pe_ref[l:l + 1, :], w1_ref[l]) for l, x in enumerate(xs)]
        bot = [_dot3(x + pe_ref[half + l:half + l + 1, :], w1_ref[half + l]) for l, x in enumerate(xs)]
        hid = sum(top[1:], top[0]) + pltpu.roll(sum(bot[1:], bot[0]), n_half - 1, axis=0) + b1_ref[...]
        return _dot3(_silu(hid), w2_ref[...])

    k = mlp(xk_ref, pek_ref, w1k_ref, b1k_ref, w2k_ref)
    ko_ref[0] = _rope(k, cos_ref[...], sa_ref[...], sb_ref[...])
    vo_ref[0] = mlp(xv_ref, pev_ref, w1v_ref, b1v_ref, w2v_ref)


def nsa_compress(cols, pe_k, w1_k, b1_k, w2_k, pe_v, w1_v, b1_v, w2_v, batch, seq):
    hk, d = NSA_KV_HEADS, HEAD_DIM
    n_half = seq // CMP_STRIDE
    cmp_end = jnp.arange(n_half) * CMP_STRIDE + (CMP_BLOCK - 1)
    cos_t, sin_a, sin_b = _rope_tables(cmp_end, hk)
    eye = jnp.eye(hk, dtype=F32)
    pair = lambda w: jnp.kron(eye, w)
    w1_pair = lambda w: jax.vmap(pair)(w.reshape(CMP_BLOCK, d, CMP_HIDDEN))
    lanes2 = lambda v: jnp.tile(v, (1, hk))
    full = lambda shape: pl.BlockSpec(shape, lambda b: (0,) * len(shape))
    xspec = lambda cb: pl.BlockSpec((seq, hk * d), lambda b: (b, cb))
    ospec = pl.BlockSpec((1, n_half, hk * d), lambda b: (b, 0, 0))
    wspecs = [full((CMP_BLOCK, hk * d)), full((CMP_BLOCK, hk * d, hk * CMP_HIDDEN)), full((1, hk * CMP_HIDDEN)),
              full((hk * CMP_HIDDEN, hk * d))]
    oshape = jax.ShapeDtypeStruct((batch, n_half, hk * d), F32)
    return pl.pallas_call(
        _compress_kernel,
        grid=(batch,),
        in_specs=[xspec(NSA_CB_KC), xspec(NSA_CB_VC)] + wspecs + wspecs + [full((n_half, hk * d))] * 3,
        out_specs=[ospec, ospec],
        out_shape=[oshape, oshape],
        compiler_params=_cp(("parallel",)),
        name="nsa_compress",
    )(cols, cols, lanes2(pe_k), w1_pair(w1_k), lanes2(b1_k.reshape(1, -1)), pair(w2_k),
      lanes2(pe_v), w1_pair(w1_v), lanes2(b1_v.reshape(1, -1)), pair(w2_v), cos_t, sin_a, sin_b)


def _rope_t(x, cos_t, sin_a, sin_b):
    half = ROPE_DIM // 2
    up = jnp.concatenate([x[half:], x[:half]], axis=0)
    down = jnp.concatenate([x[-half:], x[:-half]], axis=0)
    return x * cos_t + up * sin_a + down * sin_b


def _nsa_attn_kernel(q_ref, ks_ref, vs_ref, kw_ref, vw_ref, gate_ref, kc_ref, vc_ref,
                     cosk_ref, sak_ref, sbk_ref, cosq_ref, saq_ref, sbq_ref, ovl_ref, get_ref,
                     o_ref, ks_s, vs_s, kw_s, vw_s, mem_s, *, seq):
    i = pl.program_id(1)
    tq = NSA_TQ
    d = HEAD_DIM
    g = NSA_GROUP
    cols = g * tq
    n_kv = NSA_KV_HEADS
    n_tiles = seq // tq

    @pl.when(i == 0)
    def _():
        ck, sa, sb = cosk_ref[...], sak_ref[...], sbk_ref[...]
        ks = _rope(ks_ref[...], ck, sa, sb)
        kw = _rope(kw_ref[...], ck, sa, sb)
        for hk in range(n_kv):
            hs = slice(hk * d, (hk + 1) * d)
            ks_s[hk] = ks[:, hs].astype(BF16)
            kw_s[hk] = kw[:, hs].astype(BF16)
        for j in range(n_tiles):
            rs = slice(j * tq, (j + 1) * tq)
            vs_s[j] = vs_ref[rs, :].T.astype(BF16)
            vw_s[j] = vw_ref[rs, :].T.astype(BF16)

    qt = _rope_t(q_ref[...].T, cosq_ref[...], saq_ref[...], sbq_ref[...]) * (HEAD_DIM ** -0.5)
    gexp = _dot_xr(get_ref[...], _sigmoid(gate_ref[...].T))

    pos_q = i * tq + (lax.broadcasted_iota(jnp.int32, (1, cols), 1) & (tq - 1))
    sub = lax.broadcasted_iota(jnp.int32, (tq, cols), 0)
    blk = lax.broadcasted_iota(jnp.int32, (32, tq), 0)
    cur = (i * tq + lax.broadcasted_iota(jnp.int32, (32, tq), 1)) // SLC_BLOCK
    sel_valid = blk <= cur
    sel_forced = (blk == 0) | (blk == cur) | (blk == cur - 1)

    q4 = [jnp.concatenate([qt[(hk * g + gg) * d:(hk * g + gg + 1) * d, :] for gg in range(g)], axis=1)
          for hk in range(n_kv)]
    q4b = [(x * math.log2(math.e)).astype(BF16) for x in q4]

    cend = sub * CMP_STRIDE + (CMP_BLOCK - 1)
    cmask = (cend <= pos_q) & (sub < seq // CMP_STRIDE - 1)
    o_cmp = []
    for hk in range(n_kv):
        s = _dot3(kc_ref[0, :, hk * d:(hk + 1) * d], q4[hk])
        sm = jnp.where(cmask, s, MASK_NEG)
        e = jnp.exp(sm - jnp.max(sm, axis=0, keepdims=True))
        p = e / jnp.sum(e, axis=0, keepdims=True) * cmask.astype(F32)
        o_cmp.append(_dot1(vc_ref[0, :, hk * d:(hk + 1) * d], p, 0, 0))
        psum = p[:, 0:tq] + p[:, tq:2 * tq] + p[:, 2 * tq:3 * tq] + p[:, 3 * tq:4 * tq]
        imp = _dot_xr(ovl_ref[...], psum)
        imp = jnp.where(sel_forced, FORCED_SCORE, jnp.where(sel_valid, imp, -FORCED_SCORE))
        members = []
        for jj in range(32):
            rj = imp[jj:jj + 1, :]
            beats = (imp > rj) | ((imp == rj) & (blk < jj))
            cnt = jnp.sum(beats.astype(F32), axis=0, keepdims=True)
            members.append((cnt < SLC_TOPN).astype(F32))
        member = jnp.concatenate(members + [jnp.zeros((8, tq), F32)], axis=0)
        mem_s[hk] = jnp.concatenate([member] * g, axis=1)

    def flash(k_s, vt_s, j_lo, mask_fn):
        def scores(j):
            jl = jnp.minimum(j, n_tiles - 1)
            ks = [k_s[hk, pl.ds(jl * tq, tq), :] for hk in range(n_kv)]
            return [_dg(ks[hk], q4b[hk]) for hk in range(n_kv)], vt_s[jl]

        def update(j, ss, vt, carry):
            rel = (pos_q - j * tq) - sub
            new = []
            ps = []
            for hk in range(n_kv):
                m_run, l_run, acc = carry[hk]
                s = jnp.where(mask_fn(hk, j, rel), ss[hk], MASK_NEG)
                m_new = jnp.maximum(m_run, jnp.max(s, axis=0, keepdims=True))
                alpha = jnp.exp2(m_run - m_new)
                p = jnp.exp2(s - m_new)
                l_new = alpha * l_run + jnp.sum(p, axis=0, keepdims=True)
                ps.append(p.astype(BF16))
                new.append((m_new, l_new, alpha * acc))
            pv = [_dg(vt[hk * d:(hk + 1) * d, :], ps[hk]) for hk in range(n_kv)]
            return tuple((m, l, a + o) for (m, l, a), o in zip(new, pv))

        def body(pair, carry):
            ja = 2 * pair
            ssa, vta = scores(ja)
            ssb, vtb = scores(ja + 1)
            return update(ja + 1, ssb, vtb, update(ja, ssa, vta, carry))

        one = (jnp.full((1, cols), MASK_NEG, F32), jnp.zeros((1, cols), F32), jnp.zeros((d, cols), F32))
        fin = lax.fori_loop(j_lo // 2, i // 2 + 1, body, (one,) * n_kv)
        return [acc / l_fin for _, l_fin, acc in fin]

    first_block = sub < SLC_BLOCK

    def slc_mask(hk, j, rel):
        m0 = mem_s[hk, pl.ds(2 * j, 1), :]
        m1 = mem_s[hk, pl.ds(2 * j + 1, 1), :]
        return (jnp.where(first_block, m0, m1) > 0.5) & (rel >= 0)

    def win_mask(hk, j, rel):
        return (rel >= 0) & (rel < WINDOW)

    o_slc = flash(ks_s, vs_s, 0, slc_mask)
    o_win = flash(kw_s, vw_s, jnp.maximum(i - WINDOW // tq, 0), win_mask)

    outs = []
    for hk in range(n_kv):
        for gg in range(g):
            h = hk * g + gg
            cs = slice(gg * tq, (gg + 1) * tq)
            ge = lambda br: gexp[br * NSA_WIDTH + h * d: br * NSA_WIDTH + (h + 1) * d, :]
            outs.append(ge(0) * o_cmp[hk][:, cs] + ge(1) * o_slc[hk][:, cs] + ge(2) * o_win[hk][:, cs])
    o_ref[...] = jnp.concatenate(outs, axis=0).T


def nsa_mix(cols, pe_k, w1_k, b1_k, w2_k, pe_v, w1_v, b1_v, w2_v, batch, seq):
    t = cols.shape[0]
    tq = NSA_TQ
    nq = seq // tq
    d = HEAD_DIM
    k_cmp, v_cmp = nsa_compress(cols, pe_k, w1_k, b1_k, w2_k, pe_v, w1_v, b1_v, w2_v, batch, seq)
    pos = jnp.arange(seq)
    cos_k, sa_k, sb_k = _rope_tables(pos, 2)
    cos_q, sa_q, sb_q = (x.T for x in _rope_tables(pos, NSA_Q_HEADS))
    n_slc = seq // SLC_BLOCK
    n_half = seq // CMP_STRIDE
    blk_tok = np.arange(n_half - 1)[:, None] * CMP_STRIDE + np.arange(CMP_BLOCK)[None, :]
    overlap = np.any((blk_tok[:, :, None] // SLC_BLOCK) == np.arange(n_slc)[None, None, :], axis=1)
    ovl_t = np.zeros((32, 128), np.float32)
    ovl_t[:n_slc, :n_half - 1] = overlap.T
    gate_e = np.zeros((3 * NSA_WIDTH, 128), np.float32)
    for h in range(NSA_Q_HEADS):
        for br in range(3):
            gate_e[br * NSA_WIDTH + h * d: br * NSA_WIDTH + (h + 1) * d, h * 3 + br] = 1.0
    kvspec = lambda cb: pl.BlockSpec((seq, 128), lambda b, i: (b, cb))
    full = lambda shape: pl.BlockSpec(shape, lambda b, i: (0,) * len(shape))
    cspec = pl.BlockSpec((1, n_half, NSA_KV_WIDTH), lambda b, i: (b, 0, 0))
    kern = functools.partial(_nsa_attn_kernel, seq=seq)
    return pl.pallas_call(
        kern,
        grid=(batch, nq),
        in_specs=[
            pl.BlockSpec((tq, NSA_WIDTH), lambda b, i: (b * nq + i, 0)),
            kvspec(NSA_CB_KS), kvspec(NSA_CB_VS), kvspec(NSA_CB_KW), kvspec(NSA_CB_VW),
            pl.BlockSpec((tq, 128), lambda b, i: (b * nq + i, NSA_CB_GATE)),
            cspec, cspec,
            full((seq, 128)), full((seq, 128)), full((seq, 128)),
            pl.BlockSpec((NSA_WIDTH, tq), lambda b, i: (0, i)),
            pl.BlockSpec((NSA_WIDTH, tq), lambda b, i: (0, i)),
            pl.BlockSpec((NSA_WIDTH, tq), lambda b, i: (0, i)),
            full((32, 128)), full((3 * NSA_WIDTH, 128)),
        ],
        out_specs=pl.BlockSpec((tq, NSA_WIDTH), lambda b, i: (b * nq + i, 0)),
        out_shape=jax.ShapeDtypeStruct((t, NSA_WIDTH), F32),
        scratch_shapes=[
            pltpu.VMEM((NSA_KV_HEADS, seq, d), BF16),
            pltpu.VMEM((nq, NSA_KV_WIDTH, tq), BF16),
            pltpu.VMEM((NSA_KV_HEADS, seq, d), BF16),
            pltpu.VMEM((nq, NSA_KV_WIDTH, tq), BF16),
            pltpu.VMEM((NSA_KV_HEADS, 40, NSA_GROUP * tq), F32),
        ],
        compiler_params=_cp(("parallel", "arbitrary")),
        name="nsa_attn",
    )(cols, cols, cols, cols, cols, cols, k_cmp, v_cmp, cos_k, sa_k, sb_k, cos_q, sa_q, sb_q,
      jnp.asarray(ovl_t, BF16), jnp.asarray(gate_e, BF16))


HYB_NSA_PAD = 1792
HYB_COLS_PAD = HYB_NSA_PAD + RWKV_COLS_PAD


def _mm2_res_kernel(a1_ref, a2_ref, w1_ref, w2_ref, r_ref, o_ref):
    acc = jnp.dot(a1_ref[...].astype(BF16), w1_ref[...], preferred_element_type=F32)
    acc = acc + jnp.dot(a2_ref[...].astype(BF16), w2_ref[...], preferred_element_type=F32)
    o_ref[...] = r_ref[...] + acc


def matmul2_residual(a1, a2, w_bf16, res, tm, tn):
    t, k1 = a1.shape
    k2 = a2.shape[1]
    n = w_bf16.shape[1]
    return pl.pallas_call(
        _mm2_res_kernel,
        grid=(t // tm, n // tn),
        in_specs=[
            pl.BlockSpec((tm, k1), lambda i, j: (i, 0)),
            pl.BlockSpec((tm, k2), lambda i, j: (i, 0)),
            pl.BlockSpec((k1, tn), lambda i, j: (0, j)),
            pl.BlockSpec((k2, tn), lambda i, j: (k1 // k2, j)),
            pl.BlockSpec((tm, tn), lambda i, j: (i, j)),
        ],
        out_specs=pl.BlockSpec((tm, tn), lambda i, j: (i, j)),
        out_shape=jax.ShapeDtypeStruct((t, n), F32),
        compiler_params=_cp(("parallel", "arbitrary")),
        name="matmul2_residual",
    )(a1, a2, w_bf16, w_bf16, res)


def hybrid_block(h, g_norm, w_in, w_out, rwkv_params, nsa_params, batch, seq):
    w_rwkv = w_in[:, :RWKV_COLS]
    w_nsa = w_in[:, RWKV_COLS:]
    w_p = jnp.concatenate([
        jnp.pad(w_nsa, ((0, 0), (0, HYB_NSA_PAD - NSA_COLS))),
        jnp.pad(w_rwkv, ((0, 0), (0, RWKV_COLS_PAD - RWKV_COLS))),
    ], axis=1).astype(BF16)
    cols = norm_matmul(h, g_norm, w_p, tm=1024, tn=896)
    y_rwkv = rwkv7_mix(cols, *rwkv_params, batch=batch, seq=seq, col_block=1)
    y_nsa = nsa_mix(cols, *nsa_params, batch=batch, seq=seq)
    return matmul2_residual(y_rwkv, y_nsa, w_out.astype(BF16), h, tm=1024, tn=512)


def kernel(x, norm_mix, norm_ffn, norm_final, hyb_w_in, hyb_w_out, rwkv_mu, rwkv_w0, rwkv_w_up, rwkv_a0, rwkv_a_up, rwkv_g_up, rwkv_k_k, rwkv_k_a, rwkv_r_k, rwkv_gn_w, rwkv_gn_b, nsa_pe_k, nsa_w1_k, nsa_b1_k, nsa_w2_k, nsa_pe_v, nsa_w1_v, nsa_b1_v, nsa_w2_v, ssm_w_in, ssm_conv_w, ssm_conv_b, ssm_dt_bias, ssm_a_log, ssm_d, ssm_norm_w, ssm_w_out, moe_wg, moe_bg, moe_we, moe_be, moe_w_gate, moe_w_up, moe_w_down):
    batch, seq, d = x.shape
    depth = norm_mix.shape[0]
    h = x.reshape(batch * seq, d)
    for layer in range(depth):
        i = layer // 2
        if layer % 2 == 0:
            rwkv_params = (rwkv_mu[i], rwkv_w0[i], rwkv_w_up[i], rwkv_a0[i], rwkv_a_up[i], rwkv_g_up[i],
                           rwkv_k_k[i], rwkv_k_a[i], rwkv_r_k[i].reshape(-1), rwkv_gn_w[i], rwkv_gn_b[i])
            nsa_params = (nsa_pe_k[i], nsa_w1_k[i], nsa_b1_k[i], nsa_w2_k[i],
                          nsa_pe_v[i], nsa_w1_v[i], nsa_b1_v[i], nsa_w2_v[i])
            h = hybrid_block(h, norm_mix[layer], hyb_w_in[i], hyb_w_out[i], rwkv_params, nsa_params, batch, seq)
        else:
            h = mamba2_block(h, norm_mix[layer], ssm_w_in[i], ssm_conv_w[i], ssm_conv_b[i], ssm_dt_bias[i],
                             ssm_a_log[i], ssm_d[i], ssm_norm_w[i], ssm_w_out[i], batch, seq)
        h = hier_moe_block(h, norm_ffn[layer], moe_wg[layer], moe_bg[layer], moe_we[layer], moe_be[layer],
                           moe_w_gate, moe_w_up, moe_w_down, layer,
                           g_final=norm_final if layer == depth - 1 else None)
    return h.reshape(batch, seq, d)
```

```python
import functools
import math

import numpy as np
import jax
import jax.numpy as jnp
from jax import lax
from jax.experimental import pallas as pl
from jax.experimental.pallas import tpu as pltpu

F32 = jnp.float32
BF16 = jnp.bfloat16

D_MODEL = 1024
NORM_EPS = 1e-6
HEAD_DIM = 64
ROPE_DIM = 16
ROPE_THETA = 500000.0
MASK_NEG = -1e30

VMEM_LIMIT = 48 * 1024 * 1024


def _cp(sem, vmem=VMEM_LIMIT):
    return pltpu.CompilerParams(dimension_semantics=sem, vmem_limit_bytes=vmem)


def _dg(a, b, ca=1, cb=0):
    return lax.dot_general(a, b, (((ca,), (cb,)), ((), ())), preferred_element_type=F32)


def _split2(x):
    hi = x.astype(BF16)
    lo = (x - hi.astype(F32)).astype(BF16)
    return hi, lo


def _split3(x):
    hi = x.astype(BF16)
    r = x - hi.astype(F32)
    mid = r.astype(BF16)
    lo = (r - mid.astype(F32)).astype(BF16)
    return hi, mid, lo


def _dot1(a, b, ca=1, cb=0):
    return _dg(a.astype(BF16), b.astype(BF16), ca, cb)


def _dot3(a, b, ca=1, cb=0):
    ah, al = _split2(a)
    bh, bl = _split2(b)
    return _dg(ah, bh, ca, cb) + (_dg(ah, bl, ca, cb) + _dg(al, bh, ca, cb))


def _dot_xl(a, b01, ca=1, cb=0):
    h, m, l = _split3(a)
    return _dg(h, b01, ca, cb) + (_dg(m, b01, ca, cb) + _dg(l, b01, ca, cb))


def _dot_xr(a01, b, ca=1, cb=0):
    h, m, l = _split3(b)
    return _dg(a01, h, ca, cb) + (_dg(a01, m, ca, cb) + _dg(a01, l, ca, cb))


def _sigmoid(x):
    return 1.0 / (1.0 + jnp.exp(-x))


def _silu(x):
    return x * _sigmoid(x)


def _softplus(x):
    return jnp.maximum(x, 0.0) + jnp.log(1.0 + jnp.exp(-jnp.abs(x)))


def _norm_mm_kernel(x_ref, g_ref, w_ref, o_ref, *, tn):
    x = x_ref[...]
    ms = jnp.mean(x * x, axis=-1, keepdims=True)
    xn = (x * lax.rsqrt(ms + NORM_EPS) * g_ref[...]).astype(BF16)
    for j in range(w_ref.shape[1] // tn):
        cs = slice(j * tn, (j + 1) * tn)
        o_ref[:, cs] = jnp.dot(xn, w_ref[:, cs], preferred_element_type=F32)


def norm_matmul(x, g, w_bf16, tm, tn):
    t, d = x.shape
    n = w_bf16.shape[1]
    return pl.pallas_call(
        functools.partial(_norm_mm_kernel, tn=tn),
        grid=(t // tm,),
        in_specs=[
            pl.BlockSpec((tm, d), lambda i: (i, 0)),
            pl.BlockSpec((1, d), lambda i: (0, 0)),
            pl.BlockSpec((d, n), lambda i: (0, 0), pipeline_mode=pl.Buffered(1)),
        ],
        out_specs=pl.BlockSpec((tm, n), lambda i: (i, 0)),
        out_shape=jax.ShapeDtypeStruct((t, n), F32),
        compiler_params=_cp(("parallel",)),
        name="norm_matmul",
    )(x, g.reshape(1, d), w_bf16)


def _mm_res_kernel(a_ref, w_ref, r_ref, o_ref):
    o_ref[...] = r_ref[...] + jnp.dot(a_ref[...].astype(BF16), w_ref[...], preferred_element_type=F32)


def matmul_residual(a, w_bf16, res, tm, tn):
    t, k = a.shape
    n = w_bf16.shape[1]
    return pl.pallas_call(
        _mm_res_kernel,
        grid=(t // tm, n // tn),
        in_specs=[
            pl.BlockSpec((tm, k), lambda i, j: (i, 0)),
            pl.BlockSpec((k, tn), lambda i, j: (0, j)),
            pl.BlockSpec((tm, tn), lambda i, j: (i, j)),
        ],
        out_specs=pl.BlockSpec((tm, tn), lambda i, j: (i, j)),
        out_shape=jax.ShapeDtypeStruct((t, n), F32),
        compiler_params=_cp(("parallel", "arbitrary")),
        name="matmul_residual",
    )(a, w_bf16, res)


MOE_GROUPS = 4
MOE_EPG = 8
MOE_EXPERTS = 32
MOE_HIDDEN = 512
MOE_TM = 256
MOE_ROUTE_TM = 512
MOE_DISPATCH_TM = 256
MOE_COMBINE_TM = 128


def _router_kernel(x_ref, g_ref, w_ref, b_ref, tri_ref, ids_ref, wts_ref, rank_ref, cnt_ref, run_ref):
    @pl.when(pl.program_id(0) == 0)
    def _():
        run_ref[...] = jnp.zeros_like(run_ref)

    x = x_ref[...]
    ms = jnp.mean(x * x, axis=-1, keepdims=True)
    u = x * lax.rsqrt(ms + NORM_EPS) * g_ref[...]
    lg = _dot3(u, w_ref[...]) + b_ref[...]
    tm = x.shape[0]
    lane = lax.broadcasted_iota(jnp.int32, (tm, 128), 1)
    gl = lg[:, :128]
    el = lg[:, 128:]
    gmask = lane < MOE_GROUPS
    gmax = jnp.max(jnp.where(gmask, gl, -jnp.inf), axis=-1, keepdims=True)
    ge = jnp.where(gmask, jnp.exp(gl - gmax), 0.0)
    pg = ge / jnp.sum(ge, axis=-1, keepdims=True)
    g_val = jnp.max(pg, axis=-1, keepdims=True)
    g_idx = jnp.min(jnp.where(gmask & (pg == g_val), lane, 128), axis=-1, keepdims=True)
    emask = (lane >= g_idx * MOE_EPG) & (lane < (g_idx + 1) * MOE_EPG)
    emax = jnp.max(jnp.where(emask, el, -jnp.inf), axis=-1, keepdims=True)
    ee = jnp.where(emask, jnp.exp(el - emax), 0.0)
    pe = ee / jnp.sum(ee, axis=-1, keepdims=True)
    v1 = jnp.max(jnp.where(emask, pe, -1.0), axis=-1, keepdims=True)
    i1 = jnp.min(jnp.where(emask & (pe == v1), lane, 128), axis=-1, keepdims=True)
    m2 = emask & (lane != i1)
    v2 = jnp.max(jnp.where(m2, pe, -1.0), axis=-1, keepdims=True)
    i2 = jnp.min(jnp.where(m2 & (pe == v2), lane, 128), axis=-1, keepdims=True)
    den = v1 + v2
    w1 = v1 / den * g_val
    w2 = v2 / den * g_val
    ids_ref[...] = jnp.where(lane == 0, i1, jnp.where(lane == 1, i2, 0))
    wts_ref[...] = jnp.where(lane == 0, w1, jnp.where(lane == 1, w2, 0.0))
    oh1 = (lane == i1).astype(F32)
    oh2 = (lane == i2).astype(F32)
    oh = oh1 + oh2
    base = run_ref[...] + _dg(tri_ref[...], oh.astype(BF16))
    r1 = jnp.sum(oh1 * base, axis=-1, keepdims=True)
    r2 = jnp.sum(oh2 * base, axis=-1, keepdims=True)
    rank_ref[...] = jnp.where(lane == 0, r1, jnp.where(lane == 1, r2, 0.0)).astype(jnp.int32)
    total = run_ref[...] + jnp.sum(oh, axis=0, keepdims=True)
    run_ref[...] = total
    cnt_ref[...] = total.astype(jnp.int32)


def moe_router(h, g, wg, bg, we, be):
    t, d = h.shape
    tm = MOE_ROUTE_TM
    w = jnp.zeros((d, 256), F32).at[:, :MOE_GROUPS].set(wg).at[:, 128:128 + MOE_EXPERTS].set(we)
    b = jnp.zeros((1, 256), F32).at[0, :MOE_GROUPS].set(bg).at[0, 128:128 + MOE_EXPERTS].set(be)
    tri = jnp.asarray(np.tril(np.ones((tm, tm)), -1), BF16)
    tok = pl.BlockSpec((tm, 128), lambda i: (i, 0))
    return pl.pallas_call(
        _router_kernel,
        grid=(t // tm,),
        in_specs=[
            pl.BlockSpec((tm, d), lambda i: (i, 0)),
            pl.BlockSpec((1, d), lambda i: (0, 0)),
            pl.BlockSpec((d, 256), lambda i: (0, 0)),
            pl.BlockSpec((1, 256), lambda i: (0, 0)),
            pl.BlockSpec((tm, tm), lambda i: (0, 0)),
        ],
        out_specs=[tok, tok, tok, pl.BlockSpec((1, 128), lambda i: (0, 0))],
        out_shape=[
            jax.ShapeDtypeStruct((t, 128), jnp.int32),
            jax.ShapeDtypeStruct((t, 128), F32),
            jax.ShapeDtypeStruct((t, 128), jnp.int32),
            jax.ShapeDtypeStruct((1, 128), jnp.int32),
        ],
        scratch_shapes=[pltpu.VMEM((1, 128), F32)],
        compiler_params=_cp(("arbitrary",)),
        name="moe_router",
    )(h, g.reshape(1, d), w, b, tri)


def _dispatch_kernel(dest_ref, pend_ref, h_ref, xs_ref, zbuf, sem, zsem):
    n_tok = h_ref.shape[0]
    tile = zbuf.shape[0]

    @pl.when(pl.program_id(0) == 0)
    def _():
        zbuf[...] = jnp.zeros_like(zbuf)

        def zero_copy(row0):
            return pltpu.make_async_copy(zbuf, xs_ref.at[pl.ds(pl.multiple_of(row0, tile), tile)], zsem)

        def has_rows(e):
            return pend_ref[e] > (pend_ref[e - 1] if e > 0 else 0)

        for e in range(MOE_EXPERTS):
            @pl.when(has_rows(e))
            def _(e=e):
                zero_copy(pend_ref[e] - tile).start()

        used = pend_ref[MOE_EXPERTS - 1] // tile
        n_tiles = xs_ref.shape[0] // tile

        def start_unused(k, carry):
            zero_copy(k * tile).start()
            return carry

        def wait_unused(k, carry):
            zero_copy(k * tile).wait()
            return carry

        lax.fori_loop(used, n_tiles, start_unused, 0)
        for e in range(MOE_EXPERTS):
            @pl.when(has_rows(e))
            def _(e=e):
                zero_copy(pend_ref[e] - tile).wait()
        lax.fori_loop(used, n_tiles, wait_unused, 0)

    def row_copy(r, a):
        return pltpu.make_async_copy(h_ref.at[pl.ds(r, 1)], xs_ref.at[pl.ds(dest_ref[0, 0, a], 1)], sem)

    def start(r, carry):
        row_copy(r, 2 * r).start()
        row_copy(r, 2 * r + 1).start()
        return carry

    def wait(r, carry):
        row_copy(r, 2 * r).wait()
        row_copy(r, 2 * r + 1).wait()
        return carry

    lax.fori_loop(0, n_tok, start, 0, unroll=8)
    lax.fori_loop(0, n_tok, wait, 0, unroll=8)


def moe_dispatch(h, dest, pad_end, np_rows):
    t, d = h.shape
    tm = MOE_DISPATCH_TM
    return pl.pallas_call(
        _dispatch_kernel,
        grid=(t // tm,),
        in_specs=[
            pl.BlockSpec((1, 1, 2 * tm), lambda i: (i, 0, 0), memory_space=pltpu.SMEM),
            pl.BlockSpec(memory_space=pltpu.SMEM),
            pl.BlockSpec((tm, d), lambda i: (i, 0)),
        ],
        out_specs=pl.BlockSpec(memory_space=pl.ANY),
        out_shape=jax.ShapeDtypeStruct((np_rows, d), F32),
        scratch_shapes=[pltpu.VMEM((MOE_TM, d), F32), pltpu.SemaphoreType.DMA(()), pltpu.SemaphoreType.DMA(())],
        compiler_params=_cp(("arbitrary",)),
        name="moe_dispatch",
    )(dest.reshape(t // tm, 1, 2 * tm), pad_end, h)


def _moe_ffn_kernel(te_ref, tv_ref, x_ref, g_ref, wg_ref, wu_ref, wd_ref, o_ref, wg_s, wu_s, wd_s):
    i = pl.program_id(0)
    prev = te_ref[jnp.maximum(i - 1, 0)]
    changed = (i == 0) | (te_ref[i] != prev)

    @pl.when(changed)
    def _():
        wg_s[...] = wg_ref[0, 0].astype(BF16)
        wu_s[...] = wu_ref[0, 0].astype(BF16)
        wd_s[...] = wd_ref[0, 0].astype(BF16)

    @pl.when(tv_ref[i] > 0)
    def _():
        x = x_ref[...]
        ms = jnp.mean(x * x, axis=-1, keepdims=True)
        x = (x * lax.rsqrt(ms + NORM_EPS) * g_ref[...]).astype(BF16)
        gate = jnp.dot(x, wg_s[...], preferred_element_type=F32)
        up = jnp.dot(x, wu_s[...], preferred_element_type=F32)
        act = _silu(gate) * up
        o_ref[...] = jnp.dot(act.astype(BF16), wd_s[...], preferred_element_type=F32)

    @pl.when(tv_ref[i] == 0)
    def _():
        o_ref[...] = jnp.zeros_like(o_ref)


def moe_ffn(xs, g, tile_expert, tile_valid, w_gate, w_up, w_down, layer):
    np_rows, d = xs.shape
    f = w_gate.shape[-1]
    tm = MOE_TM
    n_tiles = np_rows // tm
    grid_spec = pltpu.PrefetchScalarGridSpec(
        num_scalar_prefetch=2,
        grid=(n_tiles,),
        in_specs=[
            pl.BlockSpec((tm, d), lambda i, te, tv: (i, 0)),
            pl.BlockSpec((1, d), lambda i, te, tv: (0, 0)),
            pl.BlockSpec((1, 1, d, f), lambda i, te, tv: (layer, te[i], 0, 0)),
            pl.BlockSpec((1, 1, d, f), lambda i, te, tv: (layer, te[i], 0, 0)),
            pl.BlockSpec((1, 1, f, d), lambda i, te, tv: (layer, te[i], 0, 0)),
        ],
        out_specs=pl.BlockSpec((tm, d), lambda i, te, tv: (i, 0)),
        scratch_shapes=[pltpu.VMEM((d, f), BF16), pltpu.VMEM((d, f), BF16), pltpu.VMEM((f, d), BF16)],
    )
    return pl.pallas_call(
        _moe_ffn_kernel,
        grid_spec=grid_spec,
        out_shape=jax.ShapeDtypeStruct((np_rows, d), F32),
        compiler_params=_cp(("arbitrary",)),
        name="moe_ffn",
    )(tile_expert, tile_valid, xs, g.reshape(1, d), w_gate, w_up, w_down)


def _combine_kernel(dcur_ref, dnxt_ref, h_ref, w_ref, g_ref, y_ref, o_ref, ybuf, sems, *, final_norm):
    i = pl.program_id(0)
    n = pl.num_programs(0)
    tm = h_ref.shape[0]
    slot = lax.rem(i, 2)

    def row_copy(dref, a, s):
        return pltpu.make_async_copy(y_ref.at[pl.ds(dref[0, 0, a], 1)], ybuf.at[s, pl.ds(a, 1)], sems.at[s])

    def fetch(dref, s):
        def start(a, carry):
            row_copy(dref, a, s).start()
            return carry
        lax.fori_loop(0, 2 * tm, start, 0, unroll=8)

    @pl.when(i == 0)
    def _():
        fetch(dcur_ref, 0)

    @pl.when(i + 1 < n)
    def _():
        fetch(dnxt_ref, 1 - slot)

    def wait(a, carry):
        row_copy(dcur_ref, a, slot).wait()
        return carry

    lax.fori_loop(0, 2 * tm, wait, 0, unroll=8)
    w = w_ref[...]
    x = h_ref[...] + w[:, 0:1] * ybuf[slot, 0:tm, :] + w[:, 1:2] * ybuf[slot, tm:2 * tm, :]
    if final_norm:
        ms = jnp.mean(x * x, axis=-1, keepdims=True)
        x = x * lax.rsqrt(ms + NORM_EPS) * g_ref[...]
    o_ref[...] = x


def moe_combine(h, y, dest, wts, g_final=None):
    t, d = h.shape
    tm = MOE_COMBINE_TM
    n = t // tm
    dtile = dest.reshape(n, tm, 2).transpose(0, 2, 1).reshape(n, 1, 2 * tm)
    g = jnp.ones((1, d), F32) if g_final is None else g_final.reshape(1, d)
    kern = functools.partial(_combine_kernel, final_norm=g_final is not None)
    return pl.pallas_call(
        kern,
        grid=(n,),
        in_specs=[
            pl.BlockSpec((1, 1, 2 * tm), lambda i: (i, 0, 0), memory_space=pltpu.SMEM),
            pl.BlockSpec((1, 1, 2 * tm), lambda i: (jnp.minimum(i + 1, n - 1), 0, 0), memory_space=pltpu.SMEM),
            pl.BlockSpec((tm, d), lambda i: (i, 0)),
            pl.BlockSpec((tm, 128), lambda i: (i, 0)),
            pl.BlockSpec((1, d), lambda i: (0, 0)),
            pl.BlockSpec(memory_space=pl.ANY),
        ],
        out_specs=pl.BlockSpec((tm, d), lambda i: (i, 0)),
        out_shape=jax.ShapeDtypeStruct((t, d), F32),
        scratch_shapes=[pltpu.VMEM((2, 2 * tm, d), F32), pltpu.SemaphoreType.DMA((2,))],
        compiler_params=_cp(("arbitrary",)),
        name="moe_combine",
    )(dtile, dtile, h, wts, g, y)


def hier_moe_block(h, g, wg, bg, we, be, w_gate, w_up, w_down, layer, g_final=None):
    t, d = h.shape
    tm = MOE_TM
    ids, wts, rank, counts = moe_router(h, g, wg, bg, we, be)
    counts = counts[0, :MOE_EXPERTS]
    padded = ((counts + tm - 1) // tm) * tm
    pad_end = jnp.cumsum(padded)
    pad_off = pad_end - padded
    dest = jnp.take(pad_off, ids[:, :2]) + rank[:, :2]
    np_rows = 2 * t + MOE_EXPERTS * tm
    tile_start = jnp.arange(np_rows // tm, dtype=jnp.int32) * tm
    tile_expert = jnp.minimum(jnp.sum((tile_start[:, None] >= pad_end[None, :]).astype(jnp.int32), axis=1),
                              MOE_EXPERTS - 1)
    tile_valid = (tile_start < pad_end[-1]).astype(jnp.int32)
    xs = moe_dispatch(h, dest, pad_end.astype(jnp.int32), np_rows)
    y = moe_ffn(xs, g, tile_expert, tile_valid, w_gate, w_up, w_down, layer)
    return moe_combine(h, y, dest, wts, g_final)


SSM_D_INNER = 2048
SSM_HEADS = 32
SSM_GROUPS = 4
SSM_HPG = 8
SSM_STATE = 128
SSM_CONV = 4
SSM_CHUNK = 256
SSM_NORM_EPS = 1e-5
SSM_CONV_CH = SSM_D_INNER + 2 * SSM_GROUPS * SSM_STATE
SSM_IN_COLS = SSM_D_INNER + SSM_CONV_CH + SSM_HEADS
SSM_IN_PAD = 5376
SSM_GW = SSM_HPG * HEAD_DIM
SSD_X0 = SSM_D_INNER


def _ssd_kernel(x_ref, b_ref, c_ref, z_ref, dt_ref, cwx_ref, cwb_ref, cwc_ref, cbx_ref, cbb_ref, cbc_ref,
                dtb_ref, alog_ref, dsk_ref, nw_ref, sel_ref, e8_ref, tri_ref, o_ref, h_ref, tail_ref):
    c = pl.program_id(2)
    lc = SSM_CHUNK

    @pl.when(c == 0)
    def _():
        h_ref[...] = jnp.zeros_like(h_ref)
        tail_ref[...] = jnp.zeros_like(tail_ref)

    def conv_silu(raw_ref, w_ref, bias_ref, lo):
        raw = raw_ref[...]
        hi = lo + raw.shape[1]
        xx = jnp.concatenate([tail_ref[:, lo:hi], raw], axis=0)
        w = w_ref[...]
        acc = bias_ref[...] + w[SSM_CONV - 1:SSM_CONV, :] * raw
        for k in range(1, SSM_CONV):
            acc = acc + w[SSM_CONV - 1 - k:SSM_CONV - k, :] * xx[8 - k:8 - k + lc, :]
        tail_ref[:, lo:hi] = raw[lc - 8:lc, :]
        return _silu(acc)

    x = conv_silu(x_ref, cwx_ref, cbx_ref, 0)
    bm = conv_silu(b_ref, cwb_ref, cbb_ref, SSM_GW)
    cm = conv_silu(c_ref, cwc_ref, cbc_ref, SSM_GW + SSM_STATE)

    sel = sel_ref[0]
    e8 = e8_ref[...]
    tri = tri_ref[...]
    dt_all = _softplus(dt_ref[...] + dtb_ref[...])
    a_all = dt_all * (-jnp.exp(alog_ref[...]))
    dt = _dot_xl(dt_all, sel)
    cs = _dot_xl(_dot_xr(tri, a_all), sel)
    cst = cs.T
    cs_last = cs[lc - 1:lc, :]

    xdt = x * _dot_xl(dt, e8)
    cb = _dot1(cm, bm, 1, 1)
    row = lax.broadcasted_iota(jnp.int32, (lc, lc), 0)
    col = lax.broadcasted_iota(jnp.int32, (lc, lc), 1)
    causal = row >= col
    ys = []
    for hh in range(SSM_HPG):
        lmat = jnp.where(causal, jnp.exp(cs[:, hh:hh + 1] - cst[hh:hh + 1, :]), 0.0)
        ys.append(_dot1(cb * lmat, xdt[:, hh * HEAD_DIM:(hh + 1) * HEAD_DIM]))
    y = jnp.concatenate(ys, axis=1)

    h_in = h_ref[...]
    y = y + _dot1(cm, h_in) * _dot_xl(jnp.exp(cs), e8)
    xdec = xdt * _dot_xl(jnp.exp(cs_last - cs), e8)
    h_ref[...] = h_in * _dot_xl(jnp.exp(cs_last), e8) + _dot1(bm, xdec, 0, 0)

    y = y + x * dsk_ref[...]
    zz = z_ref[...]
    y = y * _silu(zz)
    y = y * lax.rsqrt(jnp.mean(y * y, axis=-1, keepdims=True) + SSM_NORM_EPS) * nw_ref[...]
    o_ref[...] = y.astype(BF16)


def ssd_mix(zx, conv_w, conv_b, dt_bias, a_log, d_skip, norm_w, batch, seq):
    t = zx.shape[0]
    lc = SSM_CHUNK
    nc = seq // lc
    g, hp = SSM_GROUPS, SSM_HPG
    lanes = 128
    pad_row = lambda v: jnp.pad(v, (0, lanes - v.shape[0])).reshape(1, lanes)
    sel = np.zeros((g, lanes, lanes), np.float32)
    for gg in range(g):
        sel[gg, gg * hp + np.arange(hp), np.arange(hp)] = 1.0
    e8 = np.zeros((lanes, SSM_GW), np.float32)
    e8[:hp] = np.kron(np.eye(hp), np.ones((1, HEAD_DIM)))
    tri = jnp.asarray(np.tril(np.ones((lc, lc))), BF16)
    d_exp = jnp.repeat(d_skip, HEAD_DIM).reshape(1, SSM_D_INNER)
    xb = SSD_X0 // SSM_GW
    bb = (SSD_X0 + SSM_D_INNER) // SSM_STATE
    cb = bb + g
    wb = SSM_D_INNER // SSM_STATE
    row = lambda w, f: pl.BlockSpec((lc, w), lambda b, gg, c: (b * nc + c, f(gg)))
    par = lambda r, w, f: pl.BlockSpec((r, w), lambda b, gg, c: (0, f(gg)))
    const = lambda shape: pl.BlockSpec(shape, lambda b, gg, c: (0,) * len(shape))
    cb2 = conv_b.reshape(1, -1)
    return pl.pallas_call(
        _ssd_kernel,
        grid=(batch, g, nc),
        in_specs=[
            row(SSM_GW, lambda gg: xb + gg),
            row(SSM_STATE, lambda gg: bb + gg),
            row(SSM_STATE, lambda gg: cb + gg),
            row(SSM_GW, lambda gg: gg),
            row(lanes, lambda gg: (SSD_X0 + SSM_CONV_CH) // lanes),
            par(SSM_CONV, SSM_GW, lambda gg: gg),
            par(SSM_CONV, SSM_STATE, lambda gg: wb + gg),
            par(SSM_CONV, SSM_STATE, lambda gg: wb + g + gg),
            par(1, SSM_GW, lambda gg: gg),
            par(1, SSM_STATE, lambda gg: wb + gg),
            par(1, SSM_STATE, lambda gg: wb + g + gg),
            const((1, lanes)), const((1, lanes)),
            par(1, SSM_GW, lambda gg: gg),
            par(1, SSM_GW, lambda gg: gg),
            pl.BlockSpec((1, lanes, lanes), lambda b, gg, c: (gg, 0, 0)),
            const((lanes, SSM_GW)), const((lc, lc)),
        ],
        out_specs=pl.BlockSpec((lc, SSM_GW), lambda b, gg, c: (b * nc + c, gg)),
        out_shape=jax.ShapeDtypeStruct((t, SSM_D_INNER), BF16),
        scratch_shapes=[pltpu.VMEM((SSM_STATE, SSM_GW), F32), pltpu.VMEM((8, SSM_GW + 2 * SSM_STATE), F32)],
        compiler_params=_cp(("parallel", "parallel", "arbitrary")),
        name="ssd",
    )(zx, zx, zx, zx, zx, conv_w, conv_w, conv_w, cb2, cb2, cb2, pad_row(dt_bias), pad_row(a_log),
      d_exp, norm_w.reshape(1, -1), jnp.asarray(sel, BF16), jnp.asarray(e8, BF16), tri)


def mamba2_block(h, g_norm, w_in, conv_w, conv_b, dt_bias, a_log, d_skip, norm_w, w_out, batch, seq):
    w_in_p = jnp.pad(w_in, ((0, 0), (0, SSM_IN_PAD - SSM_IN_COLS))).astype(BF16)
    zx = norm_matmul(h, g_norm, w_in_p, tm=512, tn=768)
    y = ssd_mix(zx, conv_w, conv_b, dt_bias, a_log, d_skip, norm_w, batch, seq)
    return matmul_residual(y, w_out.astype(BF16), h, tm=1024, tn=512)


RWKV_HEADS = 8
RWKV_WIDTH = 512
RWKV_DECAY_RANK = 32
RWKV_A_RANK = 32
RWKV_GATE_RANK = 96
RWKV_GN_EPS = 64e-5
RWKV_COLS = 3 * RWKV_WIDTH + RWKV_DECAY_RANK + RWKV_A_RANK + RWKV_GATE_RANK
RWKV_COLS_PAD = 1792
RWKV_LO = 3 * RWKV_WIDTH
RWKV_LO_W = RWKV_COLS_PAD - RWKV_LO
RWKV_CHUNK = 64


def _rwkv_prep_kernel(c_ref, p_ref, mu_ref, w0_ref, wup_ref, a0_ref, aup_ref, gup_ref, kk_ref, ka_ref, rk_ref,
                      bd_ref, tri_ref,
                      rt_ref, at_ref, bt_ref, kt_ref, v_ref, bh_ref, kh_ref, pc_ref, g_ref, bonus_ref, *, seq, tm):
    i = pl.program_id(0)
    w = RWKV_WIDTH
    c = c_ref[...]
    first = (i * tm) % seq == 0
    prev_row = jnp.where(first, 0.0, p_ref[7:8, :])
    row = lax.broadcasted_iota(jnp.int32, c.shape, 0)
    shifted = jnp.where(row == 0, prev_row, pltpu.roll(c, 1, axis=0))
    z = c + (shifted - c) * mu_ref[...]
    r = z[:, 0:w]
    k = z[:, w:2 * w]
    v = z[:, 2 * w:3 * w]
    lo = z[:, RWKV_LO:RWKV_COLS_PAD]
    w_log = -_softplus(-(w0_ref[...] + _dot3(jnp.tanh(lo), wup_ref[...]))) - 0.5
    lw = -jnp.exp(w_log)
    a = _sigmoid(a0_ref[...] + _dot3(lo, aup_ref[...]))
    g_ref[...] = _dot3(_sigmoid(lo), gup_ref[...])
    bd = bd_ref[...]
    kk = k * kk_ref[...]
    kk = kk * lax.rsqrt(jnp.maximum(_dot_xl(kk * kk, bd), 1e-24))
    k = k * (1.0 + (a - 1.0) * ka_ref[...])
    bonus_ref[...] = _dot_xl(r * k * rk_ref[...], bd) * v
    v_ref[...] = v
    av = -kk
    bv = kk * a
    tri = tri_ref[...]
    lc = RWKV_CHUNK
    pcs = []
    for cc in range(tm // lc):
        sl = slice(cc * lc, (cc + 1) * lc)
        lwc = lw[sl]
        cs = _dot_xr(tri, lwc)
        cs_end = cs[lc - 1:lc, :]
        e_in = jnp.exp(cs)
        e_out = jnp.exp(-cs)
        e_end = jnp.exp(cs_end - cs)
        rt_ref[sl, :] = r[sl] * e_in
        at_ref[sl, :] = av[sl] * jnp.exp(cs - lwc)
        bt_ref[sl, :] = bv[sl] * e_out
        kt_ref[sl, :] = k[sl] * e_out
        bh_ref[sl, :] = bv[sl] * e_end
        kh_ref[sl, :] = k[sl] * e_end
        pcs.append(jnp.exp(cs_end))
    pc_ref[...] = jnp.concatenate(pcs, axis=0)


def _rwkv_chunk_kernel(rt_ref, at_ref, bt_ref, kt_ref, v_ref, bh_ref, kh_ref, pc_ref,
                       rh_ref, yh_ref, phi_ref, psi_ref):
    lc = RWKV_CHUNK
    d = HEAD_DIM
    row = lax.broadcasted_iota(jnp.int32, (lc, lc), 0)
    col = lax.broadcasted_iota(jnp.int32, (lc, lc), 1)
    strict = row > col
    incl = row >= col
    eye = (row == col).astype(F32)
    n_ch = rt_ref.shape[0] // lc
    probs = [(cc, hh) for cc in range(n_ch) for hh in range(RWKV_HEADS)]

    def load(ref):
        return [ref[cc * lc:(cc + 1) * lc, hh * d:(hh + 1) * d] for cc, hh in probs]

    def each(fn, *lists):
        return [fn(*xs) for xs in zip(*lists)]

    rt, at, bt, kt, vv, bh, kh = (load(r) for r in (rt_ref, at_ref, bt_ref, kt_ref, v_ref, bh_ref, kh_ref))
    lowp = lambda xs: [x.astype(BF16) for x in xs]
    rtb, atb, btb, ktb, vvb, bhb, khb = (lowp(x) for x in (rt, at, bt, kt, vv, bh, kh))
    nt = lambda a, b: _dg(a, b, 1, 1)
    tn = lambda a, b: _dg(a, b, 0, 0)
    arb = each(lambda a, r: jnp.concatenate([a, r], axis=0), atb, rtb)
    gb = each(nt, arb, btb)
    gk = each(nt, arb, ktb)
    lab = each(lambda x: jnp.where(strict, x[:lc], 0.0), gb)
    lak = each(lambda x: jnp.where(strict, x[:lc], 0.0), gk)
    mrb = lowp(each(lambda x: jnp.where(incl, x[lc:], 0.0), gb))
    mrk = lowp(each(lambda x: jnp.where(incl, x[lc:], 0.0), gk))
    tinv = each(lambda l: eye + l, lab)
    lp = lowp(lab)
    lp = lowp(each(_dg, lp, lp))
    for _ in range(4):
        both = each(lambda l, t: _dg(jnp.concatenate([l, t.astype(BF16)], axis=0), l), lp, tinv)
        tinv = each(lambda t, x: t + x[lc:], tinv, both)
        lp = lowp(each(lambda x: x[:lc], both))
    tinv = lowp(each(lambda t, l: t + _dg(t.astype(BF16), l), tinv, lp))
    lv = lowp(each(_dg, lowp(lak), vvb))
    uh = lowp(each(_dg, tinv, lv))
    ah = lowp(each(_dg, tinv, atb))
    rh = each(lambda r, m, a: r + _dg(m, a), rt, mrb, ah)
    yh = each(lambda m, u, mk, v: _dg(m, u) + _dg(mk, v), mrb, uh, mrk, vvb)
    phi = each(lambda b, a: tn(b, a), bhb, ah)
    psi = each(lambda b, u, k, v: tn(b, u) + tn(k, v), bhb, uh, khb, vvb)
    for cc in range(n_ch):
        sl = slice(cc * lc, (cc + 1) * lc)
        ps = slice(cc * RWKV_HEADS, (cc + 1) * RWKV_HEADS)
        pc_row = pc_ref[cc]
        rh_ref[sl, :] = jnp.concatenate(rh[ps], axis=1)
        yh_ref[sl, :] = jnp.concatenate(yh[ps], axis=1)
        phi_ref[sl, :] = jnp.concatenate(phi[ps], axis=1) + jnp.concatenate([eye] * RWKV_HEADS, axis=1) * pc_row
        psi_ref[sl, :] = jnp.concatenate(psi[ps], axis=1)


def _rwkv_scan_kernel(rh_ref, yh_ref, phi_ref, psi_ref, bonus_ref, g_ref, gw_ref, gb_ref, bd_ref, o_ref, st_ref,
                      *, n_chunks):
    lc = RWKV_CHUNK
    d = HEAD_DIM

    @pl.when(pl.program_id(1) == 0)
    def _():
        st_ref[...] = jnp.zeros_like(st_ref)

    hsl = [slice(hh * d, (hh + 1) * d) for hh in range(RWKV_HEADS)]
    rows = []
    for cc in range(n_chunks):
        sl = slice(cc * lc, (cc + 1) * lc)
        stb = [st_ref[hh].astype(BF16) for hh in range(RWKV_HEADS)]
        ys = [_dg(rh_ref[sl, hs].astype(BF16), st) + yh_ref[sl, hs] for hs, st in zip(hsl, stb)]
        for hh, (hs, st) in enumerate(zip(hsl, stb)):
            st_ref[hh] = _dg(phi_ref[sl, hs].astype(BF16), st) + psi_ref[sl, hs]
        rows.append(jnp.concatenate(ys, axis=1))
    y = jnp.concatenate(rows, axis=0)
    bd = bd_ref[...]
    mean = _dot_xl(y, bd) * (1.0 / d)
    yc = y - mean
    var = _dot_xl(yc * yc, bd) * (1.0 / d)
    yn = yc * lax.rsqrt(var + RWKV_GN_EPS) * gw_ref[...] + gb_ref[...]
    o_ref[...] = (yn + bonus_ref[...]) * g_ref[...]


def rwkv7_mix(cols, mu, w0, w_up, a0, a_up, g_up, k_k, k_a, r_k, gn_w, gn_b, batch, seq, col_block=0):
    t = cols.shape[0]
    w = RWKV_WIDTH
    lc = RWKV_CHUNK
    tm = 512
    row2 = lambda x: x.reshape(1, -1)
    mu_p = jnp.pad(mu, (0, RWKV_COLS_PAD - RWKV_COLS)).reshape(1, -1)
    r0 = RWKV_DECAY_RANK
    r1 = r0 + RWKV_A_RANK
    r2 = r1 + RWKV_GATE_RANK
    wup_p = jnp.zeros((RWKV_LO_W, w), F32).at[0:r0].set(w_up)
    aup_p = jnp.zeros((RWKV_LO_W, w), F32).at[r0:r1].set(a_up)
    gup_p = jnp.zeros((RWKV_LO_W, w), F32).at[r1:r2].set(g_up)
    bd = jnp.asarray(np.kron(np.eye(RWKV_HEADS), np.ones((HEAD_DIM, HEAD_DIM))), BF16)
    tri = jnp.asarray(np.tril(np.ones((lc, lc))), BF16)
    full = lambda shape: pl.BlockSpec(shape, lambda i: (0,) * len(shape))
    tok = pl.BlockSpec((tm, w), lambda i: (i, 0))
    big = jax.ShapeDtypeStruct((t, w), F32)
    prep = functools.partial(_rwkv_prep_kernel, seq=seq, tm=tm)
    rt, at, bt, kt, v, bh, kh, pc, g, bonus = pl.pallas_call(
        prep,
        grid=(t // tm,),
        in_specs=[
            pl.BlockSpec((tm, RWKV_COLS_PAD), lambda i: (i, col_block)),
            pl.BlockSpec((8, RWKV_COLS_PAD), lambda i: (jnp.maximum(i * (tm // 8) - 1, 0), col_block)),
            full((1, RWKV_COLS_PAD)),
            full((1, w)), full((RWKV_LO_W, w)), full((1, w)), full((RWKV_LO_W, w)), full((RWKV_LO_W, w)),
            full((1, w)), full((1, w)), full((1, w)), full((w, w)), full((lc, lc)),
        ],
        out_specs=[tok, tok, tok, tok, tok, tok, tok, pl.BlockSpec((tm // lc, w), lambda i: (i, 0)), tok, tok],
        out_shape=[big, big, big, big, big, big, big, jax.ShapeDtypeStruct((t // lc, w), F32), big, big],
        compiler_params=_cp(("parallel",)),
        name="rwkv_prep",
    )(cols, cols, mu_p, row2(w0), wup_p, row2(a0), aup_p, gup_p, row2(k_k), row2(k_a), row2(r_k), bd, tri)

    n_ch = 2
    ch = pl.BlockSpec((n_ch * lc, w), lambda i: (i, 0))
    rh, yh, phi, psi = pl.pallas_call(
        _rwkv_chunk_kernel,
        grid=(t // (n_ch * lc),),
        in_specs=[ch, ch, ch, ch, ch, ch, ch, pl.BlockSpec((n_ch, 1, w), lambda i: (i, 0, 0))],
        out_specs=[ch, ch, ch, ch],
        out_shape=[big, big, big, big],
        compiler_params=_cp(("parallel",)),
        name="rwkv_chunk",
    )(rt, at, bt, kt, v, bh, kh, pc.reshape(t // lc, 1, w))

    n_chunks = 4
    ts = n_chunks * lc
    nsb = seq // ts
    blk = pl.BlockSpec((ts, w), lambda b, j: (b * nsb + j, 0))
    scan = functools.partial(_rwkv_scan_kernel, n_chunks=n_chunks)
    return pl.pallas_call(
        scan,
        grid=(batch, nsb),
        in_specs=[blk, blk, blk, blk, blk, blk,
                  pl.BlockSpec((1, w), lambda b, j: (0, 0)), pl.BlockSpec((1, w), lambda b, j: (0, 0)),
                  pl.BlockSpec((w, w), lambda b, j: (0, 0))],
        out_specs=blk,
        out_shape=big,
        scratch_shapes=[pltpu.VMEM((RWKV_HEADS, HEAD_DIM, HEAD_DIM), F32)],
        compiler_params=_cp(("parallel", "arbitrary")),
        name="rwkv_scan",
    )(rh, yh, phi, psi, bonus, g, row2(gn_w), row2(gn_b), bd)


NSA_Q_HEADS = 8
NSA_KV_HEADS = 2
NSA_GROUP = 4
NSA_WIDTH = 512
NSA_KV_WIDTH = 128
NSA_COLS = NSA_WIDTH + 6 * NSA_KV_WIDTH + 3 * NSA_Q_HEADS
CMP_BLOCK = 32
CMP_STRIDE = 16
CMP_HIDDEN = 128
SLC_BLOCK = 64
SLC_TOPN = 16
WINDOW = 512
FORCED_SCORE = 1e6
NSA_TQ = 128
NSA_CB_KC, NSA_CB_VC, NSA_CB_KS, NSA_CB_VS, NSA_CB_KW, NSA_CB_VW, NSA_CB_GATE = 4, 5, 6, 7, 8, 9, 10


def _rope_tables(pos, reps):
    half = ROPE_DIM // 2
    inv = jnp.power(ROPE_THETA, -jnp.arange(half, dtype=F32) * (2.0 / ROPE_DIM))
    ang = pos.astype(F32)[:, None] * inv[None, :]
    cos = jnp.cos(ang)
    sin = jnp.sin(ang)
    p = pos.shape[0]
    ones = jnp.ones((p, HEAD_DIM - ROPE_DIM), F32)
    zeros = jnp.zeros((p, HEAD_DIM - half), F32)
    cos_t = jnp.concatenate([cos, cos, ones], axis=1)
    sin_a = jnp.concatenate([-sin, zeros], axis=1)
    sin_b = jnp.concatenate([jnp.zeros((p, half), F32), sin, jnp.zeros((p, HEAD_DIM - ROPE_DIM), F32)], axis=1)
    tile = lambda x: jnp.tile(x, (1, reps))
    return tile(cos_t), tile(sin_a), tile(sin_b)


def _rope(x, cos_t, sin_a, sin_b):
    n = x.shape[1]
    half = ROPE_DIM // 2
    return x * cos_t + pltpu.roll(x, n - half, axis=1) * sin_a + pltpu.roll(x, half, axis=1) * sin_b


def _compress_kernel(xk_ref, xv_ref, pek_ref, w1k_ref, b1k_ref, w2k_ref, pev_ref, w1v_ref, b1v_ref, w2v_ref,
                     cos_ref, sa_ref, sb_ref, ko_ref, vo_ref):
    n_half = ko_ref.shape[1]
    half = CMP_BLOCK // 2

    def mlp(x_ref, pe_ref, w1_ref, b1_ref, w2_ref):
        xs = [x_ref[pl.ds(l, n_half, stride=CMP_STRIDE), :] for l in range(half)]
        top = [_dot3(x + pe_ref[l:l + 1, :], w1_ref[l]) for l, x in enumerate(xs)]
        bot = [_dot3(x + pe_ref[half + l:half + l + 1, :], w1_ref[half + l]) for l, x in enumerate(xs)]
        hid = sum(top[1:], top[0]) + pltpu.roll(sum(bot[1:], bot[0]), n_half - 1, axis=0) + b1_ref[...]
        return _dot3(_silu(hid), w2_ref[...])

    k = mlp(xk_ref, pek_ref, w1k_ref, b1k_ref, w2k_ref)
    ko_ref[0] = _rope(k, cos_ref[...], sa_ref[...], sb_ref[...])
    vo_ref[0] = mlp(xv_ref, pev_ref, w1v_ref, b1v_ref, w2v_ref)


def nsa_compress(cols, pe_k, w1_k, b1_k, w2_k, pe_v, w1_v, b1_v, w2_v, batch, seq):
    hk, d = NSA_KV_HEADS, HEAD_DIM
    n_half = seq // CMP_STRIDE
    cmp_end = jnp.arange(n_half) * CMP_STRIDE + (CMP_BLOCK - 1)
    cos_t, sin_a, sin_b = _rope_tables(cmp_end, hk)
    eye = jnp.eye(hk, dtype=F32)
    pair = lambda w: jnp.kron(eye, w)
    w1_pair = lambda w: jax.vmap(pair)(w.reshape(CMP_BLOCK, d, CMP_HIDDEN))
    lanes2 = lambda v: jnp.tile(v, (1, hk))
    full = lambda shape: pl.BlockSpec(shape, lambda b: (0,) * len(shape))
    xspec = lambda cb: pl.BlockSpec((seq, hk * d), lambda b: (b, cb))
    ospec = pl.BlockSpec((1, n_half, hk * d), lambda b: (b, 0, 0))
    wspecs = [full((CMP_BLOCK, hk * d)), full((CMP_BLOCK, hk * d, hk * CMP_HIDDEN)), full((1, hk * CMP_HIDDEN)),
              full((hk * CMP_HIDDEN, hk * d))]
    oshape = jax.ShapeDtypeStruct((batch, n_half, hk * d), F32)
    return pl.pallas_call(
        _compress_kernel,
        grid=(batch,),
        in_specs=[xspec(NSA_CB_KC), xspec(NSA_CB_VC)] + wspecs + wspecs + [full((n_half, hk * d))] * 3,
        out_specs=[ospec, ospec],
        out_shape=[oshape, oshape],
        compiler_params=_cp(("parallel",)),
        name="nsa_compress",
    )(cols, cols, lanes2(pe_k), w1_pair(w1_k), lanes2(b1_k.reshape(1, -1)), pair(w2_k),
      lanes2(pe_v), w1_pair(w1_v), lanes2(b1_v.reshape(1, -1)), pair(w2_v), cos_t, sin_a, sin_b)


def _rope_t(x, cos_t, sin_a, sin_b):
    half = ROPE_DIM // 2
    up = jnp.concatenate([x[half:], x[:half]], axis=0)
    down = jnp.concatenate([x[-half:], x[:-half]], axis=0)
    return x * cos_t + up * sin_a + down * sin_b


def _nsa_attn_kernel(q_ref, ks_ref, vs_ref, kw_ref, vw_ref, gate_ref, kc_ref, vc_ref,
                     cosk_ref, sak_ref, sbk_ref, cosq_ref, saq_ref, sbq_ref, ovl_ref, get_ref,
                     o_ref, ks_s, vs_s, kw_s, vw_s, mem_s, *, seq):
    i = pl.program_id(1)
    tq = NSA_TQ
    d = HEAD_DIM
    g = NSA_GROUP
    cols = g * tq
    n_kv = NSA_KV_HEADS
    n_tiles = seq // tq

    @pl.when(i == 0)
    def _():
        ck, sa, sb = cosk_ref[...], sak_ref[...], sbk_ref[...]
        ks = _rope(ks_ref[...], ck, sa, sb)
        kw = _rope(kw_ref[...], ck, sa, sb)
        for hk in range(n_kv):
            hs = slice(hk * d, (hk + 1) * d)
            ks_s[hk] = ks[:, hs].astype(BF16)
            kw_s[hk] = kw[:, hs].astype(BF16)
        for j in range(n_tiles):
            rs = slice(j * tq, (j + 1) * tq)
            vs_s[j] = vs_ref[rs, :].T.astype(BF16)
            vw_s[j] = vw_ref[rs, :].T.astype(BF16)

    qt = _rope_t(q_ref[...].T, cosq_ref[...], saq_ref[...], sbq_ref[...]) * (HEAD_DIM ** -0.5)
    gexp = _dot_xr(get_ref[...], _sigmoid(gate_ref[...].T))

    pos_q = i * tq + (lax.broadcasted_iota(jnp.int32, (1, cols), 1) & (tq - 1))
    sub = lax.broadcasted_iota(jnp.int32, (tq, cols), 0)
    blk = lax.broadcasted_iota(jnp.int32, (32, tq), 0)
    cur = (i * tq + lax.broadcasted_iota(jnp.int32, (32, tq), 1)) // SLC_BLOCK
    sel_valid = blk <= cur
    sel_forced = (blk == 0) | (blk == cur) | (blk == cur - 1)

    q4 = [jnp.concatenate([qt[(hk * g + gg) * d:(hk * g + gg + 1) * d, :] for gg in range(g)], axis=1)
          for hk in range(n_kv)]
    q4b = [(x * math.log2(math.e)).astype(BF16) for x in q4]

    cend = sub * CMP_STRIDE + (CMP_BLOCK - 1)
    cmask = (cend <= pos_q) & (sub < seq // CMP_STRIDE - 1)
    o_cmp = []
    for hk in range(n_kv):
        s = _dot3(kc_ref[0, :, hk * d:(hk + 1) * d], q4[hk])
        sm = jnp.where(cmask, s, MASK_NEG)
        e = jnp.exp(sm - jnp.max(sm, axis=0, keepdims=True))
        p = e / jnp.sum(e, axis=0, keepdims=True) * cmask.astype(F32)
        o_cmp.append(_dot1(vc_ref[0, :, hk * d:(hk + 1) * d], p, 0, 0))
        psum = p[:, 0:tq] + p[:, tq:2 * tq] + p[:, 2 * tq:3 * tq] + p[:, 3 * tq:4 * tq]
        imp = _dot_xr(ovl_ref[...], psum)
        imp = jnp.where(sel_forced, FORCED_SCORE, jnp.where(sel_valid, imp, -FORCED_SCORE))
        members = []
        for jj in range(32):
            rj = imp[jj:jj + 1, :]
            beats = (imp > rj) | ((imp == rj) & (blk < jj))
            cnt = jnp.sum(beats.astype(F32), axis=0, keepdims=True)
            members.append((cnt < SLC_TOPN).astype(F32))
        member = jnp.concatenate(members + [jnp.zeros((8, tq), F32)], axis=0)
        mem_s[hk] = jnp.concatenate([member] * g, axis=1)

    def flash(k_s, vt_s, j_lo, mask_fn):
        def scores(j):
            jl = jnp.minimum(j, n_tiles - 1)
            ks = [k_s[hk, pl.ds(jl * tq, tq), :] for hk in range(n_kv)]
            return [_dg(ks[hk], q4b[hk]) for hk in range(n_kv)], vt_s[jl]

        def update(j, ss, vt, carry):
            rel = (pos_q - j * tq) - sub
            new = []
            ps = []
            for hk in range(n_kv):
                m_run, l_run, acc = carry[hk]
                s = jnp.where(mask_fn(hk, j, rel), ss[hk], MASK_NEG)
                m_new = jnp.maximum(m_run, jnp.max(s, axis=0, keepdims=True))
                alpha = jnp.exp2(m_run - m_new)
                p = jnp.exp2(s - m_new)
                l_new = alpha * l_run + jnp.sum(p, axis=0, keepdims=True)
                ps.append(p.astype(BF16))
                new.append((m_new, l_new, alpha * acc))
            pv = [_dg(vt[hk * d:(hk + 1) * d, :], ps[hk]) for hk in range(n_kv)]
            return tuple((m, l, a + o) for (m, l, a), o in zip(new, pv))

        def body(pair, carry):
            ja = 2 * pair
            ssa, vta = scores(ja)
            ssb, vtb = scores(ja + 1)
            return update(ja + 1, ssb, vtb, update(ja, ssa, vta, carry))

        one = (jnp.full((1, cols), MASK_NEG, F32), jnp.zeros((1, cols), F32), jnp.zeros((d, cols), F32))
        fin = lax.fori_loop(j_lo // 2, i // 2 + 1, body, (one,) * n_kv)
        return [acc / l_fin for _, l_fin, acc in fin]

    first_block = sub < SLC_BLOCK

    def slc_mask(hk, j, rel):
        m0 = mem_s[hk, pl.ds(2 * j, 1), :]
        m1 = mem_s[hk, pl.ds(2 * j + 1, 1), :]
        return (jnp.where(first_block, m0, m1) > 0.5) & (rel >= 0)

    def win_mask(hk, j, rel):
        return (rel >= 0) & (rel < WINDOW)

    o_slc = flash(ks_s, vs_s, 0, slc_mask)
    o_win = flash(kw_s, vw_s, jnp.maximum(i - WINDOW // tq, 0), win_mask)

    outs = []
    for hk in range(n_kv):
        for gg in range(g):
            h = hk * g + gg
            cs = slice(gg * tq, (gg + 1) * tq)
            ge = lambda br: gexp[br * NSA_WIDTH + h * d: br * NSA_WIDTH + (h + 1) * d, :]
            outs.append(ge(0) * o_cmp[hk][:, cs] + ge(1) * o_slc[hk][:, cs] + ge(2) * o_win[hk][:, cs])
    o_ref[...] = jnp.concatenate(outs, axis=0).T


def nsa_mix(cols, pe_k, w1_k, b1_k, w2_k, pe_v, w1_v, b1_v, w2_v, batch, seq):
    t = cols.shape[0]
    tq = NSA_TQ
    nq = seq // tq
    d = HEAD_DIM
    k_cmp, v_cmp = nsa_compress(cols, pe_k, w1_k, b1_k, w2_k, pe_v, w1_v, b1_v, w2_v, batch, seq)
    pos = jnp.arange(seq)
    cos_k, sa_k, sb_k = _rope_tables(pos, 2)
    cos_q, sa_q, sb_q = (x.T for x in _rope_tables(pos, NSA_Q_HEADS))
    n_slc = seq // SLC_BLOCK
    n_half = seq // CMP_STRIDE
    blk_tok = np.arange(n_half - 1)[:, None] * CMP_STRIDE + np.arange(CMP_BLOCK)[None, :]
    overlap = np.any((blk_tok[:, :, None] // SLC_BLOCK) == np.arange(n_slc)[None, None, :], axis=1)
    ovl_t = np.zeros((32, 128), np.float32)
    ovl_t[:n_slc, :n_half - 1] = overlap.T
    gate_e = np.zeros((3 * NSA_WIDTH, 128), np.float32)
    for h in range(NSA_Q_HEADS):
        for br in range(3):
            gate_e[br * NSA_WIDTH + h * d: br * NSA_WIDTH + (h + 1) * d, h * 3 + br] = 1.0
    kvspec = lambda cb: pl.BlockSpec((seq, 128), lambda b, i: (b, cb))
    full = lambda shape: pl.BlockSpec(shape, lambda b, i: (0,) * len(shape))
    cspec = pl.BlockSpec((1, n_half, NSA_KV_WIDTH), lambda b, i: (b, 0, 0))
    kern = functools.partial(_nsa_attn_kernel, seq=seq)
    return pl.pallas_call(
        kern,
        grid=(batch, nq),
        in_specs=[
            pl.BlockSpec((tq, NSA_WIDTH), lambda b, i: (b * nq + i, 0)),
            kvspec(NSA_CB_KS), kvspec(NSA_CB_VS), kvspec(NSA_CB_KW), kvspec(NSA_CB_VW),
            pl.BlockSpec((tq, 128), lambda b, i: (b * nq + i, NSA_CB_GATE)),
            cspec, cspec,
            full((seq, 128)), full((seq, 128)), full((seq, 128)),
            pl.BlockSpec((NSA_WIDTH, tq), lambda b, i: (0, i)),
            pl.BlockSpec((NSA_WIDTH, tq), lambda b, i: (0, i)),
            pl.BlockSpec((NSA_WIDTH, tq), lambda b, i: (0, i)),
            full((32, 128)), full((3 * NSA_WIDTH, 128)),
        ],
        out_specs=pl.BlockSpec((tq, NSA_WIDTH), lambda b, i: (b * nq + i, 0)),
        out_shape=jax.ShapeDtypeStruct((t, NSA_WIDTH), F32),
        scratch_shapes=[
            pltpu.VMEM((NSA_KV_HEADS, seq, d), BF16),
            pltpu.VMEM((nq, NSA_KV_WIDTH, tq), BF16),
            pltpu.VMEM((NSA_KV_HEADS, seq, d), BF16),
            pltpu.VMEM((nq, NSA_KV_WIDTH, tq), BF16),
            pltpu.VMEM((NSA_KV_HEADS, 40, NSA_GROUP * tq), F32),
        ],
        compiler_params=_cp(("parallel", "arbitrary")),
        name="nsa_attn",
    )(cols, cols, cols, cols, cols, cols, k_cmp, v_cmp, cos_k, sa_k, sb_k, cos_q, sa_q, sb_q,
      jnp.asarray(ovl_t, BF16), jnp.asarray(gate_e, BF16))


HYB_NSA_PAD = 1792
HYB_COLS_PAD = HYB_NSA_PAD + RWKV_COLS_PAD


def _mm2_res_kernel(a1_ref, a2_ref, w1_ref, w2_ref, r_ref, o_ref):
    acc = jnp.dot(a1_ref[...].astype(BF16), w1_ref[...], preferred_element_type=F32)
    acc = acc + jnp.dot(a2_ref[...].astype(BF16), w2_ref[...], preferred_element_type=F32)
    o_ref[...] = r_ref[...] + acc


def matmul2_residual(a1, a2, w_bf16, res, tm, tn):
    t, k1 = a1.shape
    k2 = a2.shape[1]
    n = w_bf16.shape[1]
    return pl.pallas_call(
        _mm2_res_kernel,
        grid=(t // tm, n // tn),
        in_specs=[
            pl.BlockSpec((tm, k1), lambda i, j: (i, 0)),
            pl.BlockSpec((tm, k2), lambda i, j: (i, 0)),
            pl.BlockSpec((k1, tn), lambda i, j: (0, j)),
            pl.BlockSpec((k2, tn), lambda i, j: (k1 // k2, j)),
            pl.BlockSpec((tm, tn), lambda i, j: (i, j)),
        ],
        out_specs=pl.BlockSpec((tm, tn), lambda i, j: (i, j)),
        out_shape=jax.ShapeDtypeStruct((t, n), F32),
        compiler_params=_cp(("parallel", "arbitrary")),
        name="matmul2_residual",
    )(a1, a2, w_bf16, w_bf16, res)


def hybrid_block(h, g_norm, w_in, w_out, rwkv_params, nsa_params, batch, seq):
    w_rwkv = w_in[:, :RWKV_COLS]
    w_nsa = w_in[:, RWKV_COLS:]
    w_p = jnp.concatenate([
        jnp.pad(w_nsa, ((0, 0), (0, HYB_NSA_PAD - NSA_COLS))),
        jnp.pad(w_rwkv, ((0, 0), (0, RWKV_COLS_PAD - RWKV_COLS))),
    ], axis=1).astype(BF16)
    cols = norm_matmul(h, g_norm, w_p, tm=512, tn=896)
    y_rwkv = rwkv7_mix(cols, *rwkv_params, batch=batch, seq=seq, col_block=1)
    y_nsa = nsa_mix(cols, *nsa_params, batch=batch, seq=seq)
    return matmul2_residual(y_rwkv, y_nsa, w_out.astype(BF16), h, tm=1024, tn=512)


def kernel(x, norm_mix, norm_ffn, norm_final, hyb_w_in, hyb_w_out, rwkv_mu, rwkv_w0, rwkv_w_up, rwkv_a0, rwkv_a_up, rwkv_g_up, rwkv_k_k, rwkv_k_a, rwkv_r_k, rwkv_gn_w, rwkv_gn_b, nsa_pe_k, nsa_w1_k, nsa_b1_k, nsa_w2_k, nsa_pe_v, nsa_w1_v, nsa_b1_v, nsa_w2_v, ssm_w_in, ssm_conv_w, ssm_conv_b, ssm_dt_bias, ssm_a_log, ssm_d, ssm_norm_w, ssm_w_out, moe_wg, moe_bg, moe_we, moe_be, moe_w_gate, moe_w_up, moe_w_down):
    batch, seq, d = x.shape
    depth = norm_mix.shape[0]
    h = x.reshape(batch * seq, d)
    for layer in range(depth):
        i = layer // 2
        if layer % 2 == 0:
            rwkv_params = (rwkv_mu[i], rwkv_w0[i], rwkv_w_up[i], rwkv_a0[i], rwkv_a_up[i], rwkv_g_up[i],
                           rwkv_k_k[i], rwkv_k_a[i], rwkv_r_k[i].reshape(-1), rwkv_gn_w[i], rwkv_gn_b[i])
            nsa_params = (nsa_pe_k[i], nsa_w1_k[i], nsa_b1_k[i], nsa_w2_k[i],
                          nsa_pe_v[i], nsa_w1_v[i], nsa_b1_v[i], nsa_w2_v[i])
            h = hybrid_block(h, norm_mix[layer], hyb_w_in[i], hyb_w_out[i], rwkv_params, nsa_params, batch, seq)
        else:
            h = mamba2_block(h, norm_mix[layer], ssm_w_in[i], ssm_conv_w[i], ssm_conv_b[i], ssm_dt_bias[i],
                             ssm_a_log[i], ssm_d[i], ssm_norm_w[i], ssm_w_out[i], batch, seq)
        h = hier_moe_block(h, norm_ffn[layer], moe_wg[layer], moe_bg[layer], moe_we[layer], moe_be[layer],
                           moe_w_gate, moe_w_up, moe_w_down, layer,
                           g_final=norm_final if layer == depth - 1 else None)
    return h.reshape(batch, seq, d)
```

```python
import functools
import math

import numpy as np
import jax
import jax.numpy as jnp
from jax import lax
from jax.experimental import pallas as pl
from jax.experimental.pallas import tpu as pltpu

F32 = jnp.float32
BF16 = jnp.bfloat16

D_MODEL = 1024
NORM_EPS = 1e-6
HEAD_DIM = 64
ROPE_DIM = 16
ROPE_THETA = 500000.0
MASK_NEG = -1e30

VMEM_LIMIT = 48 * 1024 * 1024


def _cp(sem, vmem=VMEM_LIMIT):
    return pltpu.CompilerParams(dimension_semantics=sem, vmem_limit_bytes=vmem)


def _dg(a, b, ca=1, cb=0):
    return lax.dot_general(a, b, (((ca,), (cb,)), ((), ())), preferred_element_type=F32)


def _split2(x):
    hi = x.astype(BF16)
    lo = (x - hi.astype(F32)).astype(BF16)
    return hi, lo


def _split3(x):
    hi = x.astype(BF16)
    r = x - hi.astype(F32)
    mid = r.astype(BF16)
    lo = (r - mid.astype(F32)).astype(BF16)
    return hi, mid, lo


def _dot1(a, b, ca=1, cb=0):
    return _dg(a.astype(BF16), b.astype(BF16), ca, cb)


def _dot3(a, b, ca=1, cb=0):
    ah, al = _split2(a)
    bh, bl = _split2(b)
    return _dg(ah, bh, ca, cb) + (_dg(ah, bl, ca, cb) + _dg(al, bh, ca, cb))


def _dot_xl(a, b01, ca=1, cb=0):
    h, m, l = _split3(a)
    return _dg(h, b01, ca, cb) + (_dg(m, b01, ca, cb) + _dg(l, b01, ca, cb))


def _dot_xr(a01, b, ca=1, cb=0):
    h, m, l = _split3(b)
    return _dg(a01, h, ca, cb) + (_dg(a01, m, ca, cb) + _dg(a01, l, ca, cb))


def _sigmoid(x):
    return 1.0 / (1.0 + jnp.exp(-x))


def _silu(x):
    return x * _sigmoid(x)


def _softplus(x):
    return jnp.maximum(x, 0.0) + jnp.log(1.0 + jnp.exp(-jnp.abs(x)))


def _norm_mm_kernel(x_ref, g_ref, w_ref, o_ref, *, tn):
    x = x_ref[...]
    ms = jnp.mean(x * x, axis=-1, keepdims=True)
    xn = (x * lax.rsqrt(ms + NORM_EPS) * g_ref[...]).astype(BF16)
    for j in range(w_ref.shape[1] // tn):
        cs = slice(j * tn, (j + 1) * tn)
        o_ref[:, cs] = jnp.dot(xn, w_ref[:, cs], preferred_element_type=F32)


def norm_matmul(x, g, w_bf16, tm, tn):
    t, d = x.shape
    n = w_bf16.shape[1]
    return pl.pallas_call(
        functools.partial(_norm_mm_kernel, tn=tn),
        grid=(t // tm,),
        in_specs=[
            pl.BlockSpec((tm, d), lambda i: (i, 0)),
            pl.BlockSpec((1, d), lambda i: (0, 0)),
            pl.BlockSpec((d, n), lambda i: (0, 0), pipeline_mode=pl.Buffered(1)),
        ],
        out_specs=pl.BlockSpec((tm, n), lambda i: (i, 0)),
        out_shape=jax.ShapeDtypeStruct((t, n), F32),
        compiler_params=_cp(("parallel",)),
        name="norm_matmul",
    )(x, g.reshape(1, d), w_bf16)


def _mm_res_kernel(a_ref, w_ref, r_ref, o_ref):
    o_ref[...] = r_ref[...] + jnp.dot(a_ref[...].astype(BF16), w_ref[...], preferred_element_type=F32)


def matmul_residual(a, w_bf16, res, tm, tn):
    t, k = a.shape
    n = w_bf16.shape[1]
    return pl.pallas_call(
        _mm_res_kernel,
        grid=(t // tm, n // tn),
        in_specs=[
            pl.BlockSpec((tm, k), lambda i, j: (i, 0)),
            pl.BlockSpec((k, tn), lambda i, j: (0, j)),
            pl.BlockSpec((tm, tn), lambda i, j: (i, j)),
        ],
        out_specs=pl.BlockSpec((tm, tn), lambda i, j: (i, j)),
        out_shape=jax.ShapeDtypeStruct((t, n), F32),
        compiler_params=_cp(("parallel", "arbitrary")),
        name="matmul_residual",
    )(a, w_bf16, res)


MOE_GROUPS = 4
MOE_EPG = 8
MOE_EXPERTS = 32
MOE_HIDDEN = 512
MOE_TM = 256
MOE_ROUTE_TM = 512
MOE_DISPATCH_TM = 256
MOE_COMBINE_TM = 128


def _router_kernel(x_ref, g_ref, w_ref, b_ref, tri_ref, ids_ref, wts_ref, rank_ref, cnt_ref, run_ref):
    @pl.when(pl.program_id(0) == 0)
    def _():
        run_ref[...] = jnp.zeros_like(run_ref)

    x = x_ref[...]
    ms = jnp.mean(x * x, axis=-1, keepdims=True)
    u = x * lax.rsqrt(ms + NORM_EPS) * g_ref[...]
    lg = _dot3(u, w_ref[...]) + b_ref[...]
    tm = x.shape[0]
    lane = lax.broadcasted_iota(jnp.int32, (tm, 128), 1)
    gl = lg[:, :128]
    el = lg[:, 128:]
    gmask = lane < MOE_GROUPS
    gmax = jnp.max(jnp.where(gmask, gl, -jnp.inf), axis=-1, keepdims=True)
    ge = jnp.where(gmask, jnp.exp(gl - gmax), 0.0)
    pg = ge / jnp.sum(ge, axis=-1, keepdims=True)
    g_val = jnp.max(pg, axis=-1, keepdims=True)
    g_idx = jnp.min(jnp.where(gmask & (pg == g_val), lane, 128), axis=-1, keepdims=True)
    emask = (lane >= g_idx * MOE_EPG) & (lane < (g_idx + 1) * MOE_EPG)
    emax = jnp.max(jnp.where(emask, el, -jnp.inf), axis=-1, keepdims=True)
    ee = jnp.where(emask, jnp.exp(el - emax), 0.0)
    pe = ee / jnp.sum(ee, axis=-1, keepdims=True)
    v1 = jnp.max(jnp.where(emask, pe, -1.0), axis=-1, keepdims=True)
    i1 = jnp.min(jnp.where(emask & (pe == v1), lane, 128), axis=-1, keepdims=True)
    m2 = emask & (lane != i1)
    v2 = jnp.max(jnp.where(m2, pe, -1.0), axis=-1, keepdims=True)
    i2 = jnp.min(jnp.where(m2 & (pe == v2), lane, 128), axis=-1, keepdims=True)
    den = v1 + v2
    w1 = v1 / den * g_val
    w2 = v2 / den * g_val
    ids_ref[...] = jnp.where(lane == 0, i1, jnp.where(lane == 1, i2, 0))
    wts_ref[...] = jnp.where(lane == 0, w1, jnp.where(lane == 1, w2, 0.0))
    oh1 = (lane == i1).astype(F32)
    oh2 = (lane == i2).astype(F32)
    oh = oh1 + oh2
    base = run_ref[...] + _dg(tri_ref[...], oh.astype(BF16))
    r1 = jnp.sum(oh1 * base, axis=-1, keepdims=True)
    r2 = jnp.sum(oh2 * base, axis=-1, keepdims=True)
    rank_ref[...] = jnp.where(lane == 0, r1, jnp.where(lane == 1, r2, 0.0)).astype(jnp.int32)
    total = run_ref[...] + jnp.sum(oh, axis=0, keepdims=True)
    run_ref[...] = total
    cnt_ref[...] = total.astype(jnp.int32)


def moe_router(h, g, wg, bg, we, be):
    t, d = h.shape
    tm = MOE_ROUTE_TM
    w = jnp.zeros((d, 256), F32).at[:, :MOE_GROUPS].set(wg).at[:, 128:128 + MOE_EXPERTS].set(we)
    b = jnp.zeros((1, 256), F32).at[0, :MOE_GROUPS].set(bg).at[0, 128:128 + MOE_EXPERTS].set(be)
    tri = jnp.asarray(np.tril(np.ones((tm, tm)), -1), BF16)
    tok = pl.BlockSpec((tm, 128), lambda i: (i, 0))
    return pl.pallas_call(
        _router_kernel,
        grid=(t // tm,),
        in_specs=[
            pl.BlockSpec((tm, d), lambda i: (i, 0)),
            pl.BlockSpec((1, d), lambda i: (0, 0)),
            pl.BlockSpec((d, 256), lambda i: (0, 0)),
            pl.BlockSpec((1, 256), lambda i: (0, 0)),
            pl.BlockSpec((tm, tm), lambda i: (0, 0)),
        ],
        out_specs=[tok, tok, tok, pl.BlockSpec((1, 128), lambda i: (0, 0))],
        out_shape=[
            jax.ShapeDtypeStruct((t, 128), jnp.int32),
            jax.ShapeDtypeStruct((t, 128), F32),
            jax.ShapeDtypeStruct((t, 128), jnp.int32),
            jax.ShapeDtypeStruct((1, 128), jnp.int32),
        ],
        scratch_shapes=[pltpu.VMEM((1, 128), F32)],
        compiler_params=_cp(("arbitrary",)),
        name="moe_router",
    )(h, g.reshape(1, d), w, b, tri)


def _dispatch_kernel(dest_ref, pend_ref, h_ref, xs_ref, zbuf, sem, zsem):
    n_tok = h_ref.shape[0]
    tile = zbuf.shape[0]

    @pl.when(pl.program_id(0) == 0)
    def _():
        zbuf[...] = jnp.zeros_like(zbuf)

        def zero_copy(row0):
            return pltpu.make_async_copy(zbuf, xs_ref.at[pl.ds(pl.multiple_of(row0, tile), tile)], zsem)

        def has_rows(e):
            return pend_ref[e] > (pend_ref[e - 1] if e > 0 else 0)

        for e in range(MOE_EXPERTS):
            @pl.when(has_rows(e))
            def _(e=e):
                zero_copy(pend_ref[e] - tile).start()

        used = pend_ref[MOE_EXPERTS - 1] // tile
        n_tiles = xs_ref.shape[0] // tile

        def start_unused(k, carry):
            zero_copy(k * tile).start()
            return carry

        def wait_unused(k, carry):
            zero_copy(k * tile).wait()
            return carry

        lax.fori_loop(used, n_tiles, start_unused, 0)
        for e in range(MOE_EXPERTS):
            @pl.when(has_rows(e))
            def _(e=e):
                zero_copy(pend_ref[e] - tile).wait()
        lax.fori_loop(used, n_tiles, wait_unused, 0)

    def row_copy(r, a):
        return pltpu.make_async_copy(h_ref.at[pl.ds(r, 1)], xs_ref.at[pl.ds(dest_ref[0, 0, a], 1)], sem)

    def start(r, carry):
        row_copy(r, r).start()
        row_copy(r, n_tok + r).start()
        return carry

    def wait(r, carry):
        row_copy(r, r).wait()
        row_copy(r, n_tok + r).wait()
        return carry

    lax.fori_loop(0, n_tok, start, 0, unroll=8)
    lax.fori_loop(0, n_tok, wait, 0, unroll=8)


def _dest_tiles(dest, tm):
    n = dest.shape[1] // tm
    return dest.reshape(2, n, tm).transpose(1, 0, 2).reshape(n, 1, 2 * tm)


def moe_dispatch(h, dest, pad_end, np_rows):
    t, d = h.shape
    tm = MOE_DISPATCH_TM
    return pl.pallas_call(
        _dispatch_kernel,
        grid=(t // tm,),
        in_specs=[
            pl.BlockSpec((1, 1, 2 * tm), lambda i: (i, 0, 0), memory_space=pltpu.SMEM),
            pl.BlockSpec(memory_space=pltpu.SMEM),
            pl.BlockSpec((tm, d), lambda i: (i, 0)),
        ],
        out_specs=pl.BlockSpec(memory_space=pl.ANY),
        out_shape=jax.ShapeDtypeStruct((np_rows, d), F32),
        scratch_shapes=[pltpu.VMEM((MOE_TM, d), F32), pltpu.SemaphoreType.DMA(()), pltpu.SemaphoreType.DMA(())],
        compiler_params=_cp(("arbitrary",)),
        name="moe_dispatch",
    )(_dest_tiles(dest, tm), pad_end, h)


def _moe_ffn_kernel(te_ref, tv_ref, x_ref, g_ref, wg_ref, wu_ref, wd_ref, o_ref, wg_s, wu_s, wd_s):
    i = pl.program_id(0)
    prev = te_ref[jnp.maximum(i - 1, 0)]
    changed = (i == 0) | (te_ref[i] != prev)

    @pl.when(changed)
    def _():
        wg_s[...] = wg_ref[0, 0].astype(BF16)
        wu_s[...] = wu_ref[0, 0].astype(BF16)
        wd_s[...] = wd_ref[0, 0].astype(BF16)

    @pl.when(tv_ref[i] > 0)
    def _():
        x = x_ref[...]
        ms = jnp.mean(x * x, axis=-1, keepdims=True)
        x = (x * lax.rsqrt(ms + NORM_EPS) * g_ref[...]).astype(BF16)
        gate = jnp.dot(x, wg_s[...], preferred_element_type=F32)
        up = jnp.dot(x, wu_s[...], preferred_element_type=F32)
        act = _silu(gate) * up
        o_ref[...] = jnp.dot(act.astype(BF16), wd_s[...], preferred_element_type=F32)

    @pl.when(tv_ref[i] == 0)
    def _():
        o_ref[...] = jnp.zeros_like(o_ref)


def moe_ffn(xs, g, tile_expert, tile_valid, w_gate, w_up, w_down, layer):
    np_rows, d = xs.shape
    f = w_gate.shape[-1]
    tm = MOE_TM
    n_tiles = np_rows // tm
    grid_spec = pltpu.PrefetchScalarGridSpec(
        num_scalar_prefetch=2,
        grid=(n_tiles,),
        in_specs=[
            pl.BlockSpec((tm, d), lambda i, te, tv: (i, 0)),
            pl.BlockSpec((1, d), lambda i, te, tv: (0, 0)),
            pl.BlockSpec((1, 1, d, f), lambda i, te, tv: (layer, te[i], 0, 0)),
            pl.BlockSpec((1, 1, d, f), lambda i, te, tv: (layer, te[i], 0, 0)),
            pl.BlockSpec((1, 1, f, d), lambda i, te, tv: (layer, te[i], 0, 0)),
        ],
        out_specs=pl.BlockSpec((tm, d), lambda i, te, tv: (i, 0)),
        scratch_shapes=[pltpu.VMEM((d, f), BF16), pltpu.VMEM((d, f), BF16), pltpu.VMEM((f, d), BF16)],
    )
    return pl.pallas_call(
        _moe_ffn_kernel,
        grid_spec=grid_spec,
        out_shape=jax.ShapeDtypeStruct((np_rows, d), F32),
        compiler_params=_cp(("arbitrary",)),
        name="moe_ffn",
    )(tile_expert, tile_valid, xs, g.reshape(1, d), w_gate, w_up, w_down)


def _combine_kernel(dcur_ref, dnxt_ref, h_ref, w_ref, g_ref, y_ref, o_ref, ybuf, sems, *, final_norm):
    i = pl.program_id(0)
    n = pl.num_programs(0)
    tm = h_ref.shape[0]
    slot = lax.rem(i, 2)

    def row_copy(dref, a, s):
        return pltpu.make_async_copy(y_ref.at[pl.ds(dref[0, 0, a], 1)], ybuf.at[s, pl.ds(a, 1)], sems.at[s])

    def fetch(dref, s):
        def start(a, carry):
            row_copy(dref, a, s).start()
            return carry
        lax.fori_loop(0, 2 * tm, start, 0, unroll=8)

    @pl.when(i == 0)
    def _():
        fetch(dcur_ref, 0)

    @pl.when(i + 1 < n)
    def _():
        fetch(dnxt_ref, 1 - slot)

    def wait(a, carry):
        row_copy(dcur_ref, a, slot).wait()
        return carry

    lax.fori_loop(0, 2 * tm, wait, 0, unroll=8)
    w = w_ref[...]
    x = h_ref[...] + w[:, 0:1] * ybuf[slot, 0:tm, :] + w[:, 1:2] * ybuf[slot, tm:2 * tm, :]
    if final_norm:
        ms = jnp.mean(x * x, axis=-1, keepdims=True)
        x = x * lax.rsqrt(ms + NORM_EPS) * g_ref[...]
    o_ref[...] = x


def moe_combine(h, y, dest, wts, g_final=None):
    t, d = h.shape
    tm = MOE_COMBINE_TM
    n = t // tm
    dtile = _dest_tiles(dest, tm)
    g = jnp.ones((1, d), F32) if g_final is None else g_final.reshape(1, d)
    kern = functools.partial(_combine_kernel, final_norm=g_final is not None)
    return pl.pallas_call(
        kern,
        grid=(n,),
        in_specs=[
            pl.BlockSpec((1, 1, 2 * tm), lambda i: (i, 0, 0), memory_space=pltpu.SMEM),
            pl.BlockSpec((1, 1, 2 * tm), lambda i: (jnp.minimum(i + 1, n - 1), 0, 0), memory_space=pltpu.SMEM),
            pl.BlockSpec((tm, d), lambda i: (i, 0)),
            pl.BlockSpec((tm, 128), lambda i: (i, 0)),
            pl.BlockSpec((1, d), lambda i: (0, 0)),
            pl.BlockSpec(memory_space=pl.ANY),
        ],
        out_specs=pl.BlockSpec((tm, d), lambda i: (i, 0)),
        out_shape=jax.ShapeDtypeStruct((t, d), F32),
        scratch_shapes=[pltpu.VMEM((2, 2 * tm, d), F32), pltpu.SemaphoreType.DMA((2,))],
        compiler_params=_cp(("arbitrary",)),
        name="moe_combine",
    )(dtile, dtile, h, wts, g, y)


def hier_moe_block(h, g, wg, bg, we, be, w_gate, w_up, w_down, layer, g_final=None):
    t, d = h.shape
    tm = MOE_TM
    ids, wts, rank, counts = moe_router(h, g, wg, bg, we, be)
    counts = counts[0, :MOE_EXPERTS]
    padded = ((counts + tm - 1) // tm) * tm
    pad_end = jnp.cumsum(padded)
    pad_off = pad_end - padded
    dest = jnp.take(pad_off, ids[:, :2].T) + rank[:, :2].T
    np_rows = 2 * t + MOE_EXPERTS * tm
    tile_start = jnp.arange(np_rows // tm, dtype=jnp.int32) * tm
    tile_expert = jnp.minimum(jnp.sum((tile_start[:, None] >= pad_end[None, :]).astype(jnp.int32), axis=1),
                              MOE_EXPERTS - 1)
    tile_valid = (tile_start < pad_end[-1]).astype(jnp.int32)
    xs = moe_dispatch(h, dest, pad_end.astype(jnp.int32), np_rows)
    y = moe_ffn(xs, g, tile_expert, tile_valid, w_gate, w_up, w_down, layer)
    return moe_combine(h, y, dest, wts, g_final)


SSM_D_INNER = 2048
SSM_HEADS = 32
SSM_GROUPS = 4
SSM_HPG = 8
SSM_STATE = 128
SSM_CONV = 4
SSM_CHUNK = 256
SSM_NORM_EPS = 1e-5
SSM_CONV_CH = SSM_D_INNER + 2 * SSM_GROUPS * SSM_STATE
SSM_IN_COLS = SSM_D_INNER + SSM_CONV_CH + SSM_HEADS
SSM_IN_PAD = 5376
SSM_GW = SSM_HPG * HEAD_DIM
SSD_X0 = SSM_D_INNER


def _ssd_kernel(x_ref, b_ref, c_ref, z_ref, dt_ref, cwx_ref, cwb_ref, cwc_ref, cbx_ref, cbb_ref, cbc_ref,
                dtb_ref, alog_ref, dsk_ref, nw_ref, sel_ref, e8_ref, tri_ref, o_ref, h_ref, tail_ref):
    c = pl.program_id(2)
    lc = SSM_CHUNK

    @pl.when(c == 0)
    def _():
        h_ref[...] = jnp.zeros_like(h_ref)
        tail_ref[...] = jnp.zeros_like(tail_ref)

    def conv_silu(raw_ref, w_ref, bias_ref, lo):
        raw = raw_ref[...]
        hi = lo + raw.shape[1]
        xx = jnp.concatenate([tail_ref[:, lo:hi], raw], axis=0)
        w = w_ref[...]
        acc = bias_ref[...] + w[SSM_CONV - 1:SSM_CONV, :] * raw
        for k in range(1, SSM_CONV):
            acc = acc + w[SSM_CONV - 1 - k:SSM_CONV - k, :] * xx[8 - k:8 - k + lc, :]
        tail_ref[:, lo:hi] = raw[lc - 8:lc, :]
        return _silu(acc)

    x = conv_silu(x_ref, cwx_ref, cbx_ref, 0)
    bm = conv_silu(b_ref, cwb_ref, cbb_ref, SSM_GW)
    cm = conv_silu(c_ref, cwc_ref, cbc_ref, SSM_GW + SSM_STATE)

    sel = sel_ref[0]
    e8 = e8_ref[...]
    tri = tri_ref[...]
    dt_all = _softplus(dt_ref[...] + dtb_ref[...])
    a_all = dt_all * (-jnp.exp(alog_ref[...]))
    dt = _dot_xl(dt_all, sel)
    cs = _dot_xl(_dot_xr(tri, a_all), sel)
    cst = cs.T
    cs_last = cs[lc - 1:lc, :]

    xdt = x * _dot_xl(dt, e8)
    cb = _dot1(cm, bm, 1, 1)
    row = lax.broadcasted_iota(jnp.int32, (lc, lc), 0)
    col = lax.broadcasted_iota(jnp.int32, (lc, lc), 1)
    causal = row >= col
    ys = []
    for hh in range(SSM_HPG):
        lmat = jnp.where(causal, jnp.exp(cs[:, hh:hh + 1] - cst[hh:hh + 1, :]), 0.0)
        ys.append(_dot1(cb * lmat, xdt[:, hh * HEAD_DIM:(hh + 1) * HEAD_DIM]))
    y = jnp.concatenate(ys, axis=1)

    h_in = h_ref[...]
    y = y + _dot1(cm, h_in) * _dot_xl(jnp.exp(cs), e8)
    xdec = xdt * _dot_xl(jnp.exp(cs_last - cs), e8)
    h_ref[...] = h_in * _dot_xl(jnp.exp(cs_last), e8) + _dot1(bm, xdec, 0, 0)

    y = y + x * dsk_ref[...]
    zz = z_ref[...]
    y = y * _silu(zz)
    y = y * lax.rsqrt(jnp.mean(y * y, axis=-1, keepdims=True) + SSM_NORM_EPS) * nw_ref[...]
    o_ref[...] = y.astype(BF16)


def ssd_mix(zx, conv_w, conv_b, dt_bias, a_log, d_skip, norm_w, batch, seq):
    t = zx.shape[0]
    lc = SSM_CHUNK
    nc = seq // lc
    g, hp = SSM_GROUPS, SSM_HPG
    lanes = 128
    pad_row = lambda v: jnp.pad(v, (0, lanes - v.shape[0])).reshape(1, lanes)
    sel = np.zeros((g, lanes, lanes), np.float32)
    for gg in range(g):
        sel[gg, gg * hp + np.arange(hp), np.arange(hp)] = 1.0
    e8 = np.zeros((lanes, SSM_GW), np.float32)
    e8[:hp] = np.kron(np.eye(hp), np.ones((1, HEAD_DIM)))
    tri = jnp.asarray(np.tril(np.ones((lc, lc))), BF16)
    d_exp = jnp.repeat(d_skip, HEAD_DIM).reshape(1, SSM_D_INNER)
    xb = SSD_X0 // SSM_GW
    bb = (SSD_X0 + SSM_D_INNER) // SSM_STATE
    cb = bb + g
    wb = SSM_D_INNER // SSM_STATE
    row = lambda w, f: pl.BlockSpec((lc, w), lambda b, gg, c: (b * nc + c, f(gg)))
    par = lambda r, w, f: pl.BlockSpec((r, w), lambda b, gg, c: (0, f(gg)))
    const = lambda shape: pl.BlockSpec(shape, lambda b, gg, c: (0,) * len(shape))
    cb2 = conv_b.reshape(1, -1)
    return pl.pallas_call(
        _ssd_kernel,
        grid=(batch, g, nc),
        in_specs=[
            row(SSM_GW, lambda gg: xb + gg),
            row(SSM_STATE, lambda gg: bb + gg),
            row(SSM_STATE, lambda gg: cb + gg),
            row(SSM_GW, lambda gg: gg),
            row(lanes, lambda gg: (SSD_X0 + SSM_CONV_CH) // lanes),
            par(SSM_CONV, SSM_GW, lambda gg: gg),
            par(SSM_CONV, SSM_STATE, lambda gg: wb + gg),
            par(SSM_CONV, SSM_STATE, lambda gg: wb + g + gg),
            par(1, SSM_GW, lambda gg: gg),
            par(1, SSM_STATE, lambda gg: wb + gg),
            par(1, SSM_STATE, lambda gg: wb + g + gg),
            const((1, lanes)), const((1, lanes)),
            par(1, SSM_GW, lambda gg: gg),
            par(1, SSM_GW, lambda gg: gg),
            pl.BlockSpec((1, lanes, lanes), lambda b, gg, c: (gg, 0, 0)),
            const((lanes, SSM_GW)), const((lc, lc)),
        ],
        out_specs=pl.BlockSpec((lc, SSM_GW), lambda b, gg, c: (b * nc + c, gg)),
        out_shape=jax.ShapeDtypeStruct((t, SSM_D_INNER), BF16),
        scratch_shapes=[pltpu.VMEM((SSM_STATE, SSM_GW), F32), pltpu.VMEM((8, SSM_GW + 2 * SSM_STATE), F32)],
        compiler_params=_cp(("parallel", "parallel", "arbitrary")),
        name="ssd",
    )(zx, zx, zx, zx, zx, conv_w, conv_w, conv_w, cb2, cb2, cb2, pad_row(dt_bias), pad_row(a_log),
      d_exp, norm_w.reshape(1, -1), jnp.asarray(sel, BF16), jnp.asarray(e8, BF16), tri)


def mamba2_block(h, g_norm, w_in, conv_w, conv_b, dt_bias, a_log, d_skip, norm_w, w_out, batch, seq):
    w_in_p = jnp.pad(w_in, ((0, 0), (0, SSM_IN_PAD - SSM_IN_COLS))).astype(BF16)
    zx = norm_matmul(h, g_norm, w_in_p, tm=512, tn=768)
    y = ssd_mix(zx, conv_w, conv_b, dt_bias, a_log, d_skip, norm_w, batch, seq)
    return matmul_residual(y, w_out.astype(BF16), h, tm=1024, tn=512)


RWKV_HEADS = 8
RWKV_WIDTH = 512
RWKV_DECAY_RANK = 32
RWKV_A_RANK = 32
RWKV_GATE_RANK = 96
RWKV_GN_EPS = 64e-5
RWKV_COLS = 3 * RWKV_WIDTH + RWKV_DECAY_RANK + RWKV_A_RANK + RWKV_GATE_RANK
RWKV_COLS_PAD = 1792
RWKV_LO = 3 * RWKV_WIDTH
RWKV_LO_W = RWKV_COLS_PAD - RWKV_LO
RWKV_CHUNK = 64


def _rwkv_prep_kernel(c_ref, p_ref, mu_ref, w0_ref, wup_ref, a0_ref, aup_ref, gup_ref, kk_ref, ka_ref, rk_ref,
                      bd_ref, tri_ref,
                      rt_ref, at_ref, bt_ref, kt_ref, v_ref, bh_ref, kh_ref, pc_ref, g_ref, bonus_ref, *, seq, tm):
    i = pl.program_id(0)
    w = RWKV_WIDTH
    c = c_ref[...]
    first = (i * tm) % seq == 0
    prev_row = jnp.where(first, 0.0, p_ref[7:8, :])
    row = lax.broadcasted_iota(jnp.int32, c.shape, 0)
    shifted = jnp.where(row == 0, prev_row, pltpu.roll(c, 1, axis=0))
    z = c + (shifted - c) * mu_ref[...]
    r = z[:, 0:w]
    k = z[:, w:2 * w]
    v = z[:, 2 * w:3 * w]
    lo = z[:, RWKV_LO:RWKV_COLS_PAD]
    w_log = -_softplus(-(w0_ref[...] + _dot3(jnp.tanh(lo), wup_ref[...]))) - 0.5
    lw = -jnp.exp(w_log)
    a = _sigmoid(a0_ref[...] + _dot3(lo, aup_ref[...]))
    g_ref[...] = _dot3(_sigmoid(lo), gup_ref[...])
    bd = bd_ref[...]
    kk = k * kk_ref[...]
    kk = kk * lax.rsqrt(jnp.maximum(_dot_xl(kk * kk, bd), 1e-24))
    k = k * (1.0 + (a - 1.0) * ka_ref[...])
    bonus_ref[...] = _dot_xl(r * k * rk_ref[...], bd) * v
    v_ref[...] = v
    av = -kk
    bv = kk * a
    tri = tri_ref[...]
    lc = RWKV_CHUNK
    pcs = []
    for cc in range(tm // lc):
        sl = slice(cc * lc, (cc + 1) * lc)
        lwc = lw[sl]
        cs = _dot_xr(tri, lwc)
        cs_end = cs[lc - 1:lc, :]
        e_in = jnp.exp(cs)
        e_out = jnp.exp(-cs)
        e_end = jnp.exp(cs_end - cs)
        rt_ref[sl, :] = r[sl] * e_in
        at_ref[sl, :] = av[sl] * jnp.exp(cs - lwc)
        bt_ref[sl, :] = bv[sl] * e_out
        kt_ref[sl, :] = k[sl] * e_out
        bh_ref[sl, :] = bv[sl] * e_end
        kh_ref[sl, :] = k[sl] * e_end
        pcs.append(jnp.exp(cs_end))
    pc_ref[...] = jnp.concatenate(pcs, axis=0)


def _rwkv_chunk_kernel(rt_ref, at_ref, bt_ref, kt_ref, v_ref, bh_ref, kh_ref, pc_ref,
                       rh_ref, yh_ref, phi_ref, psi_ref):
    lc = RWKV_CHUNK
    d = HEAD_DIM
    row = lax.broadcasted_iota(jnp.int32, (lc, lc), 0)
    col = lax.broadcasted_iota(jnp.int32, (lc, lc), 1)
    strict = row > col
    incl = row >= col
    eye = (row == col).astype(F32)
    n_ch = rt_ref.shape[0] // lc
    probs = [(cc, hh) for cc in range(n_ch) for hh in range(RWKV_HEADS)]

    def load(ref):
        return [ref[cc * lc:(cc + 1) * lc, hh * d:(hh + 1) * d] for cc, hh in probs]

    def each(fn, *lists):
        return [fn(*xs) for xs in zip(*lists)]

    rt, at, bt, kt, vv, bh, kh = (load(r) for r in (rt_ref, at_ref, bt_ref, kt_ref, v_ref, bh_ref, kh_ref))
    lowp = lambda xs: [x.astype(BF16) for x in xs]
    rtb, atb, btb, ktb, vvb, bhb, khb = (lowp(x) for x in (rt, at, bt, kt, vv, bh, kh))
    nt = lambda a, b: _dg(a, b, 1, 1)
    tn = lambda a, b: _dg(a, b, 0, 0)
    arb = each(lambda a, r: jnp.concatenate([a, r], axis=0), atb, rtb)
    gb = each(nt, arb, btb)
    gk = each(nt, arb, ktb)
    lab = each(lambda x: jnp.where(strict, x[:lc], 0.0), gb)
    lak = each(lambda x: jnp.where(strict, x[:lc], 0.0), gk)
    mrb = lowp(each(lambda x: jnp.where(incl, x[lc:], 0.0), gb))
    mrk = lowp(each(lambda x: jnp.where(incl, x[lc:], 0.0), gk))
    tinv = each(lambda l: eye + l, lab)
    lp = lowp(lab)
    lp = lowp(each(_dg, lp, lp))
    for _ in range(4):
        both = each(lambda l, t: _dg(jnp.concatenate([l, t.astype(BF16)], axis=0), l), lp, tinv)
        tinv = each(lambda t, x: t + x[lc:], tinv, both)
        lp = lowp(each(lambda x: x[:lc], both))
    tinv = lowp(each(lambda t, l: t + _dg(t.astype(BF16), l), tinv, lp))
    lv = lowp(each(_dg, lowp(lak), vvb))
    uh = lowp(each(_dg, tinv, lv))
    ah = lowp(each(_dg, tinv, atb))
    rh = each(lambda r, m, a: r + _dg(m, a), rt, mrb, ah)
    yh = each(lambda m, u, mk, v: _dg(m, u) + _dg(mk, v), mrb, uh, mrk, vvb)
    phi = each(lambda b, a: tn(b, a), bhb, ah)
    psi = each(lambda b, u, k, v: tn(b, u) + tn(k, v), bhb, uh, khb, vvb)
    for cc in range(n_ch):
        sl = slice(cc * lc, (cc + 1) * lc)
        ps = slice(cc * RWKV_HEADS, (cc + 1) * RWKV_HEADS)
        pc_row = pc_ref[cc]
        rh_ref[sl, :] = jnp.concatenate(rh[ps], axis=1)
        yh_ref[sl, :] = jnp.concatenate(yh[ps], axis=1)
        phi_ref[sl, :] = jnp.concatenate(phi[ps], axis=1) + jnp.concatenate([eye] * RWKV_HEADS, axis=1) * pc_row
        psi_ref[sl, :] = jnp.concatenate(psi[ps], axis=1)


def _rwkv_scan_kernel(rh_ref, yh_ref, phi_ref, psi_ref, bonus_ref, g_ref, gw_ref, gb_ref, bd_ref, o_ref, st_ref,
                      *, n_chunks):
    lc = RWKV_CHUNK
    d = HEAD_DIM

    @pl.when(pl.program_id(1) == 0)
    def _():
        st_ref[...] = jnp.zeros_like(st_ref)

    hsl = [slice(hh * d, (hh + 1) * d) for hh in range(RWKV_HEADS)]
    rows = []
    for cc in range(n_chunks):
        sl = slice(cc * lc, (cc + 1) * lc)
        stb = [st_ref[hh].astype(BF16) for hh in range(RWKV_HEADS)]
        ys = [_dg(rh_ref[sl, hs].astype(BF16), st) + yh_ref[sl, hs] for hs, st in zip(hsl, stb)]
        for hh, (hs, st) in enumerate(zip(hsl, stb)):
            st_ref[hh] = _dg(phi_ref[sl, hs].astype(BF16), st) + psi_ref[sl, hs]
        rows.append(jnp.concatenate(ys, axis=1))
    y = jnp.concatenate(rows, axis=0)
    bd = bd_ref[...]
    mean = _dot_xl(y, bd) * (1.0 / d)
    yc = y - mean
    var = _dot_xl(yc * yc, bd) * (1.0 / d)
    yn = yc * lax.rsqrt(var + RWKV_GN_EPS) * gw_ref[...] + gb_ref[...]
    o_ref[...] = (yn + bonus_ref[...]) * g_ref[...]


def rwkv7_mix(cols, mu, w0, w_up, a0, a_up, g_up, k_k, k_a, r_k, gn_w, gn_b, batch, seq, col_block=0):
    t = cols.shape[0]
    w = RWKV_WIDTH
    lc = RWKV_CHUNK
    tm = 512
    row2 = lambda x: x.reshape(1, -1)
    mu_p = jnp.pad(mu, (0, RWKV_COLS_PAD - RWKV_COLS)).reshape(1, -1)
    r0 = RWKV_DECAY_RANK
    r1 = r0 + RWKV_A_RANK
    r2 = r1 + RWKV_GATE_RANK
    wup_p = jnp.zeros((RWKV_LO_W, w), F32).at[0:r0].set(w_up)
    aup_p = jnp.zeros((RWKV_LO_W, w), F32).at[r0:r1].set(a_up)
    gup_p = jnp.zeros((RWKV_LO_W, w), F32).at[r1:r2].set(g_up)
    bd = jnp.asarray(np.kron(np.eye(RWKV_HEADS), np.ones((HEAD_DIM, HEAD_DIM))), BF16)
    tri = jnp.asarray(np.tril(np.ones((lc, lc))), BF16)
    full = lambda shape: pl.BlockSpec(shape, lambda i: (0,) * len(shape))
    tok = pl.BlockSpec((tm, w), lambda i: (i, 0))
    big = jax.ShapeDtypeStruct((t, w), F32)
    prep = functools.partial(_rwkv_prep_kernel, seq=seq, tm=tm)
    rt, at, bt, kt, v, bh, kh, pc, g, bonus = pl.pallas_call(
        prep,
        grid=(t // tm,),
        in_specs=[
            pl.BlockSpec((tm, RWKV_COLS_PAD), lambda i: (i, col_block)),
            pl.BlockSpec((8, RWKV_COLS_PAD), lambda i: (jnp.maximum(i * (tm // 8) - 1, 0), col_block)),
            full((1, RWKV_COLS_PAD)),
            full((1, w)), full((RWKV_LO_W, w)), full((1, w)), full((RWKV_LO_W, w)), full((RWKV_LO_W, w)),
            full((1, w)), full((1, w)), full((1, w)), full((w, w)), full((lc, lc)),
        ],
        out_specs=[tok, tok, tok, tok, tok, tok, tok, pl.BlockSpec((tm // lc, w), lambda i: (i, 0)), tok, tok],
        out_shape=[big, big, big, big, big, big, big, jax.ShapeDtypeStruct((t // lc, w), F32), big, big],
        compiler_params=_cp(("parallel",)),
        name="rwkv_prep",
    )(cols, cols, mu_p, row2(w0), wup_p, row2(a0), aup_p, gup_p, row2(k_k), row2(k_a), row2(r_k), bd, tri)

    n_ch = 4
    ch = pl.BlockSpec((n_ch * lc, w), lambda i: (i, 0))
    rh, yh, phi, psi = pl.pallas_call(
        _rwkv_chunk_kernel,
        grid=(t // (n_ch * lc),),
        in_specs=[ch, ch, ch, ch, ch, ch, ch, pl.BlockSpec((n_ch, 1, w), lambda i: (i, 0, 0))],
        out_specs=[ch, ch, ch, ch],
        out_shape=[big, big, big, big],
        compiler_params=_cp(("parallel",)),
        name="rwkv_chunk",
    )(rt, at, bt, kt, v, bh, kh, pc.reshape(t // lc, 1, w))

    n_chunks = 4
    ts = n_chunks * lc
    nsb = seq // ts
    blk = pl.BlockSpec((ts, w), lambda b, j: (b * nsb + j, 0))
    scan = functools.partial(_rwkv_scan_kernel, n_chunks=n_chunks)
    return pl.pallas_call(
        scan,
        grid=(batch, nsb),
        in_specs=[blk, blk, blk, blk, blk, blk,
                  pl.BlockSpec((1, w), lambda b, j: (0, 0)), pl.BlockSpec((1, w), lambda b, j: (0, 0)),
                  pl.BlockSpec((w, w), lambda b, j: (0, 0))],
        out_specs=blk,
        out_shape=big,
        scratch_shapes=[pltpu.VMEM((RWKV_HEADS, HEAD_DIM, HEAD_DIM), F32)],
        compiler_params=_cp(("parallel", "arbitrary")),
        name="rwkv_scan",
    )(rh, yh, phi, psi, bonus, g, row2(gn_w), row2(gn_b), bd)


NSA_Q_HEADS = 8
NSA_KV_HEADS = 2
NSA_GROUP = 4
NSA_WIDTH = 512
NSA_KV_WIDTH = 128
NSA_COLS = NSA_WIDTH + 6 * NSA_KV_WIDTH + 3 * NSA_Q_HEADS
CMP_BLOCK = 32
CMP_STRIDE = 16
CMP_HIDDEN = 128
SLC_BLOCK = 64
SLC_TOPN = 16
WINDOW = 512
FORCED_SCORE = 1e6
NSA_TQ = 128
NSA_CB_KC, NSA_CB_VC, NSA_CB_KS, NSA_CB_VS, NSA_CB_KW, NSA_CB_VW, NSA_CB_GATE = 4, 5, 6, 7, 8, 9, 10


def _rope_tables(pos, reps):
    half = ROPE_DIM // 2
    inv = jnp.power(ROPE_THETA, -jnp.arange(half, dtype=F32) * (2.0 / ROPE_DIM))
    ang = pos.astype(F32)[:, None] * inv[None, :]
    cos = jnp.cos(ang)
    sin = jnp.sin(ang)
    p = pos.shape[0]
    ones = jnp.ones((p, HEAD_DIM - ROPE_DIM), F32)
    zeros = jnp.zeros((p, HEAD_DIM - half), F32)
    cos_t = jnp.concatenate([cos, cos, ones], axis=1)
    sin_a = jnp.concatenate([-sin, zeros], axis=1)
    sin_b = jnp.concatenate([jnp.zeros((p, half), F32), sin, jnp.zeros((p, HEAD_DIM - ROPE_DIM), F32)], axis=1)
    tile = lambda x: jnp.tile(x, (1, reps))
    return tile(cos_t), tile(sin_a), tile(sin_b)


def _rope(x, cos_t, sin_a, sin_b):
    n = x.shape[1]
    half = ROPE_DIM // 2
    return x * cos_t + pltpu.roll(x, n - half, axis=1) * sin_a + pltpu.roll(x, half, axis=1) * sin_b


def _compress_kernel(xk_ref, xv_ref, pek_ref, w1k_ref, b1k_ref, w2k_ref, pev_ref, w1v_ref, b1v_ref, w2v_ref,
                     cos_ref, sa_ref, sb_ref, ko_ref, vo_ref):
    n_half = ko_ref.shape[1]
    half = CMP_BLOCK // 2

    def mlp(x_ref, pe_ref, w1_ref, b1_ref, w2_ref):
        xs = [x_ref[pl.ds(l, n_half, stride=CMP_STRIDE), :] for l in range(half)]
        top = [_dot3(x + pe_ref[l:l + 1, :], w1_ref[l]) for l, x in enumerate(xs)]
        bot = [_dot3(x + pe_ref[half + l:half + l + 1, :], w1_ref[half + l]) for l, x in enumerate(xs)]
        hid = sum(top[1:], top[0]) + pltpu.roll(sum(bot[1:], bot[0]), n_half - 1, axis=0) + b1_ref[...]
        return _dot3(_silu(hid), w2_ref[...])

    k = mlp(xk_ref, pek_ref, w1k_ref, b1k_ref, w2k_ref)
    ko_ref[0] = _rope(k, cos_ref[...], sa_ref[...], sb_ref[...])
    vo_ref[0] = mlp(xv_ref, pev_ref, w1v_ref, b1v_ref, w2v_ref)


def nsa_compress(cols, pe_k, w1_k, b1_k, w2_k, pe_v, w1_v, b1_v, w2_v, batch, seq):
    hk, d = NSA_KV_HEADS, HEAD_DIM
    n_half = seq // CMP_STRIDE
    cmp_end = jnp.arange(n_half) * CMP_STRIDE + (CMP_BLOCK - 1)
    cos_t, sin_a, sin_b = _rope_tables(cmp_end, hk)
    eye = jnp.eye(hk, dtype=F32)
    pair = lambda w: jnp.kron(eye, w)
    w1_pair = lambda w: jax.vmap(pair)(w.reshape(CMP_BLOCK, d, CMP_HIDDEN))
    lanes2 = lambda v: jnp.tile(v, (1, hk))
    full = lambda shape: pl.BlockSpec(shape, lambda b: (0,) * len(shape))
    xspec = lambda cb: pl.BlockSpec((seq, hk * d), lambda b: (b, cb))
    ospec = pl.BlockSpec((1, n_half, hk * d), lambda b: (b, 0, 0))
    wspecs = [full((CMP_BLOCK, hk * d)), full((CMP_BLOCK, hk * d, hk * CMP_HIDDEN)), full((1, hk * CMP_HIDDEN)),
              full((hk * CMP_HIDDEN, hk * d))]
    oshape = jax.ShapeDtypeStruct((batch, n_half, hk * d), F32)
    return pl.pallas_call(
        _compress_kernel,
        grid=(batch,),
        in_specs=[xspec(NSA_CB_KC), xspec(NSA_CB_VC)] + wspecs + wspecs + [full((n_half, hk * d))] * 3,
        out_specs=[ospec, ospec],
        out_shape=[oshape, oshape],
        compiler_params=_cp(("parallel",)),
        name="nsa_compress",
    )(cols, cols, lanes2(pe_k), w1_pair(w1_k), lanes2(b1_k.reshape(1, -1)), pair(w2_k),
      lanes2(pe_v), w1_pair(w1_v), lanes2(b1_v.reshape(1, -1)), pair(w2_v), cos_t, sin_a, sin_b)


def _rope_t(x, cos_t, sin_a, sin_b):
    half = ROPE_DIM // 2
    up = jnp.concatenate([x[half:], x[:half]], axis=0)
    down = jnp.concatenate([x[-half:], x[:-half]], axis=0)
    return x * cos_t + up * sin_a + down * sin_b


def _nsa_attn_kernel(q_ref, ks_ref, vs_ref, kw_ref, vw_ref, gate_ref, kc_ref, vc_ref,
                     cosk_ref, sak_ref, sbk_ref, cosq_ref, saq_ref, sbq_ref, ovl_ref, get_ref,
                     o_ref, ks_s, vs_s, kw_s, vw_s, mem_s, *, seq):
    i = pl.program_id(1)
    tq = NSA_TQ
    d = HEAD_DIM
    g = NSA_GROUP
    cols = g * tq
    n_kv = NSA_KV_HEADS
    n_tiles = seq // tq

    @pl.when(i == 0)
    def _():
        ck, sa, sb = cosk_ref[...], sak_ref[...], sbk_ref[...]
        ks = _rope(ks_ref[...], ck, sa, sb)
        kw = _rope(kw_ref[...], ck, sa, sb)
        for hk in range(n_kv):
            hs = slice(hk * d, (hk + 1) * d)
            ks_s[hk] = ks[:, hs].astype(BF16)
            kw_s[hk] = kw[:, hs].astype(BF16)
        for j in range(n_tiles):
            rs = slice(j * tq, (j + 1) * tq)
            vs_s[j] = vs_ref[rs, :].T.astype(BF16)
            vw_s[j] = vw_ref[rs, :].T.astype(BF16)

    qt = _rope_t(q_ref[...].T, cosq_ref[...], saq_ref[...], sbq_ref[...]) * (HEAD_DIM ** -0.5)
    gexp = _dot_xr(get_ref[...], _sigmoid(gate_ref[...].T))

    pos_q = i * tq + (lax.broadcasted_iota(jnp.int32, (1, cols), 1) & (tq - 1))
    sub = lax.broadcasted_iota(jnp.int32, (tq, cols), 0)
    blk = lax.broadcasted_iota(jnp.int32, (32, tq), 0)
    cur = (i * tq + lax.broadcasted_iota(jnp.int32, (32, tq), 1)) // SLC_BLOCK
    sel_valid = blk <= cur
    sel_forced = (blk == 0) | (blk == cur) | (blk == cur - 1)

    q4 = [jnp.concatenate([qt[(hk * g + gg) * d:(hk * g + gg + 1) * d, :] for gg in range(g)], axis=1)
          for hk in range(n_kv)]
    q4b = [(x * math.log2(math.e)).astype(BF16) for x in q4]

    cend = sub * CMP_STRIDE + (CMP_BLOCK - 1)
    cmask = (cend <= pos_q) & (sub < seq // CMP_STRIDE - 1)
    o_cmp = []
    for hk in range(n_kv):
        s = _dot3(kc_ref[0, :, hk * d:(hk + 1) * d], q4[hk])
        sm = jnp.where(cmask, s, MASK_NEG)
        e = jnp.exp(sm - jnp.max(sm, axis=0, keepdims=True))
        p = e / jnp.sum(e, axis=0, keepdims=True) * cmask.astype(F32)
        o_cmp.append(_dot1(vc_ref[0, :, hk * d:(hk + 1) * d], p, 0, 0))
        psum = p[:, 0:tq] + p[:, tq:2 * tq] + p[:, 2 * tq:3 * tq] + p[:, 3 * tq:4 * tq]
        imp = _dot_xr(ovl_ref[...], psum)
        imp = jnp.where(sel_forced, FORCED_SCORE, jnp.where(sel_valid, imp, -FORCED_SCORE))
        members = []
        for jj in range(32):
            rj = imp[jj:jj + 1, :]
            beats = (imp > rj) | ((imp == rj) & (blk < jj))
            cnt = jnp.sum(beats.astype(F32), axis=0, keepdims=True)
            members.append((cnt < SLC_TOPN).astype(F32))
        member = jnp.concatenate(members + [jnp.zeros((8, tq), F32)], axis=0)
        mem_s[hk] = jnp.concatenate([member] * g, axis=1)

    def flash(k_s, vt_s, j_lo, mask_fn):
        def scores(j):
            jl = jnp.minimum(j, n_tiles - 1)
            ks = [k_s[hk, pl.ds(jl * tq, tq), :] for hk in range(n_kv)]
            return [_dg(ks[hk], q4b[hk]) for hk in range(n_kv)], vt_s[jl]

        def update(j, ss, vt, carry):
            rel = (pos_q - j * tq) - sub
            new = []
            ps = []
            for hk in range(n_kv):
                m_run, l_run, acc = carry[hk]
                s = jnp.where(mask_fn(hk, j, rel), ss[hk], MASK_NEG)
                m_new = jnp.maximum(m_run, jnp.max(s, axis=0, keepdims=True))
                alpha = jnp.exp2(m_run - m_new)
                p = jnp.exp2(s - m_new)
                l_new = alpha * l_run + jnp.sum(p, axis=0, keepdims=True)
                ps.append(p.astype(BF16))
                new.append((m_new, l_new, alpha * acc))
            pv = [_dg(vt[hk * d:(hk + 1) * d, :], ps[hk]) for hk in range(n_kv)]
            return tuple((m, l, a + o) for (m, l, a), o in zip(new, pv))

        def body(pair, carry):
            ja = 2 * pair
            ssa, vta = scores(ja)
            ssb, vtb = scores(ja + 1)
            return update(ja + 1, ssb, vtb, update(ja, ssa, vta, carry))

        one = (jnp.full((1, cols), MASK_NEG, F32), jnp.zeros((1, cols), F32), jnp.zeros((d, cols), F32))
        fin = lax.fori_loop(j_lo // 2, i // 2 + 1, body, (one,) * n_kv)
        return [acc / l_fin for _, l_fin, acc in fin]

    first_block = sub < SLC_BLOCK

    def slc_mask(hk, j, rel):
        m0 = mem_s[hk, pl.ds(2 * j, 1), :]
        m1 = mem_s[hk, pl.ds(2 * j + 1, 1), :]
        return (jnp.where(first_block, m0, m1) > 0.5) & (rel >= 0)

    def win_mask(hk, j, rel):
        return (rel >= 0) & (rel < WINDOW)

    o_slc = flash(ks_s, vs_s, 0, slc_mask)
    o_win = flash(kw_s, vw_s, jnp.maximum(i - WINDOW // tq, 0), win_mask)

    outs = []
    for hk in range(n_kv):
        for gg in range(g):
            h = hk * g + gg
            cs = slice(gg * tq, (gg + 1) * tq)
            ge = lambda br: gexp[br * NSA_WIDTH + h * d: br * NSA_WIDTH + (h + 1) * d, :]
            outs.append(ge(0) * o_cmp[hk][:, cs] + ge(1) * o_slc[hk][:, cs] + ge(2) * o_win[hk][:, cs])
    o_ref[...] = jnp.concatenate(outs, axis=0).T


def nsa_mix(cols, pe_k, w1_k, b1_k, w2_k, pe_v, w1_v, b1_v, w2_v, batch, seq):
    t = cols.shape[0]
    tq = NSA_TQ
    nq = seq // tq
    d = HEAD_DIM
    k_cmp, v_cmp = nsa_compress(cols, pe_k, w1_k, b1_k, w2_k, pe_v, w1_v, b1_v, w2_v, batch, seq)
    pos = jnp.arange(seq)
    cos_k, sa_k, sb_k = _rope_tables(pos, 2)
    cos_q, sa_q, sb_q = (x.T for x in _rope_tables(pos, NSA_Q_HEADS))
    n_slc = seq // SLC_BLOCK
    n_half = seq // CMP_STRIDE
    blk_tok = np.arange(n_half - 1)[:, None] * CMP_STRIDE + np.arange(CMP_BLOCK)[None, :]
    overlap = np.any((blk_tok[:, :, None] // SLC_BLOCK) == np.arange(n_slc)[None, None, :], axis=1)
    ovl_t = np.zeros((32, 128), np.float32)
    ovl_t[:n_slc, :n_half - 1] = overlap.T
    gate_e = np.zeros((3 * NSA_WIDTH, 128), np.float32)
    for h in range(NSA_Q_HEADS):
        for br in range(3):
            gate_e[br * NSA_WIDTH + h * d: br * NSA_WIDTH + (h + 1) * d, h * 3 + br] = 1.0
    kvspec = lambda cb: pl.BlockSpec((seq, 128), lambda b, i: (b, cb))
    full = lambda shape: pl.BlockSpec(shape, lambda b, i: (0,) * len(shape))
    cspec = pl.BlockSpec((1, n_half, NSA_KV_WIDTH), lambda b, i: (b, 0, 0))
    kern = functools.partial(_nsa_attn_kernel, seq=seq)
    return pl.pallas_call(
        kern,
        grid=(batch, nq),
        in_specs=[
            pl.BlockSpec((tq, NSA_WIDTH), lambda b, i: (b * nq + i, 0)),
            kvspec(NSA_CB_KS), kvspec(NSA_CB_VS), kvspec(NSA_CB_KW), kvspec(NSA_CB_VW),
            pl.BlockSpec((tq, 128), lambda b, i: (b * nq + i, NSA_CB_GATE)),
            cspec, cspec,
            full((seq, 128)), full((seq, 128)), full((seq, 128)),
            pl.BlockSpec((NSA_WIDTH, tq), lambda b, i: (0, i)),
            pl.BlockSpec((NSA_WIDTH, tq), lambda b, i: (0, i)),
            pl.BlockSpec((NSA_WIDTH, tq), lambda b, i: (0, i)),
            full((32, 128)), full((3 * NSA_WIDTH, 128)),
        ],
        out_specs=pl.BlockSpec((tq, NSA_WIDTH), lambda b, i: (b * nq + i, 0)),
        out_shape=jax.ShapeDtypeStruct((t, NSA_WIDTH), F32),
        scratch_shapes=[
            pltpu.VMEM((NSA_KV_HEADS, seq, d), BF16),
            pltpu.VMEM((nq, NSA_KV_WIDTH, tq), BF16),
            pltpu.VMEM((NSA_KV_HEADS, seq, d), BF16),
            pltpu.VMEM((nq, NSA_KV_WIDTH, tq), BF16),
            pltpu.VMEM((NSA_KV_HEADS, 40, NSA_GROUP * tq), F32),
        ],
        compiler_params=_cp(("parallel", "arbitrary")),
        name="nsa_attn",
    )(cols, cols, cols, cols, cols, cols, k_cmp, v_cmp, cos_k, sa_k, sb_k, cos_q, sa_q, sb_q,
      jnp.asarray(ovl_t, BF16), jnp.asarray(gate_e, BF16))


HYB_NSA_PAD = 1792
HYB_COLS_PAD = HYB_NSA_PAD + RWKV_COLS_PAD


def _mm2_res_kernel(a1_ref, a2_ref, w1_ref, w2_ref, r_ref, o_ref):
    acc = jnp.dot(a1_ref[...].astype(BF16), w1_ref[...], preferred_element_type=F32)
    acc = acc + jnp.dot(a2_ref[...].astype(BF16), w2_ref[...], preferred_element_type=F32)
    o_ref[...] = r_ref[...] + acc


def matmul2_residual(a1, a2, w_bf16, res, tm, tn):
    t, k1 = a1.shape
    k2 = a2.shape[1]
    n = w_bf16.shape[1]
    return pl.pallas_call(
        _mm2_res_kernel,
        grid=(t // tm, n // tn),
        in_specs=[
            pl.BlockSpec((tm, k1), lambda i, j: (i, 0)),
            pl.BlockSpec((tm, k2), lambda i, j: (i, 0)),
            pl.BlockSpec((k1, tn), lambda i, j: (0, j)),
            pl.BlockSpec((k2, tn), lambda i, j: (k1 // k2, j)),
            pl.BlockSpec((tm, tn), lambda i, j: (i, j)),
        ],
        out_specs=pl.BlockSpec((tm, tn), lambda i, j: (i, j)),
        out_shape=jax.ShapeDtypeStruct((t, n), F32),
        compiler_params=_cp(("parallel", "arbitrary")),
        name="matmul2_residual",
    )(a1, a2, w_bf16, w_bf16, res)


def hybrid_block(h, g_norm, w_in, w_out, rwkv_params, nsa_params, batch, seq):
    w_rwkv = w_in[:, :RWKV_COLS]
    w_nsa = w_in[:, RWKV_COLS:]
    w_p = jnp.concatenate([
        jnp.pad(w_nsa, ((0, 0), (0, HYB_NSA_PAD - NSA_COLS))),
        jnp.pad(w_rwkv, ((0, 0), (0, RWKV_COLS_PAD - RWKV_COLS))),
    ], axis=1).astype(BF16)
    cols = norm_matmul(h, g_norm, w_p, tm=512, tn=896)
    y_rwkv = rwkv7_mix(cols, *rwkv_params, batch=batch, seq=seq, col_block=1)
    y_nsa = nsa_mix(cols, *nsa_params, batch=batch, seq=seq)
    return matmul2_residual(y_rwkv, y_nsa, w_out.astype(BF16), h, tm=1024, tn=512)


def kernel(x, norm_mix, norm_ffn, norm_final, hyb_w_in, hyb_w_out, rwkv_mu, rwkv_w0, rwkv_w_up, rwkv_a0, rwkv_a_up, rwkv_g_up, rwkv_k_k, rwkv_k_a, rwkv_r_k, rwkv_gn_w, rwkv_gn_b, nsa_pe_k, nsa_w1_k, nsa_b1_k, nsa_w2_k, nsa_pe_v, nsa_w1_v, nsa_b1_v, nsa_w2_v, ssm_w_in, ssm_conv_w, ssm_conv_b, ssm_dt_bias, ssm_a_log, ssm_d, ssm_norm_w, ssm_w_out, moe_wg, moe_bg, moe_we, moe_be, moe_w_gate, moe_w_up, moe_w_down):
    batch, seq, d = x.shape
    depth = norm_mix.shape[0]
    h = x.reshape(batch * seq, d)
    for layer in range(depth):
        i = layer // 2
        if layer % 2 == 0:
            rwkv_params = (rwkv_mu[i], rwkv_w0[i], rwkv_w_up[i], rwkv_a0[i], rwkv_a_up[i], rwkv_g_up[i],
                           rwkv_k_k[i], rwkv_k_a[i], rwkv_r_k[i].reshape(-1), rwkv_gn_w[i], rwkv_gn_b[i])
            nsa_params = (nsa_pe_k[i], nsa_w1_k[i], nsa_b1_k[i], nsa_w2_k[i],
                          nsa_pe_v[i], nsa_w1_v[i], nsa_b1_v[i], nsa_w2_v[i])
            h = hybrid_block(h, norm_mix[layer], hyb_w_in[i], hyb_w_out[i], rwkv_params, nsa_params, batch, seq)
        else:
            h = mamba2_block(h, norm_mix[layer], ssm_w_in[i], ssm_conv_w[i], ssm_conv_b[i], ssm_dt_bias[i],
                             ssm_a_log[i], ssm_d[i], ssm_norm_w[i], ssm_w_out[i], batch, seq)
        h = hier_moe_block(h, norm_ffn[layer], moe_wg[layer], moe_bg[layer], moe_we[layer], moe_be[layer],
                           moe_w_gate, moe_w_up, moe_w_down, layer,
                           g_final=norm_final if layer == depth - 1 else None)
    return h.reshape(batch, seq, d)
```
